```python
import math
import jax, jax.numpy as jnp
from jax import lax
import numpy as np

D_MODEL = 1024
BATCH = 2
SEQ = 16384
DEPTH = 1
DEC_BATCH = 32
DEC_SEQ = 16
PAST_LEN = 1024

CHUNK = 64
Q_BLOCK = 128
RMS_EPS = 1e-6
ATTN_WIDTH = D_MODEL // 2
A_HEAD_DIM = 64
A_HEADS = ATTN_WIDTH // (2 * A_HEAD_DIM)
QK_A = A_HEADS * 2 * A_HEAD_DIM
REC_WIDTH = D_MODEL - ATTN_WIDTH
R_HEADS = 4
R_DV = REC_WIDTH // R_HEADS
R_DK = 128
FK_R = R_HEADS * R_DK
SPLITS = list(np.cumsum([QK_A, QK_A, ATTN_WIDTH, FK_R, FK_R, REC_WIDTH])[:].tolist())
IN_COLS = 2 * QK_A + ATTN_WIDTH + 2 * FK_R + 2 * REC_WIDTH
MIX_WIDTH = ATTN_WIDTH + REC_WIDTH
N_GROUPS = 4
EXPERTS_PER_GROUP = 8
N_EXPERTS = N_GROUPS * EXPERTS_PER_GROUP
TOP_K = 2
D_EXPERT = 512
ROW_BLOCK = 128

kernel_name = "hymba_diffattn_hgrn2_hmoe_stream"

F32 = jnp.float32


def rmsnorm(x, w):
    xf = x.astype(F32)
    y = xf * lax.rsqrt(jnp.mean(xf * xf, axis=-1, keepdims=True) + RMS_EPS)
    return (y * w.astype(F32)).astype(x.dtype)


def split_proj(h, w_in, lb):
    B, L, _ = h.shape
    z = h @ w_in
    qa, ka, va, qr, fr, ir, gr = jnp.split(z, SPLITS, axis=-1)
    qa = qa.reshape(B, L, A_HEADS, 2, A_HEAD_DIM)
    ka = ka.reshape(B, L, A_HEADS, 2, A_HEAD_DIM)
    va = va.reshape(B, L, A_HEADS, 2 * A_HEAD_DIM)
    qr = jax.nn.silu(qr.astype(F32)).reshape(B, L, R_HEADS, R_DK)
    f = lb + (1.0 - lb) * jax.nn.sigmoid(fr.astype(F32).reshape(B, L, R_HEADS, R_DK))
    logf = jnp.log(f)
    kr = 1.0 - f
    ir = ir.astype(F32).reshape(B, L, R_HEADS, R_DV)
    gr = gr.reshape(B, L, R_HEADS, R_DV)
    return qa, ka, va, qr, kr, logf, ir, gr


def diff_attention(q, k, v, q_pos, k_pos, lam):
    s = jnp.einsum('bqhcd,bkhcd->bhcqk', q.astype(F32), k.astype(F32)) * (A_HEAD_DIM ** -0.5)
    visible = (k_pos[None, :] // CHUNK) <= (q_pos[:, None] // CHUNK)
    s = jnp.where(visible, s, -jnp.inf)
    p = jax.nn.softmax(s, axis=-1)
    w = p[:, :, 0] - lam * p[:, :, 1]
    return jnp.einsum('bhqk,bkhe->bqhe', w, v.astype(F32))


def attn_prompt(qa, ka, va, lam):
    B, S = qa.shape[:2]
    nqb = S // Q_BLOCK
    qb = qa.reshape(B, nqb, Q_BLOCK, A_HEADS, 2, A_HEAD_DIM).swapaxes(0, 1)
    k_pos = jnp.arange(S)

    def one(args):
        q_blk, bi = args
        q_pos = bi * Q_BLOCK + jnp.arange(Q_BLOCK)
        return diff_attention(q_blk, ka, va, q_pos, k_pos, lam)

    o = lax.map(one, (qb, jnp.arange(nqb)))
    return o.swapaxes(0, 1).reshape(B, S, A_HEADS, 2 * A_HEAD_DIM)


def hgrn_block(q, k, v, logf, S0):
    L = q.shape[1]
    b = jnp.cumsum(logf, axis=1)
    causal = jnp.tril(jnp.ones((L, L), dtype=bool))
    diff = b[:, :, None] - b[:, None, :]
    decay = jnp.exp(jnp.where(causal[None, :, :, None, None], diff, -jnp.inf))
    A = jnp.einsum('bthk,btshk,bshk->bhts', q, decay, k)
    o = jnp.einsum('bhts,bshv->bthv', A, v) + jnp.einsum('bthk,bhkv->bthv', q * jnp.exp(b), S0)
    bL = b[:, -1]
    S_new = jnp.exp(bL)[..., None] * S0 + jnp.einsum('bshk,bshv->bhkv', k * jnp.exp(bL[:, None] - b), v)
    return o, S_new


def hgrn_prompt(q, k, v, logf):
    B, L = q.shape[:2]
    nc = L // CHUNK

    def to_chunks(a):
        return a.reshape(B, nc, CHUNK, *a.shape[2:]).swapaxes(0, 1)

    def step(S, inp):
        qc, kc, vc, lc = inp
        o, S = hgrn_block(qc, kc, vc, lc, S)
        return S, o

    S0 = jnp.zeros((B, R_HEADS, R_DK, R_DV), F32)
    S, o = lax.scan(step, S0, (to_chunks(q), to_chunks(k), to_chunks(v), to_chunks(logf)))
    return o.swapaxes(0, 1).reshape(B, L, R_HEADS, R_DV), S


def merge_groups(oa, orec, gr, subln_w, gnorm_w, w_out, lam_init, dtype):
    B, L = oa.shape[:2]
    a = (rmsnorm(oa, subln_w) * (1.0 - lam_init)).reshape(B, L, ATTN_WIDTH).astype(dtype)
    r = (rmsnorm(orec, gnorm_w) * jax.nn.silu(gr.astype(F32))).reshape(B, L, REC_WIDTH).astype(dtype)
    return jnp.concatenate([a, r], axis=-1) @ w_out


def grouped_experts(t, experts, gates, wg, wu, wd):
    N, K = experts.shape
    M = N * K
    flat_e = experts.reshape(-1)
    flat_tok = jnp.repeat(jnp.arange(N, dtype=jnp.int32), K)
    flat_g = gates.reshape(-1)
    order = jnp.argsort(flat_e)
    se = flat_e[order]
    counts = jnp.bincount(flat_e, length=N_EXPERTS)
    starts = jnp.cumsum(counts) - counts
    pcounts = (counts + ROW_BLOCK - 1) // ROW_BLOCK * ROW_BLOCK
    pends = jnp.cumsum(pcounts)
    pstarts = pends - pcounts
    dest = pstarts[se] + jnp.arange(M) - starts[se]
    n_blocks = -(-M // ROW_BLOCK) + N_EXPERTS
    R = n_blocks * ROW_BLOCK
    row_tok = jnp.full((R,), N, jnp.int32).at[dest].set(flat_tok[order])
    row_gate = jnp.zeros((R,), flat_g.dtype).at[dest].set(flat_g[order])
    blk_exp = jnp.minimum(jnp.searchsorted(pends, jnp.arange(n_blocks) * ROW_BLOCK, side='right'), N_EXPERTS - 1)
    t_pad = jnp.concatenate([t, jnp.zeros((1, t.shape[1]), t.dtype)], axis=0)
    xs = t_pad[row_tok].reshape(n_blocks, ROW_BLOCK, t.shape[1])

    def run(args):
        xb, e = args
        return (jax.nn.silu(xb @ wg[e]) * (xb @ wu[e])) @ wd[e]

    ys = lax.map(run, (xs, blk_exp)).reshape(R, t.shape[1])
    ys = ys * row_gate[:, None].astype(ys.dtype)
    return jax.ops.segment_sum(ys, row_tok, num_segments=N + 1)[:N]


def hier_moe(h, w_group, w_router, wg, wu, wd):
    shp = h.shape
    t = h.reshape(-1, shp[-1])
    N = t.shape[0]
    g_logits = (t @ w_group).astype(F32)
    g_prob = jax.nn.softmax(g_logits, axis=-1)
    g_sel = jnp.argmax(g_logits, axis=-1)
    p_group = jnp.take_along_axis(g_prob, g_sel[:, None], axis=-1)
    e_logits = (t @ w_router).astype(F32).reshape(N, N_GROUPS, EXPERTS_PER_GROUP)
    e_logits = jnp.take_along_axis(e_logits, g_sel[:, None, None], axis=1)[:, 0]
    top_v, top_i = lax.top_k(e_logits, TOP_K)
    gates = p_group * jax.nn.softmax(top_v, axis=-1)
    experts = (g_sel[:, None] * EXPERTS_PER_GROUP + top_i).astype(jnp.int32)
    return grouped_experts(t, experts, gates, wg, wu, wd).reshape(shp)


def setup_inputs(seed: int = 0) -> dict:
    key = jax.random.key(seed)
    ks = jax.random.split(key, 24)
    nrm = lambda k, s, sc: jax.random.normal(k, s, F32) * sc
    gain = lambda k, s: 1.0 + 0.02 * jax.random.normal(k, s, F32)
    return {
        "x_prompt": nrm(ks[0], (BATCH, SEQ, D_MODEL), 1.0),
        "x_sample": nrm(ks[1], (DEC_BATCH, DEC_SEQ, D_MODEL), 1.0),
        "cache_k": nrm(ks[2], (DEPTH, DEC_BATCH, PAST_LEN, A_HEADS, 2 * A_HEAD_DIM), 1.0),
        "cache_v": nrm(ks[3], (DEPTH, DEC_BATCH, PAST_LEN, A_HEADS, 2 * A_HEAD_DIM), 1.0),
        "state_rec": nrm(ks[4], (DEPTH, DEC_BATCH, R_HEADS, R_DK, R_DV), 0.5),
        "w_in": nrm(ks[5], (DEPTH, D_MODEL, IN_COLS), D_MODEL ** -0.5),
        "lam_q1": nrm(ks[6], (DEPTH, A_HEAD_DIM), 0.1),
        "lam_k1": nrm(ks[7], (DEPTH, A_HEAD_DIM), 0.1),
        "lam_q2": nrm(ks[8], (DEPTH, A_HEAD_DIM), 0.1),
        "lam_k2": nrm(ks[9], (DEPTH, A_HEAD_DIM), 0.1),
        "subln_w": gain(ks[10], (DEPTH, 2 * A_HEAD_DIM)),
        "lb_param": nrm(ks[11], (DEPTH + 1, FK_R), 0.5),
        "gnorm_w": gain(ks[12], (DEPTH, R_DV)),
        "w_out": nrm(ks[13], (DEPTH, MIX_WIDTH, D_MODEL), MIX_WIDTH ** -0.5),
        "norm1_w": gain(ks[14], (DEPTH, D_MODEL)),
        "norm2_w": gain(ks[15], (DEPTH, D_MODEL)),
        "w_group": nrm(ks[16], (DEPTH, D_MODEL, N_GROUPS), D_MODEL ** -0.5),
        "w_router": nrm(ks[17], (DEPTH, D_MODEL, N_EXPERTS), D_MODEL ** -0.5),
        "w_e_gate": nrm(ks[18], (DEPTH, N_EXPERTS, D_MODEL, D_EXPERT), D_MODEL ** -0.5),
        "w_e_up": nrm(ks[19], (DEPTH, N_EXPERTS, D_MODEL, D_EXPERT), D_MODEL ** -0.5),
        "w_e_down": nrm(ks[20], (DEPTH, N_EXPERTS, D_EXPERT, D_MODEL), D_EXPERT ** -0.5),
        "final_w": gain(ks[21], (D_MODEL,)),
    }


def reference(x_prompt, x_sample, cache_k, cache_v, state_rec, w_in, lam_q1, lam_k1, lam_q2, lam_k2,
              subln_w, lb_param, gnorm_w, w_out, norm1_w, norm2_w, w_group, w_router,
              w_e_gate, w_e_up, w_e_down, final_w):
    lbs = jnp.cumsum(jax.nn.softmax(lb_param.astype(F32), axis=0), axis=0)
    yp, ys = x_prompt, x_sample
    Bp, S = x_prompt.shape[:2]
    Bs, T = x_sample.shape[:2]
    kp_l, vp_l, sp_l, ks_l, vs_l, ss_l = [], [], [], [], [], []
    for l in range(DEPTH):
        lam_init = 0.8 - 0.6 * math.exp(-0.3 * l)
        lam = (jnp.exp(jnp.sum(lam_q1[l].astype(F32) * lam_k1[l].astype(F32)))
               - jnp.exp(jnp.sum(lam_q2[l].astype(F32) * lam_k2[l].astype(F32))) + lam_init)
        lb = lbs[l].reshape(R_HEADS, R_DK)

        h = rmsnorm(yp, norm1_w[l])
        qa, ka, va, qr, kr, logf, ir, gr = split_proj(h, w_in[l], lb)
        oa = attn_prompt(qa, ka, va, lam)
        orec, s_p = hgrn_prompt(qr, kr, ir, logf)
        yp = yp + merge_groups(oa, orec, gr, subln_w[l], gnorm_w[l], w_out[l], lam_init, yp.dtype)
        yp = yp + hier_moe(rmsnorm(yp, norm2_w[l]), w_group[l], w_router[l], w_e_gate[l], w_e_up[l], w_e_down[l])
        kp_l.append(ka.reshape(Bp, S, A_HEADS, 2 * A_HEAD_DIM))
        vp_l.append(va)
        sp_l.append(s_p)

        h = rmsnorm(ys, norm1_w[l])
        qa, ka, va, qr, kr, logf, ir, gr = split_proj(h, w_in[l], lb)
        k_all = jnp.concatenate([cache_k[l].astype(ka.dtype).reshape(Bs, -1, A_HEADS, 2, A_HEAD_DIM), ka], axis=1)
        v_all = jnp.concatenate([cache_v[l].astype(va.dtype), va], axis=1)
        P = cache_k.shape[2]
        oa = diff_attention(qa, k_all, v_all, P + jnp.arange(T), jnp.arange(P + T), lam)
        orec, s_s = hgrn_block(qr, kr, ir, logf, state_rec[l].astype(F32))
        ys = ys + merge_groups(oa, orec, gr, subln_w[l], gnorm_w[l], w_out[l], lam_init, ys.dtype)
        ys = ys + hier_moe(rmsnorm(ys, norm2_w[l]), w_group[l], w_router[l], w_e_gate[l], w_e_up[l], w_e_down[l])
        ks_l.append(ka.reshape(Bs, T, A_HEADS, 2 * A_HEAD_DIM))
        vs_l.append(va)
        ss_l.append(s_s)

    y_prompt = rmsnorm(yp, final_w)
    y_sample = rmsnorm(ys, final_w)
    return (y_prompt, y_sample, jnp.stack(kp_l), jnp.stack(vp_l), jnp.stack(sp_l),
            jnp.stack(ks_l), jnp.stack(vs_l), jnp.stack(ss_l))
```

```python
import functools
import math

import numpy as np
import jax
import jax.numpy as jnp
from jax import lax
from jax.experimental import pallas as pl
from jax.experimental.pallas import tpu as pltpu

F32 = jnp.float32
BF16 = jnp.bfloat16

D_MODEL = 1024
RMS_EPS = 1e-6
CHUNK = 64
LOG2_CHUNK = 6
A_HEADS = 4
A_HEAD_DIM = 64
HEAD_W = 2 * A_HEAD_DIM
R_HEADS = 4
SEG_W = 512
N_SEG = 7
N_GROUPS = 4
EXPERTS_PER_GROUP = 8
N_EXPERTS = N_GROUPS * EXPERTS_PER_GROUP
TOP_K = 2
D_EXPERT = 512
ROW_BLOCK = 128
ROUTE_W = 128
LAM_INIT = 0.8 - 0.6 * math.exp(-0.3 * 0)
VMEM_LIMIT = 56 * 1024 * 1024


def _sigmoid(x):
    return 1.0 / (1.0 + jnp.exp(-x))


def _dot(a, b):
    return jnp.dot(a, b, preferred_element_type=F32)


def _dot_nt(a, b):
    return lax.dot_general(a, b, (((1,), (1,)), ((), ())), preferred_element_type=F32)


def _dot_tn(a, b):
    return lax.dot_general(a, b, (((0,), (0,)), ((), ())), preferred_element_type=F32)


def _rms(x, w):
    return x * lax.rsqrt(jnp.mean(x * x, axis=-1, keepdims=True) + RMS_EPS) * w


def _inproj_kernel(x_ref, n1_ref, w_ref, lbp_ref, q_ref, kf_ref, vf_ref, kb_ref, vb_ref,
                   qr_ref, logf_ref, kr_ref, ir_ref, gr_ref):
    h = _rms(x_ref[...], n1_ref[...]).astype(BF16)

    def seg(i):
        return _dot(h, w_ref[:, i * SEG_W:(i + 1) * SEG_W])

    q_ref[...] = (seg(0) * (A_HEAD_DIM ** -0.5)).astype(BF16)
    k = seg(1)
    kf_ref[...] = k
    kb_ref[...] = k.astype(BF16)
    v = seg(2)
    vf_ref[...] = v
    vb_ref[...] = v.astype(BF16)
    qr = seg(3)
    qr_ref[...] = qr * _sigmoid(qr)
    p = lbp_ref[...]
    e = jnp.exp(p - jnp.max(p, axis=0, keepdims=True))
    lb = e[0:1] / jnp.sum(e, axis=0, keepdims=True)
    f = lb + (1.0 - lb) * _sigmoid(seg(4))
    logf_ref[...] = jnp.log(f)
    kr_ref[...] = 1.0 - f
    ir_ref[...] = seg(5).astype(BF16)
    gr_ref[...] = seg(6)


def _inproj(x, n1, w_bf, lbp, tm):
    n = x.shape[0]
    row = lambda i: (i, 0)
    fix = lambda i: (0, 0)
    out = lambda dt: jax.ShapeDtypeStruct((n, SEG_W), dt)
    ospec = pl.BlockSpec((tm, SEG_W), row)
    return pl.pallas_call(
        _inproj_kernel,
        grid=(n // tm,),
        in_specs=[pl.BlockSpec((tm, D_MODEL), row), pl.BlockSpec((1, D_MODEL), fix),
                  pl.BlockSpec((D_MODEL, N_SEG * SEG_W), fix), pl.BlockSpec(lbp.shape, fix)],
        out_specs=[ospec] * 10,
        out_shape=[out(BF16), out(F32), out(F32), out(BF16), out(BF16),
                   out(F32), out(F32), out(F32), out(BF16), out(F32)],
        compiler_params=pltpu.CompilerParams(dimension_semantics=("arbitrary",),
                                             vmem_limit_bytes=VMEM_LIMIT),
        name="inproj",
    )(x, n1, w_bf, lbp)


def _lam_value(lam_ref):
    l = lam_ref[...]
    s1 = jnp.sum(l[0:1] * l[1:2], axis=-1, keepdims=True)
    s2 = jnp.sum(l[2:3] * l[3:4], axis=-1, keepdims=True)
    return jnp.exp(s1) - jnp.exp(s2) + LAM_INIT


def _split_maps(q):
    lane = lax.broadcasted_iota(jnp.int32, q.shape, 1)
    zero = jnp.zeros_like(q)
    return jnp.concatenate([jnp.where(lane < A_HEAD_DIM, q, zero),
                            jnp.where(lane >= A_HEAD_DIM, q, zero)], axis=0)


def _attn_finish(o_num, l, lam, sub_w, tq):
    o = o_num[0:tq] / l[0:tq] - lam * (o_num[tq:] / l[tq:])
    return (_rms(o, sub_w) * (1.0 - LAM_INIT)).astype(BF16)


def _attn_prompt_kernel(qi_ref, kj_ref, last_ref, lam_ref, sub_ref, q_ref, k_ref, v_ref, o_ref,
                        qbd_sc, m_sc, l_sc, acc_sc, *, tq, tk):
    s = pl.program_id(2)
    qi = qi_ref[s]
    kj = kj_ref[s]

    @pl.when(kj == 0)
    def _():
        qbd_sc[...] = _split_maps(q_ref[...])
        m_sc[...] = jnp.full(m_sc.shape, -jnp.inf, F32)
        l_sc[...] = jnp.zeros(l_sc.shape, F32)
        acc_sc[...] = jnp.zeros(acc_sc.shape, F32)

    def tile(masked):
        sc = _dot_nt(qbd_sc[...], k_ref[...])
        if masked:
            row = lax.broadcasted_iota(jnp.int32, (2 * tq, 1), 0)
            q_chunk = (qi * tq + jnp.where(row >= tq, row - tq, row)) >> LOG2_CHUNK
            k_chunk = (kj * tk + lax.broadcasted_iota(jnp.int32, (1, tk), 1)) >> LOG2_CHUNK
            sc = jnp.where(k_chunk <= q_chunk, sc, -jnp.inf)
        m_old = m_sc[...]
        m_new = jnp.maximum(m_old, jnp.max(sc, axis=-1, keepdims=True))
        alpha = jnp.exp(m_old - m_new)
        p = jnp.exp(sc - m_new)
        l_sc[...] = alpha * l_sc[...] + jnp.sum(p, axis=-1, keepdims=True)
        acc_sc[...] = alpha * acc_sc[...] + _dot(p.astype(BF16), v_ref[...])
        m_sc[...] = m_new

    needs_mask = ((kj * tk + tk - 1) >> LOG2_CHUNK) > ((qi * tq) >> LOG2_CHUNK)
    pl.when(needs_mask)(lambda: tile(True))
    pl.when(jnp.logical_not(needs_mask))(lambda: tile(False))

    @pl.when(last_ref[s] == 1)
    def _():
        o_ref[...] = _attn_finish(acc_sc[...], l_sc[...], _lam_value(lam_ref), sub_ref[...], tq)


def _attn_prompt(q, k, v, lam_vecs, sub_w, batch, seq):
    tq = min(512, seq)
    tk = min(1024, seq)
    nq = seq // tq
    qi, kj, last = [], [], []
    for i in range(nq):
        nk = -(-((i + 1) * tq) // tk)
        for j in range(nk):
            qi.append(i), kj.append(j), last.append(int(j == nk - 1))
    qi, kj, last = (jnp.asarray(a, jnp.int32) for a in (qi, kj, last))
    nkb = seq // tk
    fix = lambda b, h, s, qi, kj, la: (0, 0)
    grid_spec = pltpu.PrefetchScalarGridSpec(
        num_scalar_prefetch=3,
        grid=(batch, A_HEADS, int(qi.shape[0])),
        in_specs=[
            pl.BlockSpec((4, A_HEAD_DIM), fix),
            pl.BlockSpec((1, HEAD_W), fix),
            pl.BlockSpec((tq, HEAD_W), lambda b, h, s, qi, kj, la: (b * nq + qi[s], h)),
            pl.BlockSpec((tk, HEAD_W), lambda b, h, s, qi, kj, la: (b * nkb + kj[s], h)),
            pl.BlockSpec((tk, HEAD_W), lambda b, h, s, qi, kj, la: (b * nkb + kj[s], h)),
        ],
        out_specs=pl.BlockSpec((tq, HEAD_W), lambda b, h, s, qi, kj, la: (b * nq + qi[s], h)),
        scratch_shapes=[pltpu.VMEM((2 * tq, HEAD_W), BF16), pltpu.VMEM((2 * tq, 1), F32),
                        pltpu.VMEM((2 * tq, 1), F32), pltpu.VMEM((2 * tq, HEAD_W), F32)],
    )
    return pl.pallas_call(
        functools.partial(_attn_prompt_kernel, tq=tq, tk=tk),
        grid_spec=grid_spec,
        out_shape=jax.ShapeDtypeStruct(q.shape, BF16),
        compiler_params=pltpu.CompilerParams(
            dimension_semantics=("arbitrary", "arbitrary", "arbitrary"), vmem_limit_bytes=VMEM_LIMIT),
        name="attn_prompt",
    )(qi, kj, last, lam_vecs, sub_w, q, k, v)


def _attn_sample_kernel(lam_ref, sub_ref, q_ref, kc_ref, vc_ref, kn_ref, vn_ref, o_ref, *, t):
    lam = _lam_value(lam_ref)
    for h in range(A_HEADS):
        cs = slice(h * HEAD_W, (h + 1) * HEAD_W)
        qbd = _split_maps(q_ref[:, cs])
        kc = kc_ref[0, :, cs].astype(BF16)
        vc = vc_ref[0, :, cs].astype(BF16)
        s_past = _dot_nt(qbd, kc)
        s_new = _dot_nt(qbd, kn_ref[:, cs])
        m = jnp.maximum(jnp.max(s_past, axis=-1, keepdims=True), jnp.max(s_new, axis=-1, keepdims=True))
        p_past = jnp.exp(s_past - m)
        p_new = jnp.exp(s_new - m)
        l = jnp.sum(p_past, axis=-1, keepdims=True) + jnp.sum(p_new, axis=-1, keepdims=True)
        o_num = _dot(p_past.astype(BF16), vc) + _dot(p_new.astype(BF16), vn_ref[:, cs])
        o_ref[:, cs] = _attn_finish(o_num, l, lam, sub_ref[...], t)


def _attn_sample(q, cache_k, cache_v, k_new, v_new, lam_vecs, sub_w, batch, t):
    past = cache_k.shape[1]
    fix = lambda b: (0, 0)
    row = lambda b: (b, 0)
    return pl.pallas_call(
        functools.partial(_attn_sample_kernel, t=t),
        grid=(batch,),
        in_specs=[pl.BlockSpec((4, A_HEAD_DIM), fix), pl.BlockSpec((1, HEAD_W), fix),
                  pl.BlockSpec((t, SEG_W), row),
                  pl.BlockSpec((1, past, SEG_W), lambda b: (b, 0, 0)),
                  pl.BlockSpec((1, past, SEG_W), lambda b: (b, 0, 0)),
                  pl.BlockSpec((t, SEG_W), row), pl.BlockSpec((t, SEG_W), row)],
        out_specs=pl.BlockSpec((t, SEG_W), row),
        out_shape=jax.ShapeDtypeStruct(q.shape, BF16),
        compiler_params=pltpu.CompilerParams(dimension_semantics=("arbitrary",),
                                             vmem_limit_bytes=VMEM_LIMIT),
        name="attn_sample",
    )(lam_vecs, sub_w, q, cache_k, cache_v, k_new, v_new)


def _hgrn_consts(length):
    t = np.arange(length)[:, None]
    j = np.arange(length)[None, :]
    sums = [j <= t, j > t]
    masks = [j == t]
    blk = length
    while blk >= 2:
        half = blk // 2
        mid_t = (t // blk) * blk + half
        mid_j = (j // blk) * blk + half
        sums.append(np.where(t >= mid_t, (j >= mid_t) & (j <= t), (j > t) & (j < mid_t)))
        masks.append((t // blk == j // blk) & (t >= mid_t) & (j < mid_j))
        blk = half
    sums = np.concatenate(sums, axis=0).astype(np.float32)
    masks = np.concatenate(masks, axis=0).astype(np.float32)
    return jnp.asarray(sums, BF16), jnp.asarray(masks, F32)


def _hgrn_kernel(*refs, length, chunks, has_state_in):
    if has_state_in:
        (sums_ref, masks_ref, gn_ref, q_ref, logf_ref, k_ref, v_ref, g_ref, s0_ref,
         r_ref, sout_ref, st_sc) = refs
    else:
        (sums_ref, masks_ref, gn_ref, q_ref, logf_ref, k_ref, v_ref, g_ref,
         r_ref, sout_ref, st_sc) = refs
    step = pl.program_id(1)
    levels = int(math.log2(length))
    L = length

    @pl.when(step == 0)
    def _():
        for h in range(R_HEADS):
            if has_state_in:
                st_sc[h] = s0_ref[0, h].T
            else:
                st_sc[h] = jnp.zeros(st_sc.shape[1:], F32)

    def chunk(c, carry):
        off = pl.multiple_of(c * L, L)
        rows = pl.ds(off, L)
        logf = logf_ref[rows, :]
        hi = logf.astype(BF16)
        lo = (logf - hi.astype(F32)).astype(BF16)
        expo = _dot(sums_ref[...], hi) + _dot(sums_ref[...], lo)
        for h in range(R_HEADS):
            cs = slice(h * HEAD_W, (h + 1) * HEAD_W)
            q = q_ref[rows, cs]
            k = k_ref[rows, cs]
            v = v_ref[rows, cs]
            b = expo[0:L, cs]
            a = masks_ref[0:L, :] * _dot_nt(q.astype(BF16), k.astype(BF16))
            for lv in range(levels):
                x = jnp.exp(expo[(2 + lv) * L:(3 + lv) * L, cs])
                a = a + masks_ref[(1 + lv) * L:(2 + lv) * L, :] * _dot_nt(
                    (q * x).astype(BF16), (k * x).astype(BF16))
            st = st_sc[h]
            o = _dot(a.astype(BF16), v) + _dot_nt((q * jnp.exp(b)).astype(BF16), st.astype(BF16))
            k_dec = (k * jnp.exp(expo[L:2 * L, cs])).astype(BF16)
            st_sc[h] = st * jnp.exp(b[L - 1:L, :]) + _dot_tn(v, k_dec)
            g = g_ref[rows, cs]
            r_ref[rows, cs] = (_rms(o, gn_ref[...]) * (g * _sigmoid(g))).astype(BF16)
        return carry

    lax.fori_loop(0, chunks, chunk, 0)

    @pl.when(step == pl.num_programs(1) - 1)
    def _():
        for h in range(R_HEADS):
            sout_ref[0, h] = st_sc[h].T


def _hgrn(q, logf, k, v, g, gn_w, state_in, batch, seq, length, chunks):
    sums, masks = _hgrn_consts(length)
    tm = length * chunks
    steps = seq // tm
    fix = lambda b, s: (0, 0)
    row = lambda b, s: (b * steps + s, 0)
    tok = pl.BlockSpec((tm, SEG_W), row)
    in_specs = [pl.BlockSpec(sums.shape, fix), pl.BlockSpec(masks.shape, fix), pl.BlockSpec((1, HEAD_W), fix),
                tok, tok, tok, tok, tok]
    args = [sums, masks, gn_w, q, logf, k, v, g]
    state_spec = pl.BlockSpec((1, R_HEADS, HEAD_W, HEAD_W), lambda b, s: (b, 0, 0, 0))
    if state_in is not None:
        in_specs.append(state_spec)
        args.append(state_in)
    return pl.pallas_call(
        functools.partial(_hgrn_kernel, length=length, chunks=chunks, has_state_in=state_in is not None),
        grid=(batch, steps),
        in_specs=in_specs,
        out_specs=[tok, state_spec],
        out_shape=[jax.ShapeDtypeStruct(q.shape, BF16),
                   jax.ShapeDtypeStruct((batch, R_HEADS, HEAD_W, HEAD_W), F32)],
        scratch_shapes=[pltpu.VMEM((R_HEADS, HEAD_W, HEAD_W), F32)],
        compiler_params=pltpu.CompilerParams(dimension_semantics=("arbitrary", "arbitrary"),
                                             vmem_limit_bytes=VMEM_LIMIT),
        name="hgrn",
    )(*args)


def _merge_kernel(x_ref, a_ref, r_ref, wo_ref, n2_ref, wr_hi_ref, wr_lo_ref,
                  ymid_ref, h2_ref, route_ref):
    half = a_ref.shape[1]
    y = x_ref[...] + _dot(a_ref[...], wo_ref[0:half, :]) + _dot(r_ref[...], wo_ref[half:, :])
    ymid_ref[...] = y
    hn = _rms(y, n2_ref[...])
    h2_ref[...] = hn
    hi = hn.astype(BF16)
    lo = (hn - hi.astype(F32)).astype(BF16)
    logits = _dot(hi, wr_hi_ref[...]) + _dot(lo, wr_hi_ref[...]) + _dot(hi, wr_lo_ref[...])
    lane = lax.broadcasted_iota(jnp.int32, logits.shape, 1)
    neg = -jnp.inf
    big = jnp.int32(ROUTE_W)

    def top1(mask):
        val = jnp.max(jnp.where(mask, logits, neg), axis=-1, keepdims=True)
        idx = jnp.min(jnp.where(mask & (logits == val), lane, big), axis=-1, keepdims=True)
        return val, idx

    is_group = lane < N_GROUPS
    g_max, g_sel = top1(is_group)
    p_group = 1.0 / jnp.sum(jnp.where(is_group, jnp.exp(logits - g_max), 0.0), axis=-1, keepdims=True)
    first = N_GROUPS + g_sel * EXPERTS_PER_GROUP
    in_group = (lane >= first) & (lane < first + EXPERTS_PER_GROUP)
    v1, i1 = top1(in_group)
    v2, i2 = top1(in_group & (lane != i1))
    e2 = jnp.exp(v2 - v1)
    gate1 = p_group * (1.0 / (1.0 + e2))
    gate2 = p_group * (e2 / (1.0 + e2))
    col = lax.broadcasted_iota(jnp.int32, route_ref.shape, 1)
    route_ref[...] = jnp.where(col == 0, (i1 - N_GROUPS).astype(F32),
                               jnp.where(col == 1, (i2 - N_GROUPS).astype(F32),
                                         jnp.where(col == 2, gate1, gate2)))


def _merge(x, a, r, wo_bf, n2, wr_hi, wr_lo, tm):
    n = x.shape[0]
    row = lambda i: (i, 0)
    fix = lambda i: (0, 0)
    return pl.pallas_call(
        _merge_kernel,
        grid=(n // tm,),
        in_specs=[pl.BlockSpec((tm, D_MODEL), row), pl.BlockSpec((tm, SEG_W), row),
                  pl.BlockSpec((tm, SEG_W), row), pl.BlockSpec((D_MODEL, D_MODEL), fix),
                  pl.BlockSpec((1, D_MODEL), fix), pl.BlockSpec((D_MODEL, ROUTE_W), fix),
                  pl.BlockSpec((D_MODEL, ROUTE_W), fix)],
        out_specs=[pl.BlockSpec((tm, D_MODEL), row), pl.BlockSpec((tm, D_MODEL), row),
                   pl.BlockSpec((tm, 8), row)],
        out_shape=[jax.ShapeDtypeStruct((n, D_MODEL), F32), jax.ShapeDtypeStruct((n, D_MODEL), F32),
                   jax.ShapeDtypeStruct((n, 8), F32)],
        compiler_params=pltpu.CompilerParams(dimension_semantics=("arbitrary",),
                                             vmem_limit_bytes=VMEM_LIMIT),
        name="merge",
    )(x, a, r, wo_bf, n2, wr_hi, wr_lo)


def _experts_kernel(blk_exp_ref, valid_ref, idx_hbm, gate_ref, h_hbm, wg_ref, wu_ref, wd_ref, out_hbm,
                    idx_sm, xbuf, ybuf, wg_sc, wu_sc, wd_sc, sem_idx, sem_in, sem_out):
    i = pl.program_id(0)
    n_valid = valid_ref[i]

    @pl.when(n_valid > 0)
    def _():
        idx_copy = pltpu.make_async_copy(idx_hbm.at[i], idx_sm, sem_idx)
        idx_copy.start()

        prev = blk_exp_ref[jnp.maximum(i - 1, 0)]

        @pl.when((i == 0) | (prev != blk_exp_ref[i]))
        def _():
            wg_sc[...] = wg_ref[0].astype(BF16)
            wu_sc[...] = wu_ref[0].astype(BF16)
            wd_sc[...] = wd_ref[0].astype(BF16)

        idx_copy.wait()

        def row_in(r):
            return pltpu.make_async_copy(h_hbm.at[pl.ds(idx_sm[r], 1)], xbuf.at[pl.ds(r, 1)], sem_in)

        def row_out(r):
            return pltpu.make_async_copy(ybuf.at[pl.ds(r, 1)],
                                         out_hbm.at[pl.ds(idx_sm[ROW_BLOCK + r], 1)], sem_out)

        def start_in(r, c):
            row_in(r).start()
            return c

        def wait_in(r, c):
            row_in(r).wait()
            return c

        def start_out(r, c):
            row_out(r).start()
            return c

        def wait_out(r, c):
            row_out(r).wait()
            return c

        lax.fori_loop(0, ROW_BLOCK, start_in, 0)
        lax.fori_loop(0, ROW_BLOCK, wait_in, 0)
        xb = xbuf[...].astype(BF16)
        gate = _dot(xb, wg_sc[...])
        up = _dot(xb, wu_sc[...])
        act = (gate * _sigmoid(gate) * up).astype(BF16)
        ybuf[...] = _dot(act, wd_sc[...]) * gate_ref[...]
        lax.fori_loop(0, n_valid, start_out, 0)
        lax.fori_loop(0, n_valid, wait_out, 0)


def _experts(h2, blk_exp, valid, idx, row_gate, wg, wu, wd, n_out_rows):
    n_blocks = int(blk_exp.shape[0])
    wmap = lambda i, be, us: (be[i], 0, 0)
    grid_spec = pltpu.PrefetchScalarGridSpec(
        num_scalar_prefetch=2,
        grid=(n_blocks,),
        in_specs=[pl.BlockSpec(memory_space=pl.ANY),
                  pl.BlockSpec((ROW_BLOCK, 1), lambda i, be, us: (i, 0)),
                  pl.BlockSpec(memory_space=pl.ANY),
                  pl.BlockSpec((1, D_MODEL, D_EXPERT), wmap),
                  pl.BlockSpec((1, D_MODEL, D_EXPERT), wmap),
                  pl.BlockSpec((1, D_EXPERT, D_MODEL), wmap)],
        out_specs=pl.BlockSpec(memory_space=pl.ANY),
        scratch_shapes=[pltpu.SMEM((2 * ROW_BLOCK,), jnp.int32),
                        pltpu.VMEM((ROW_BLOCK, D_MODEL), F32), pltpu.VMEM((ROW_BLOCK, D_MODEL), F32),
                        pltpu.VMEM((D_MODEL, D_EXPERT), BF16), pltpu.VMEM((D_MODEL, D_EXPERT), BF16),
                        pltpu.VMEM((D_EXPERT, D_MODEL), BF16),
                        pltpu.SemaphoreType.DMA, pltpu.SemaphoreType.DMA, pltpu.SemaphoreType.DMA],
    )
    return pl.pallas_call(
        _experts_kernel,
        grid_spec=grid_spec,
        out_shape=jax.ShapeDtypeStruct((n_out_rows, D_MODEL), F32),
        compiler_params=pltpu.CompilerParams(dimension_semantics=("arbitrary",),
                                             vmem_limit_bytes=VMEM_LIMIT),
        name="experts",
    )(blk_exp, valid, idx, row_gate, h2, wg, wu, wd)


def _route_rows(route, n_tok):
    experts = route[:, 0:TOP_K].astype(jnp.int32)
    gates = route[:, TOP_K:2 * TOP_K]
    m = n_tok * TOP_K
    flat_e = experts.reshape(-1)
    flat_g = gates.reshape(-1)
    order = jnp.argsort(flat_e)
    se = flat_e[order]
    counts = jnp.bincount(flat_e, length=N_EXPERTS)
    starts = jnp.cumsum(counts) - counts
    pcounts = (counts + ROW_BLOCK - 1) // ROW_BLOCK * ROW_BLOCK
    pends = jnp.cumsum(pcounts)
    pstarts = pends - pcounts
    dest = pstarts[se] + jnp.arange(m, dtype=jnp.int32) - starts[se]
    n_blocks = -(-m // ROW_BLOCK) + N_EXPERTS
    n_rows = n_blocks * ROW_BLOCK
    src_tok = order // TOP_K
    dst_row = (order % TOP_K) * n_tok + src_tok
    row_tok = jnp.zeros((n_rows,), jnp.int32).at[dest].set(src_tok.astype(jnp.int32))
    row_dst = jnp.zeros((n_rows,), jnp.int32).at[dest].set(dst_row.astype(jnp.int32))
    row_gate = jnp.zeros((n_rows,), F32).at[dest].set(flat_g[order])
    blk_start = jnp.arange(n_blocks, dtype=jnp.int32) * ROW_BLOCK
    blk_exp = jnp.minimum(jnp.searchsorted(pends, blk_start, side='right'), N_EXPERTS - 1).astype(jnp.int32)
    valid = jnp.clip(pstarts[blk_exp] + counts[blk_exp] - blk_start, 0, ROW_BLOCK).astype(jnp.int32)
    idx = jnp.concatenate([row_tok.reshape(n_blocks, ROW_BLOCK), row_dst.reshape(n_blocks, ROW_BLOCK)], axis=1)
    return blk_exp, valid, idx, row_gate.reshape(n_rows, 1)


def _final_kernel(y_ref, m0_ref, m1_ref, w_ref, o_ref):
    o_ref[...] = _rms(y_ref[...] + m0_ref[...] + m1_ref[...], w_ref[...])


def _final(ymid, moe_out, w, tok_off, n_tok_all, tm):
    n = ymid.shape[0]
    b0 = tok_off // tm
    b1 = (n_tok_all + tok_off) // tm
    return pl.pallas_call(
        _final_kernel,
        grid=(n // tm,),
        in_specs=[pl.BlockSpec((tm, D_MODEL), lambda i: (i, 0)),
                  pl.BlockSpec((tm, D_MODEL), lambda i: (b0 + i, 0)),
                  pl.BlockSpec((tm, D_MODEL), lambda i: (b1 + i, 0)),
                  pl.BlockSpec((1, D_MODEL), lambda i: (0, 0))],
        out_specs=pl.BlockSpec((tm, D_MODEL), lambda i: (i, 0)),
        out_shape=jax.ShapeDtypeStruct((n, D_MODEL), F32),
        compiler_params=pltpu.CompilerParams(dimension_semantics=("arbitrary",),
                                             vmem_limit_bytes=VMEM_LIMIT),
        name="final",
    )(ymid, moe_out, moe_out, w)


def kernel(x_prompt, x_sample, cache_k, cache_v, state_rec, w_in, lam_q1, lam_k1, lam_q2, lam_k2,
           subln_w, lb_param, gnorm_w, w_out, norm1_w, norm2_w, w_group, w_router,
           w_e_gate, w_e_up, w_e_down, final_w):
    assert w_in.shape[0] == 1 and lb_param.shape[0] == 2, "single-layer model"
    bp, sp, _ = x_prompt.shape
    bs, ts, _ = x_sample.shape
    past = cache_k.shape[2]
    assert sp % CHUNK == 0 and past % CHUNK == 0 and ts <= CHUNK and ts & (ts - 1) == 0
    n_p, n_s = bp * sp, bs * ts
    n_all = n_p + n_s
    tm = 512 if (n_p % 512 == 0 and n_s % 512 == 0) else math.gcd(n_p, n_s)

    w_in_bf = w_in[0].astype(BF16)
    w_out_bf = w_out[0].astype(BF16)
    n1 = norm1_w[0].reshape(1, D_MODEL)
    n2 = norm2_w[0].reshape(1, D_MODEL)
    lam_vecs = jnp.stack([lam_q1[0], lam_k1[0], lam_q2[0], lam_k2[0]]).astype(F32)
    sub_w = subln_w[0].reshape(1, HEAD_W)
    gn_w = gnorm_w[0].reshape(1, HEAD_W)
    w_route = jnp.zeros((D_MODEL, ROUTE_W), F32)
    w_route = w_route.at[:, 0:N_GROUPS].set(w_group[0]).at[:, N_GROUPS:N_GROUPS + N_EXPERTS].set(w_router[0])
    wr_hi = w_route.astype(BF16)
    wr_lo = (w_route - wr_hi.astype(F32)).astype(BF16)

    xp = x_prompt.reshape(n_p, D_MODEL)
    xs = x_sample.reshape(n_s, D_MODEL)

    q, kf, vf, kb, vb, qr, logf, kr, ir, gr = _inproj(xp, n1, w_in_bf, lb_param, tm)
    a_p = _attn_prompt(q, kb, vb, lam_vecs, sub_w, bp, sp)
    chunks = 8 if sp % (8 * CHUNK) == 0 else 1
    r_p, state_p = _hgrn(qr, logf, kr, ir, gr, gn_w, None, bp, sp, CHUNK, chunks)
    ymid_p, h2_p, route_p = _merge(xp, a_p, r_p, w_out_bf, n2, wr_hi, wr_lo, tm)
    k_prompt = kf.reshape(1, bp, sp, A_HEADS, HEAD_W)
    v_prompt = vf.reshape(1, bp, sp, A_HEADS, HEAD_W)

    q, kf, vf, kb, vb, qr, logf, kr, ir, gr = _inproj(xs, n1, w_in_bf, lb_param, tm)
    a_s = _attn_sample(q, cache_k[0].reshape(bs, past, SEG_W), cache_v[0].reshape(bs, past, SEG_W),
                       kb, vb, lam_vecs, sub_w, bs, ts)
    r_s, state_s = _hgrn(qr, logf, kr, ir, gr, gn_w, state_rec[0], bs, ts, ts, 1)
    ymid_s, h2_s, route_s = _merge(xs, a_s, r_s, w_out_bf, n2, wr_hi, wr_lo, tm)
    k_sample = kf.reshape(1, bs, ts, A_HEADS, HEAD_W)
    v_sample = vf.reshape(1, bs, ts, A_HEADS, HEAD_W)

    h2 = jnp.concatenate([h2_p, h2_s], axis=0)
    route = jnp.concatenate([route_p, route_s], axis=0)
    blk_exp, valid, idx, row_gate = _route_rows(route, n_all)
    moe_out = _experts(h2, blk_exp, valid, idx, row_gate, w_e_gate[0], w_e_up[0], w_e_down[0], TOP_K * n_all)

    y_prompt = _final(ymid_p, moe_out, final_w.reshape(1, D_MODEL), 0, n_all, tm).reshape(bp, sp, D_MODEL)
    y_sample = _final(ymid_s, moe_out, final_w.reshape(1, D_MODEL), n_p, n_all, tm).reshape(bs, ts, D_MODEL)
    return (y_prompt, y_sample, k_prompt, v_prompt, state_p[None], k_sample, v_sample, state_s[None])
```

```python
import functools
import math

import numpy as np
import jax
import jax.numpy as jnp
from jax import lax
from jax.experimental import pallas as pl
from jax.experimental.pallas import tpu as pltpu

F32 = jnp.float32
BF16 = jnp.bfloat16

D_MODEL = 1024
RMS_EPS = 1e-6
CHUNK = 64
LOG2_CHUNK = 6
A_HEADS = 4
A_HEAD_DIM = 64
HEAD_W = 2 * A_HEAD_DIM
R_HEADS = 4
SEG_W = 512
N_SEG = 7
N_GROUPS = 4
EXPERTS_PER_GROUP = 8
N_EXPERTS = N_GROUPS * EXPERTS_PER_GROUP
TOP_K = 2
D_EXPERT = 512
EXPERT_ROWS = 256
ROUTE_W = 128
LAM_INIT = 0.8 - 0.6 * math.exp(-0.3 * 0)
VMEM_LIMIT = 56 * 1024 * 1024


def _sigmoid(x):
    return 1.0 / (1.0 + jnp.exp(-x))


def _dot(a, b):
    return jnp.dot(a, b, preferred_element_type=F32)


def _dot_nt(a, b):
    return lax.dot_general(a, b, (((1,), (1,)), ((), ())), preferred_element_type=F32)


def _dot_tn(a, b):
    return lax.dot_general(a, b, (((0,), (0,)), ((), ())), preferred_element_type=F32)


def _rms(x, w):
    return x * lax.rsqrt(jnp.mean(x * x, axis=-1, keepdims=True) + RMS_EPS) * w


def _inproj_kernel(x_ref, n1_ref, w_ref, lbp_ref, q_ref, kf_ref, vf_ref, kb_ref, vb_ref,
                   qr_ref, logf_ref, kr_ref, ir_ref, gr_ref):
    h = _rms(x_ref[...], n1_ref[...]).astype(BF16)

    def seg(i):
        return _dot(h, w_ref[:, i * SEG_W:(i + 1) * SEG_W])

    q_ref[...] = (seg(0) * (A_HEAD_DIM ** -0.5)).astype(BF16)
    k = seg(1)
    kf_ref[...] = k
    kb_ref[...] = k.astype(BF16)
    v = seg(2)
    vf_ref[...] = v
    vb_ref[...] = v.astype(BF16)
    qr = seg(3)
    qr_ref[...] = qr * _sigmoid(qr)
    p = lbp_ref[...]
    e = jnp.exp(p - jnp.max(p, axis=0, keepdims=True))
    lb = e[0:1] / jnp.sum(e, axis=0, keepdims=True)
    f = lb + (1.0 - lb) * _sigmoid(seg(4))
    logf_ref[...] = jnp.log(f)
    kr_ref[...] = 1.0 - f
    ir_ref[...] = seg(5).astype(BF16)
    gr_ref[...] = seg(6)


def _inproj(x, n1, w_bf, lbp, tm):
    n = x.shape[0]
    row = lambda i: (i, 0)
    fix = lambda i: (0, 0)
    out = lambda dt: jax.ShapeDtypeStruct((n, SEG_W), dt)
    ospec = pl.BlockSpec((tm, SEG_W), row)
    return pl.pallas_call(
        _inproj_kernel,
        grid=(n // tm,),
        in_specs=[pl.BlockSpec((tm, D_MODEL), row), pl.BlockSpec((1, D_MODEL), fix),
                  pl.BlockSpec((D_MODEL, N_SEG * SEG_W), fix), pl.BlockSpec(lbp.shape, fix)],
        out_specs=[ospec] * 10,
        out_shape=[out(BF16), out(F32), out(F32), out(BF16), out(BF16),
                   out(F32), out(F32), out(F32), out(BF16), out(F32)],
        compiler_params=pltpu.CompilerParams(dimension_semantics=("arbitrary",),
                                             vmem_limit_bytes=VMEM_LIMIT),
        name="inproj",
    )(x, n1, w_bf, lbp)


def _lam_value(lam_ref):
    l = lam_ref[...]
    s1 = jnp.sum(l[0:1] * l[1:2], axis=-1, keepdims=True)
    s2 = jnp.sum(l[2:3] * l[3:4], axis=-1, keepdims=True)
    return jnp.exp(s1) - jnp.exp(s2) + LAM_INIT


def _split_maps(q):
    lane = lax.broadcasted_iota(jnp.int32, q.shape, 1)
    zero = jnp.zeros_like(q)
    return jnp.concatenate([jnp.where(lane < A_HEAD_DIM, q, zero),
                            jnp.where(lane >= A_HEAD_DIM, q, zero)], axis=0)


def _attn_finish(o_num, l, lam, sub_w, tq):
    o = o_num[0:tq] / l[0:tq] - lam * (o_num[tq:] / l[tq:])
    return (_rms(o, sub_w) * (1.0 - LAM_INIT)).astype(BF16)


def _attn_prompt_kernel(lam_ref, sub_ref, bias_ref, q_ref, k_ref, v_ref, o_ref,
                        qbd_sc, vone_sc, m_sc, acc_sc, *, tq, rb):
    i = pl.program_id(2)
    n_rb = tq // rb

    @pl.when(i == 0)
    def _():
        vone_sc[:, 0:HEAD_W] = v_ref[...]
        vone_sc[:, HEAD_W:] = jnp.ones((vone_sc.shape[0], HEAD_W), BF16)

    qbd_sc[...] = _split_maps(q_ref[...])
    m_sc[...] = jnp.full(m_sc.shape, -jnp.inf, F32)
    acc_sc[...] = jnp.zeros(acc_sc.shape, F32)

    def update(r, parts):
        rows = slice(r * rb, (r + 1) * rb)
        qb = qbd_sc[rows, :]
        scores = []
        for start, size, bias in parts:
            sc = _dot_nt(qb, k_ref[pl.ds(start, size), :])
            scores.append(sc if bias is None else sc + bias)
        m_old = m_sc[rows, :]
        m_new = m_old
        for sc in scores:
            m_new = jnp.maximum(m_new, jnp.max(sc, axis=-1, keepdims=True))
        pv = None
        for sc, (start, size, _) in zip(scores, parts):
            d = _dot(jnp.exp(sc - m_new).astype(BF16), vone_sc[pl.ds(start, size), :])
            pv = d if pv is None else pv + d
        acc_sc[rows, :] = jnp.exp(m_old - m_new) * acc_sc[rows, :] + pv
        m_sc[rows, :] = m_new

    def full_tile(j, carry):
        start = pl.multiple_of(j * tq, tq)
        for r in range(2 * n_rb):
            update(r, [(start, tq, None)])
        return carry

    lax.fori_loop(0, i, full_tile, 0)

    base = pl.multiple_of(i * tq, tq)
    for r in range(2 * n_rb):
        local = r % n_rb
        parts = [(base, local * rb, None)] if local else []
        parts.append((pl.multiple_of(base + local * rb, rb), rb, bias_ref[...]))
        update(r, parts)

    acc = acc_sc[...]
    o_ref[...] = _attn_finish(acc[:, 0:HEAD_W], acc[:, HEAD_W:], _lam_value(lam_ref), sub_ref[...], tq)


def _attn_prompt(q, k, v, lam_vecs, sub_w, batch, seq):
    tq = min(1024, seq)
    rb = min(256, tq)
    nq = seq // tq
    pos = np.arange(rb) // CHUNK
    bias = jnp.asarray(np.where(pos[None, :] <= pos[:, None], 0.0, -np.inf), F32)
    fix = lambda b, h, i: (0, 0)
    head = lambda b, h, i: (b, h)
    tile = lambda b, h, i: (b * nq + i, h)
    return pl.pallas_call(
        functools.partial(_attn_prompt_kernel, tq=tq, rb=rb),
        grid=(batch, A_HEADS, nq),
        in_specs=[pl.BlockSpec((4, A_HEAD_DIM), fix), pl.BlockSpec((1, HEAD_W), fix),
                  pl.BlockSpec((rb, rb), fix), pl.BlockSpec((tq, HEAD_W), tile),
                  pl.BlockSpec((seq, HEAD_W), head), pl.BlockSpec((seq, HEAD_W), head)],
        out_specs=pl.BlockSpec((tq, HEAD_W), tile),
        out_shape=jax.ShapeDtypeStruct(q.shape, BF16),
        scratch_shapes=[pltpu.VMEM((2 * tq, HEAD_W), BF16), pltpu.VMEM((seq, 2 * HEAD_W), BF16),
                        pltpu.VMEM((2 * tq, 1), F32), pltpu.VMEM((2 * tq, 2 * HEAD_W), F32)],
        compiler_params=pltpu.CompilerParams(
            dimension_semantics=("arbitrary", "arbitrary", "arbitrary"), vmem_limit_bytes=VMEM_LIMIT),
        name="attn_prompt",
    )(lam_vecs, sub_w, bias, q, k, v)


def _attn_sample_kernel(lam_ref, sub_ref, q_ref, kc_ref, vc_ref, kn_ref, vn_ref, o_ref, *, t):
    lam = _lam_value(lam_ref)
    for h in range(A_HEADS):
        cs = slice(h * HEAD_W, (h + 1) * HEAD_W)
        qbd = _split_maps(q_ref[:, cs])
        kc = kc_ref[0, :, cs].astype(BF16)
        vc = vc_ref[0, :, cs].astype(BF16)
        s_past = _dot_nt(qbd, kc)
        s_new = _dot_nt(qbd, kn_ref[:, cs])
        m = jnp.maximum(jnp.max(s_past, axis=-1, keepdims=True), jnp.max(s_new, axis=-1, keepdims=True))
        p_past = jnp.exp(s_past - m)
        p_new = jnp.exp(s_new - m)
        l = jnp.sum(p_past, axis=-1, keepdims=True) + jnp.sum(p_new, axis=-1, keepdims=True)
        o_num = _dot(p_past.astype(BF16), vc) + _dot(p_new.astype(BF16), vn_ref[:, cs])
        o_ref[:, cs] = _attn_finish(o_num, l, lam, sub_ref[...], t)


def _attn_sample(q, cache_k, cache_v, k_new, v_new, lam_vecs, sub_w, batch, t):
    past = cache_k.shape[1]
    fix = lambda b: (0, 0)
    row = lambda b: (b, 0)
    return pl.pallas_call(
        functools.partial(_attn_sample_kernel, t=t),
        grid=(batch,),
        in_specs=[pl.BlockSpec((4, A_HEAD_DIM), fix), pl.BlockSpec((1, HEAD_W), fix),
                  pl.BlockSpec((t, SEG_W), row),
                  pl.BlockSpec((1, past, SEG_W), lambda b: (b, 0, 0)),
                  pl.BlockSpec((1, past, SEG_W), lambda b: (b, 0, 0)),
                  pl.BlockSpec((t, SEG_W), row), pl.BlockSpec((t, SEG_W), row)],
        out_specs=pl.BlockSpec((t, SEG_W), row),
        out_shape=jax.ShapeDtypeStruct(q.shape, BF16),
        compiler_params=pltpu.CompilerParams(dimension_semantics=("arbitrary",),
                                             vmem_limit_bytes=VMEM_LIMIT),
        name="attn_sample",
    )(lam_vecs, sub_w, q, cache_k, cache_v, k_new, v_new)


def _hgrn_consts(length):
    t = np.arange(length)[:, None]
    j = np.arange(length)[None, :]
    sums = [j <= t, j > t]
    masks = [j == t]
    blk = length
    while blk >= 2:
        half = blk // 2
        mid_t = (t // blk) * blk + half
        mid_j = (j // blk) * blk + half
        sums.append(np.where(t >= mid_t, (j >= mid_t) & (j <= t), (j > t) & (j < mid_t)))
        masks.append((t // blk == j // blk) & (t >= mid_t) & (j < mid_j))
        blk = half
    sums = np.concatenate(sums, axis=0).astype(np.float32)
    masks = np.concatenate(masks, axis=0).astype(np.float32)
    return jnp.asarray(sums, BF16), jnp.asarray(masks, F32)


def _hgrn_kernel(*refs, length, chunks, has_state_in):
    if has_state_in:
        (sums_ref, masks_ref, gn_ref, q_ref, logf_ref, k_ref, v_ref, g_ref, s0_ref,
         r_ref, sout_ref, st_sc) = refs
    else:
        (sums_ref, masks_ref, gn_ref, q_ref, logf_ref, k_ref, v_ref, g_ref,
         r_ref, sout_ref, st_sc) = refs
    step = pl.program_id(1)
    levels = int(math.log2(length))
    L = length

    @pl.when(step == 0)
    def _():
        for h in range(R_HEADS):
            if has_state_in:
                st_sc[h] = s0_ref[0, h].T
            else:
                st_sc[h] = jnp.zeros(st_sc.shape[1:], F32)

    def chunk(c, carry):
        off = pl.multiple_of(c * L, L)
        rows = pl.ds(off, L)
        logf = logf_ref[rows, :]
        hi = logf.astype(BF16)
        lo = (logf - hi.astype(F32)).astype(BF16)
        expo = _dot(sums_ref[...], hi) + _dot(sums_ref[...], lo)
        for h in range(R_HEADS):
            cs = slice(h * HEAD_W, (h + 1) * HEAD_W)
            q = q_ref[rows, cs]
            k = k_ref[rows, cs]
            v = v_ref[rows, cs]
            b = expo[0:L, cs]
            a = masks_ref[0:L, :] * _dot_nt(q.astype(BF16), k.astype(BF16))
            for lv in range(levels):
                x = jnp.exp(expo[(2 + lv) * L:(3 + lv) * L, cs])
                a = a + masks_ref[(1 + lv) * L:(2 + lv) * L, :] * _dot_nt(
                    (q * x).astype(BF16), (k * x).astype(BF16))
            st = st_sc[h]
            o = _dot(a.astype(BF16), v) + _dot_nt((q * jnp.exp(b)).astype(BF16), st.astype(BF16))
            k_dec = (k * jnp.exp(expo[L:2 * L, cs])).astype(BF16)
            st_sc[h] = st * jnp.exp(b[L - 1:L, :]) + _dot_tn(v, k_dec)
            g = g_ref[rows, cs]
            r_ref[rows, cs] = (_rms(o, gn_ref[...]) * (g * _sigmoid(g))).astype(BF16)
        return carry

    lax.fori_loop(0, chunks, chunk, 0)

    @pl.when(step == pl.num_programs(1) - 1)
    def _():
        for h in range(R_HEADS):
            sout_ref[0, h] = st_sc[h].T


def _hgrn(q, logf, k, v, g, gn_w, state_in, batch, seq, length, chunks):
    sums, masks = _hgrn_consts(length)
    tm = length * chunks
    steps = seq // tm
    fix = lambda b, s: (0, 0)
    row = lambda b, s: (b * steps + s, 0)
    tok = pl.BlockSpec((tm, SEG_W), row)
    in_specs = [pl.BlockSpec(sums.shape, fix), pl.BlockSpec(masks.shape, fix), pl.BlockSpec((1, HEAD_W), fix),
                tok, tok, tok, tok, tok]
    args = [sums, masks, gn_w, q, logf, k, v, g]
    state_spec = pl.BlockSpec((1, R_HEADS, HEAD_W, HEAD_W), lambda b, s: (b, 0, 0, 0))
    if state_in is not None:
        in_specs.append(state_spec)
        args.append(state_in)
    return pl.pallas_call(
        functools.partial(_hgrn_kernel, length=length, chunks=chunks, has_state_in=state_in is not None),
        grid=(batch, steps),
        in_specs=in_specs,
        out_specs=[tok, state_spec],
        out_shape=[jax.ShapeDtypeStruct(q.shape, BF16),
                   jax.ShapeDtypeStruct((batch, R_HEADS, HEAD_W, HEAD_W), F32)],
        scratch_shapes=[pltpu.VMEM((R_HEADS, HEAD_W, HEAD_W), F32)],
        compiler_params=pltpu.CompilerParams(dimension_semantics=("arbitrary", "arbitrary"),
                                             vmem_limit_bytes=VMEM_LIMIT),
        name="hgrn",
    )(*args)


def _merge_kernel(x_ref, a_ref, r_ref, wo_ref, n2_ref, wr_hi_ref, wr_lo_ref, tri_ref, cnt0_ref,
                  ymid_ref, h2_ref, route_ref, cnt_ref, cnt_sc):
    step = pl.program_id(0)

    @pl.when(step == 0)
    def _():
        cnt_sc[...] = cnt0_ref[...]

    half = a_ref.shape[1]
    y = x_ref[...] + _dot(a_ref[...], wo_ref[0:half, :]) + _dot(r_ref[...], wo_ref[half:, :])
    ymid_ref[...] = y
    hn = _rms(y, n2_ref[...])
    h2_ref[...] = hn
    hi = hn.astype(BF16)
    lo = (hn - hi.astype(F32)).astype(BF16)
    logits = _dot(hi, wr_hi_ref[...]) + _dot(lo, wr_hi_ref[...]) + _dot(hi, wr_lo_ref[...])
    lane = lax.broadcasted_iota(jnp.int32, logits.shape, 1)
    neg = -jnp.inf
    big = jnp.int32(ROUTE_W)

    def top1(mask):
        val = jnp.max(jnp.where(mask, logits, neg), axis=-1, keepdims=True)
        idx = jnp.min(jnp.where(mask & (logits == val), lane, big), axis=-1, keepdims=True)
        return val, idx

    is_group = lane < N_GROUPS
    g_max, g_sel = top1(is_group)
    p_group = 1.0 / jnp.sum(jnp.where(is_group, jnp.exp(logits - g_max), 0.0), axis=-1, keepdims=True)
    first = N_GROUPS + g_sel * EXPERTS_PER_GROUP
    in_group = (lane >= first) & (lane < first + EXPERTS_PER_GROUP)
    v1, i1 = top1(in_group)
    v2, i2 = top1(in_group & (lane != i1))
    e2 = jnp.exp(v2 - v1)
    gate1 = p_group * (1.0 / (1.0 + e2))
    gate2 = p_group * (e2 / (1.0 + e2))
    hot1 = lane == i1
    hot2 = lane == i2
    hot = jnp.where(hot1 | hot2, 1.0, 0.0)
    before = cnt_sc[...] + _dot(tri_ref[...], hot.astype(BF16))
    rank1 = jnp.sum(jnp.where(hot1, before, 0.0), axis=-1, keepdims=True)
    rank2 = jnp.sum(jnp.where(hot2, before, 0.0), axis=-1, keepdims=True)
    cnt_sc[...] = cnt_sc[...] + jnp.sum(hot, axis=0, keepdims=True)
    cnt_ref[...] = cnt_sc[...]
    col = lax.broadcasted_iota(jnp.int32, route_ref.shape, 1)
    cols = [(i1 - N_GROUPS).astype(F32), (i2 - N_GROUPS).astype(F32), gate1, gate2, rank1, rank2]
    route = jnp.zeros(route_ref.shape, F32)
    for c, val in enumerate(cols):
        route = jnp.where(col == c, val, route)
    route_ref[...] = route


def _merge(x, a, r, wo_bf, n2, wr_hi, wr_lo, counts_in, tm):
    n = x.shape[0]
    row = lambda i: (i, 0)
    fix = lambda i: (0, 0)
    tri = jnp.asarray(np.tril(np.ones((tm, tm), np.float32), -1), BF16)
    return pl.pallas_call(
        _merge_kernel,
        grid=(n // tm,),
        in_specs=[pl.BlockSpec((tm, D_MODEL), row), pl.BlockSpec((tm, SEG_W), row),
                  pl.BlockSpec((tm, SEG_W), row), pl.BlockSpec((D_MODEL, D_MODEL), fix),
                  pl.BlockSpec((1, D_MODEL), fix), pl.BlockSpec((D_MODEL, ROUTE_W), fix),
                  pl.BlockSpec((D_MODEL, ROUTE_W), fix), pl.BlockSpec((tm, tm), fix),
                  pl.BlockSpec((1, ROUTE_W), fix)],
        out_specs=[pl.BlockSpec((tm, D_MODEL), row), pl.BlockSpec((tm, D_MODEL), row),
                   pl.BlockSpec((tm, 8), row), pl.BlockSpec((1, ROUTE_W), fix)],
        out_shape=[jax.ShapeDtypeStruct((n, D_MODEL), F32), jax.ShapeDtypeStruct((n, D_MODEL), F32),
                   jax.ShapeDtypeStruct((n, 8), F32), jax.ShapeDtypeStruct((1, ROUTE_W), F32)],
        scratch_shapes=[pltpu.VMEM((1, ROUTE_W), F32)],
        compiler_params=pltpu.CompilerParams(dimension_semantics=("arbitrary",),
                                             vmem_limit_bytes=VMEM_LIMIT),
        name="merge",
    )(x, a, r, wo_bf, n2, wr_hi, wr_lo, tri, counts_in)


def _dispatch_kernel(pend_ref, pcnt_ref, n_used_ref, dest_hbm, hp_ref, hs_ref, xs_hbm,
                     idx_sm, zero_sc, sem_idx, sem_out, *, tm, steps_p, steps_s, first_spare, n_blocks):
    i = pl.program_id(0)
    slot = i % 2
    n_steps = steps_p + steps_s

    def idx_copy(step, s):
        return pltpu.make_async_copy(dest_hbm.at[step], idx_sm.at[s], sem_idx.at[s])

    @pl.when(i == 0)
    def _():
        idx_copy(0, 0).start()
        zero_sc[...] = jnp.zeros(zero_sc.shape, F32)

        def fill(start):
            start = pl.multiple_of(start, EXPERT_ROWS)
            return pltpu.make_async_copy(zero_sc, xs_hbm.at[pl.ds(start, EXPERT_ROWS)], sem_out)

        for action in ("start", "wait"):
            for e in range(N_EXPERTS):
                pl.when(pcnt_ref[e] > 0)(
                    lambda e=e: getattr(fill(pend_ref[e] - EXPERT_ROWS), action)())
            for blk in range(first_spare, n_blocks):
                pl.when(blk >= n_used_ref[0])(
                    lambda blk=blk: getattr(fill(blk * EXPERT_ROWS), action)())

    idx_copy(i, slot).wait()

    @pl.when(i + 1 < n_steps)
    def _():
        idx_copy(i + 1, 1 - slot).start()

    def scatter(h_ref):
        for r in range(2 * tm):
            pltpu.make_async_copy(h_ref.at[pl.ds(r % tm, 1)], xs_hbm.at[pl.ds(idx_sm[slot, r], 1)],
                                  sem_out).start()
        for _ in range(2):
            pltpu.make_async_copy(h_ref, xs_hbm.at[pl.ds(0, tm)], sem_out).wait()

    pl.when(i < steps_p)(lambda: scatter(hp_ref))
    pl.when(i >= steps_p)(lambda: scatter(hs_ref))


def _dispatch(h2_p, h2_s, dest_tbl, pends, pcounts, n_used, n_rows, tm):
    steps_p = h2_p.shape[0] // tm
    steps_s = h2_s.shape[0] // tm
    n_blocks = n_rows // EXPERT_ROWS
    first_spare = (h2_p.shape[0] + h2_s.shape[0]) * TOP_K // EXPERT_ROWS
    grid_spec = pltpu.PrefetchScalarGridSpec(
        num_scalar_prefetch=3, grid=(steps_p + steps_s,),
        in_specs=[pl.BlockSpec(memory_space=pl.ANY),
                  pl.BlockSpec((tm, D_MODEL), lambda i, pe, pc, nu: (jnp.minimum(i, steps_p - 1), 0)),
                  pl.BlockSpec((tm, D_MODEL), lambda i, pe, pc, nu: (jnp.maximum(i - steps_p, 0), 0))],
        out_specs=pl.BlockSpec(memory_space=pl.ANY),
        scratch_shapes=[pltpu.SMEM((2, 2 * tm), jnp.int32), pltpu.VMEM((EXPERT_ROWS, D_MODEL), F32),
                        pltpu.SemaphoreType.DMA((2,)), pltpu.SemaphoreType.DMA])
    return pl.pallas_call(
        functools.partial(_dispatch_kernel, tm=tm, steps_p=steps_p, steps_s=steps_s,
                          first_spare=first_spare, n_blocks=n_blocks),
        grid_spec=grid_spec,
        out_shape=jax.ShapeDtypeStruct((n_rows, D_MODEL), F32),
        compiler_params=pltpu.CompilerParams(dimension_semantics=("arbitrary",),
                                             vmem_limit_bytes=VMEM_LIMIT),
        name="dispatch",
    )(pends, pcounts, n_used, dest_tbl, h2_p, h2_s)


def _experts_kernel(blk_exp_ref, n_used_ref, x_ref, wg_ref, wu_ref, wd_ref, y_ref, wg_sc, wu_sc, wd_sc):
    i = pl.program_id(0)

    @pl.when(i < n_used_ref[0])
    def _():
        prev = blk_exp_ref[jnp.maximum(i - 1, 0)]

        @pl.when((i == 0) | (prev != blk_exp_ref[i]))
        def _():
            wg_sc[...] = wg_ref[0].astype(BF16)
            wu_sc[...] = wu_ref[0].astype(BF16)
            wd_sc[...] = wd_ref[0].astype(BF16)

        xb = x_ref[...].astype(BF16)
        gate = _dot(xb, wg_sc[...])
        up = _dot(xb, wu_sc[...])
        act = (gate * _sigmoid(gate) * up).astype(BF16)
        y_ref[...] = _dot(act, wd_sc[...])

    @pl.when(i >= n_used_ref[0])
    def _():
        y_ref[...] = jnp.zeros(y_ref.shape, F32)


def _experts(xs, blk_exp, n_used, wg, wu, wd):
    n_blocks = int(blk_exp.shape[0])
    wmap = lambda i, be, nu: (be[i], 0, 0)
    xmap = lambda i, be, nu: (jnp.minimum(i, nu[0] - 1), 0)
    grid_spec = pltpu.PrefetchScalarGridSpec(
        num_scalar_prefetch=2,
        grid=(n_blocks,),
        in_specs=[pl.BlockSpec((EXPERT_ROWS, D_MODEL), xmap),
                  pl.BlockSpec((1, D_MODEL, D_EXPERT), wmap),
                  pl.BlockSpec((1, D_MODEL, D_EXPERT), wmap),
                  pl.BlockSpec((1, D_EXPERT, D_MODEL), wmap)],
        out_specs=pl.BlockSpec((EXPERT_ROWS, D_MODEL), lambda i, be, nu: (i, 0)),
        scratch_shapes=[pltpu.VMEM((D_MODEL, D_EXPERT), BF16), pltpu.VMEM((D_MODEL, D_EXPERT), BF16),
                        pltpu.VMEM((D_EXPERT, D_MODEL), BF16)],
    )
    return pl.pallas_call(
        _experts_kernel,
        grid_spec=grid_spec,
        out_shape=jax.ShapeDtypeStruct(xs.shape, F32),
        compiler_params=pltpu.CompilerParams(dimension_semantics=("arbitrary",),
                                             vmem_limit_bytes=VMEM_LIMIT),
        name="experts",
    )(blk_exp, n_used, xs, wg, wu, wd)


def _expert_layout(counts_row, n_tok):
    counts = counts_row[0, N_GROUPS:N_GROUPS + N_EXPERTS].astype(jnp.int32)
    pcounts = (counts + EXPERT_ROWS - 1) // EXPERT_ROWS * EXPERT_ROWS
    pends = jnp.cumsum(pcounts).astype(jnp.int32)
    pstarts = pends - pcounts
    n_blocks = -(-(n_tok * TOP_K) // EXPERT_ROWS) + N_EXPERTS
    blk_start = jnp.arange(n_blocks, dtype=jnp.int32) * EXPERT_ROWS
    blk_exp = jnp.minimum(jnp.sum(blk_start[:, None] >= pends[None, :], axis=1), N_EXPERTS - 1).astype(jnp.int32)
    n_used = (pends[-1:] // EXPERT_ROWS).astype(jnp.int32)
    return pstarts, pends, pcounts, blk_exp, n_used, n_blocks * EXPERT_ROWS


def _dest_table(route, pstarts, tm):
    experts = route[:, 0:TOP_K].astype(jnp.int32)
    ranks = route[:, 2 * TOP_K:3 * TOP_K].astype(jnp.int32)
    dest = pstarts[experts] + ranks
    return dest.reshape(-1, tm, TOP_K).swapaxes(1, 2).reshape(-1, TOP_K * tm)


def _final_kernel(dest_hbm, ys_hbm, y_ref, route_ref, w_ref, o_ref, idx_sm, gbuf, sem_idx, sem_g,
                  *, tm, n_steps):
    i = pl.program_id(0)

    def idx_copy(step):
        return pltpu.make_async_copy(dest_hbm.at[step], idx_sm.at[step % 3], sem_idx.at[step % 3])

    def gather(step):
        for r in range(2 * tm):
            pltpu.make_async_copy(ys_hbm.at[pl.ds(idx_sm[step % 3, r], 1)],
                                  gbuf.at[step % 2, pl.ds(r, 1)], sem_g.at[step % 2]).start()

    @pl.when(i == 0)
    def _():
        idx_copy(0).start()
        idx_copy(0).wait()
        gather(0)
        if n_steps > 1:
            idx_copy(1).start()

    if n_steps > 1:
        @pl.when(i + 1 < n_steps)
        def _():
            idx_copy(i + 1).wait()
            gather(i + 1)

        if n_steps > 2:
            @pl.when(i + 2 < n_steps)
            def _():
                idx_copy(i + 2).start()

    pltpu.make_async_copy(ys_hbm.at[pl.ds(0, 2 * tm)], gbuf.at[i % 2], sem_g.at[i % 2]).wait()
    route = route_ref[...]
    rows = gbuf[i % 2]
    y = y_ref[...] + route[:, 2:3] * rows[0:tm] + route[:, 3:4] * rows[tm:]
    o_ref[...] = _rms(y, w_ref[...])


def _final(ymid, route, dest_tbl, ys, w, tm):
    n = ymid.shape[0]
    n_steps = n // tm
    return pl.pallas_call(
        functools.partial(_final_kernel, tm=tm, n_steps=n_steps),
        grid=(n_steps,),
        in_specs=[pl.BlockSpec(memory_space=pl.ANY), pl.BlockSpec(memory_space=pl.ANY),
                  pl.BlockSpec((tm, D_MODEL), lambda i: (i, 0)), pl.BlockSpec((tm, 8), lambda i: (i, 0)),
                  pl.BlockSpec((1, D_MODEL), lambda i: (0, 0))],
        out_specs=pl.BlockSpec((tm, D_MODEL), lambda i: (i, 0)),
        out_shape=jax.ShapeDtypeStruct((n, D_MODEL), F32),
        scratch_shapes=[pltpu.SMEM((3, 2 * tm), jnp.int32), pltpu.VMEM((2, 2 * tm, D_MODEL), F32),
                        pltpu.SemaphoreType.DMA((3,)), pltpu.SemaphoreType.DMA((2,))],
        compiler_params=pltpu.CompilerParams(dimension_semantics=("arbitrary",),
                                             vmem_limit_bytes=VMEM_LIMIT),
        name="final",
    )(dest_tbl, ys, ymid, route, w)


def kernel(x_prompt, x_sample, cache_k, cache_v, state_rec, w_in, lam_q1, lam_k1, lam_q2, lam_k2,
           subln_w, lb_param, gnorm_w, w_out, norm1_w, norm2_w, w_group, w_router,
           w_e_gate, w_e_up, w_e_down, final_w):
    assert w_in.shape[0] == 1 and lb_param.shape[0] == 2, "single-layer model"
    bp, sp, _ = x_prompt.shape
    bs, ts, _ = x_sample.shape
    past = cache_k.shape[2]
    assert sp % CHUNK == 0 and past % CHUNK == 0 and ts <= CHUNK and ts & (ts - 1) == 0
    n_p, n_s = bp * sp, bs * ts
    n_all = n_p + n_s
    tm = 512 if (n_p % 512 == 0 and n_s % 512 == 0) else math.gcd(n_p, n_s)

    w_in_bf = w_in[0].astype(BF16)
    w_out_bf = w_out[0].astype(BF16)
    n1 = norm1_w[0].reshape(1, D_MODEL)
    n2 = norm2_w[0].reshape(1, D_MODEL)
    lam_vecs = jnp.stack([lam_q1[0], lam_k1[0], lam_q2[0], lam_k2[0]]).astype(F32)
    sub_w = subln_w[0].reshape(1, HEAD_W)
    gn_w = gnorm_w[0].reshape(1, HEAD_W)
    w_route = jnp.zeros((D_MODEL, ROUTE_W), F32)
    w_route = w_route.at[:, 0:N_GROUPS].set(w_group[0]).at[:, N_GROUPS:N_GROUPS + N_EXPERTS].set(w_router[0])
    wr_hi = w_route.astype(BF16)
    wr_lo = (w_route - wr_hi.astype(F32)).astype(BF16)

    xp = x_prompt.reshape(n_p, D_MODEL)
    xs = x_sample.reshape(n_s, D_MODEL)

    q, kf, vf, kb, vb, qr, logf, kr, ir, gr = _inproj(xp, n1, w_in_bf, lb_param, tm)
    a_p = _attn_prompt(q, kb, vb, lam_vecs, sub_w, bp, sp)
    chunks = 8 if sp % (8 * CHUNK) == 0 else 1
    r_p, state_p = _hgrn(qr, logf, kr, ir, gr, gn_w, None, bp, sp, CHUNK, chunks)
    ymid_p, h2_p, route_p, counts_p = _merge(xp, a_p, r_p, w_out_bf, n2, wr_hi, wr_lo,
                                             jnp.zeros((1, ROUTE_W), F32), tm)
    k_prompt = kf.reshape(1, bp, sp, A_HEADS, HEAD_W)
    v_prompt = vf.reshape(1, bp, sp, A_HEADS, HEAD_W)

    q, kf, vf, kb, vb, qr, logf, kr, ir, gr = _inproj(xs, n1, w_in_bf, lb_param, tm)
    a_s = _attn_sample(q, cache_k[0].reshape(bs, past, SEG_W), cache_v[0].reshape(bs, past, SEG_W),
                       kb, vb, lam_vecs, sub_w, bs, ts)
    r_s, state_s = _hgrn(qr, logf, kr, ir, gr, gn_w, state_rec[0], bs, ts, ts, 1)
    ymid_s, h2_s, route_s, counts = _merge(xs, a_s, r_s, w_out_bf, n2, wr_hi, wr_lo, counts_p, tm)
    k_sample = kf.reshape(1, bs, ts, A_HEADS, HEAD_W)
    v_sample = vf.reshape(1, bs, ts, A_HEADS, HEAD_W)

    pstarts, pends, pcounts, blk_exp, n_used, n_rows = _expert_layout(counts, n_all)
    td = min(256, math.gcd(n_p, n_s))
    dest_p = _dest_table(route_p, pstarts, td)
    dest_s = _dest_table(route_s, pstarts, td)
    xs_rows = _dispatch(h2_p, h2_s, jnp.concatenate([dest_p, dest_s], axis=0), pends, pcounts, n_used,
                        n_rows, td)
    ys_rows = _experts(xs_rows, blk_exp, n_used, w_e_gate[0], w_e_up[0], w_e_down[0])

    fw = final_w.reshape(1, D_MODEL)
    y_prompt = _final(ymid_p, route_p, dest_p, ys_rows, fw, td).reshape(bp, sp, D_MODEL)
    y_sample = _final(ymid_s, route_s, dest_s, ys_rows, fw, td).reshape(bs, ts, D_MODEL)
    return (y_prompt, y_sample, k_prompt, v_prompt, state_p[None], k_sample, v_sample, state_s[None])
```

```python
import functools
import math

import numpy as np
import jax
import jax.numpy as jnp
from jax import lax
from jax.experimental import pallas as pl
from jax.experimental.pallas import tpu as pltpu

F32 = jnp.float32
BF16 = jnp.bfloat16

D_MODEL = 1024
RMS_EPS = 1e-6
CHUNK = 64
A_HEADS = 4
A_HEAD_DIM = 64
HEAD_W = 2 * A_HEAD_DIM
KEY_BLOCK = 256
VT_ROWS = HEAD_W + 16
LOG2_E = math.log2(math.e)
R_HEADS = 4
SEG_W = 512
N_SEG = 7
N_GROUPS = 4
EXPERTS_PER_GROUP = 8
N_EXPERTS = N_GROUPS * EXPERTS_PER_GROUP
TOP_K = 2
D_EXPERT = 512
EXPERT_ROWS = 256
ROUTE_W = 128
LAM_INIT = 0.8 - 0.6 * math.exp(-0.3 * 0)
VMEM_LIMIT = 56 * 1024 * 1024


def _sigmoid(x):
    return 1.0 / (1.0 + jnp.exp(-x))


def _dot(a, b):
    return jnp.dot(a, b, preferred_element_type=F32)


def _dot_nt(a, b):
    return lax.dot_general(a, b, (((1,), (1,)), ((), ())), preferred_element_type=F32)


def _dot_tn(a, b):
    return lax.dot_general(a, b, (((0,), (0,)), ((), ())), preferred_element_type=F32)


def _rms(x, w):
    return x * lax.rsqrt(jnp.mean(x * x, axis=-1, keepdims=True) + RMS_EPS) * w


def _inproj_kernel(x_ref, n1_ref, w_ref, lbp_ref, q_ref, kf_ref, vf_ref, kb_ref, vb_ref,
                   qr_ref, logf_ref, kr_ref, ir_ref, gr_ref, qt_ref, vt_ref):
    h = _rms(x_ref[...], n1_ref[...]).astype(BF16)

    def seg(i):
        return _dot(h, w_ref[:, i * SEG_W:(i + 1) * SEG_W])

    def transposed(val, s):
        return val[s * KEY_BLOCK:(s + 1) * KEY_BLOCK, :].T.astype(BF16)

    q = seg(0) * (A_HEAD_DIM ** -0.5)
    q_ref[...] = q.astype(BF16)
    k = seg(1)
    kf_ref[...] = k
    kb_ref[...] = k.astype(BF16)
    v = seg(2)
    vf_ref[...] = v
    vb_ref[...] = v.astype(BF16)
    ones = jnp.ones((VT_ROWS - HEAD_W, KEY_BLOCK), BF16)
    for s in range(qt_ref.shape[0]):
        qt_ref[s] = transposed(q * LOG2_E, s)
        vt = transposed(v, s)
        for hd in range(A_HEADS):
            vt_ref[s, hd * VT_ROWS:hd * VT_ROWS + HEAD_W, :] = vt[hd * HEAD_W:(hd + 1) * HEAD_W, :]
            vt_ref[s, hd * VT_ROWS + HEAD_W:(hd + 1) * VT_ROWS, :] = ones
    qr = seg(3)
    qr_ref[...] = qr * _sigmoid(qr)
    p = lbp_ref[...]
    e = jnp.exp(p - jnp.max(p, axis=0, keepdims=True))
    lb = e[0:1] / jnp.sum(e, axis=0, keepdims=True)
    f = lb + (1.0 - lb) * _sigmoid(seg(4))
    logf_ref[...] = jnp.log(f)
    kr_ref[...] = 1.0 - f
    ir_ref[...] = seg(5).astype(BF16)
    gr_ref[...] = seg(6)


def _inproj(x, n1, w_bf, lbp, tm):
    n = x.shape[0]
    row = lambda i: (i, 0)
    fix = lambda i: (0, 0)
    out = lambda dt: jax.ShapeDtypeStruct((n, SEG_W), dt)
    ospec = pl.BlockSpec((tm, SEG_W), row)
    slabs = tm // KEY_BLOCK
    tspec = lambda rows: pl.BlockSpec((slabs, rows, KEY_BLOCK), lambda i: (i, 0, 0))
    tout = lambda rows: jax.ShapeDtypeStruct((n // KEY_BLOCK, rows, KEY_BLOCK), BF16)
    return pl.pallas_call(
        _inproj_kernel,
        grid=(n // tm,),
        in_specs=[pl.BlockSpec((tm, D_MODEL), row), pl.BlockSpec((1, D_MODEL), fix),
                  pl.BlockSpec((D_MODEL, N_SEG * SEG_W), fix), pl.BlockSpec(lbp.shape, fix)],
        out_specs=[ospec] * 10 + [tspec(SEG_W), tspec(A_HEADS * VT_ROWS)],
        out_shape=[out(BF16), out(F32), out(F32), out(BF16), out(BF16),
                   out(F32), out(F32), out(F32), out(BF16), out(F32), tout(SEG_W), tout(A_HEADS * VT_ROWS)],
        compiler_params=pltpu.CompilerParams(dimension_semantics=("arbitrary",),
                                             vmem_limit_bytes=VMEM_LIMIT),
        name="inproj",
    )(x, n1, w_bf, lbp)


def _lam_value(lam_ref):
    l = lam_ref[...]
    s1 = jnp.sum(l[0:1] * l[1:2], axis=-1, keepdims=True)
    s2 = jnp.sum(l[2:3] * l[3:4], axis=-1, keepdims=True)
    return jnp.exp(s1) - jnp.exp(s2) + LAM_INIT


def _split_maps(q):
    lane = lax.broadcasted_iota(jnp.int32, q.shape, 1)
    zero = jnp.zeros_like(q)
    return jnp.concatenate([jnp.where(lane < A_HEAD_DIM, q, zero),
                            jnp.where(lane >= A_HEAD_DIM, q, zero)], axis=0)


def _attn_finish(o_num, l, lam, sub_w, tq):
    o = o_num[0:tq] / l[0:tq] - lam * (o_num[tq:] / l[tq:])
    return (_rms(o, sub_w) * (1.0 - LAM_INIT)).astype(BF16)


def _attn_prompt_kernel(lam_ref, sub_ref, bias_ref, qt_ref, k_ref, vt_ref, o_ref,
                        qbd_sc, m_sc, acc_sc, sa_sc, sb_sc, *, qblocks):
    i = pl.program_id(2)
    kb = KEY_BLOCK

    sub = lax.broadcasted_iota(jnp.int32, (HEAD_W, kb), 0)
    for r in range(2 * qblocks):
        qt = qt_ref[r % qblocks]
        keep = (sub < A_HEAD_DIM) if r < qblocks else (sub >= A_HEAD_DIM)
        qbd_sc[r] = jnp.where(keep, qt, jnp.zeros_like(qt))
    m_sc[...] = jnp.full(m_sc.shape, -jnp.inf, F32)
    acc_sc[...] = jnp.zeros(acc_sc.shape, F32)

    def scores(r, key_block):
        return _dot(k_ref[pl.ds(pl.multiple_of(key_block * kb, kb), kb), :], qbd_sc[r])

    def update(r, key_block, st):
        m_old = m_sc[r:r + 1, :]
        m_new = jnp.maximum(m_old, jnp.max(st, axis=0, keepdims=True))
        p = jnp.exp2(st - m_new).astype(BF16)
        acc_sc[r] = jnp.exp2(m_old - m_new) * acc_sc[r] + _dot(vt_ref[key_block], p)
        m_sc[r:r + 1, :] = m_new

    n_q = 2 * qblocks
    first = i * qblocks
    for r in range(n_q):
        sa_sc[r] = scores(r, 0)

    def visible_tile(t, carry):
        for j in range(qblocks):
            cur, nxt = (sa_sc, sb_sc) if j % 2 == 0 else (sb_sc, sa_sc)
            for r in range(n_q):
                nxt[r] = scores(r, t * qblocks + j + 1)
                update(r, t * qblocks + j, cur[r])
        return carry

    lax.fori_loop(0, i, visible_tile, 0)

    work = [(kl, r) for kl in range(qblocks) for r in range(n_q) if r % qblocks >= kl]
    ahead = 2
    pending = {}
    for n, (kl, r) in enumerate(work):
        for kl2, r2 in work[n:n + 1 + ahead]:
            if kl2 > 0 and (kl2, r2) not in pending:
                pending[(kl2, r2)] = scores(r2, first + kl2)
        st = sa_sc[r] if kl == 0 else pending.pop((kl, r))
        update(r, first + kl, st + bias_ref[...] if r % qblocks == kl else st)

    lam = _lam_value(lam_ref)
    for ql in range(qblocks):
        a1 = acc_sc[ql]
        a2 = acc_sc[qblocks + ql]
        ot = a1[0:HEAD_W] / a1[HEAD_W:HEAD_W + 1] - lam * (a2[0:HEAD_W] / a2[HEAD_W:HEAD_W + 1])
        ot = ot * lax.rsqrt(jnp.mean(ot * ot, axis=0, keepdims=True) + RMS_EPS) * sub_ref[...]
        o_ref[ql * kb:(ql + 1) * kb, :] = (ot * (1.0 - LAM_INIT)).T.astype(BF16)


def _attn_prompt(qt, k, vt, lam_vecs, sub_col, batch, seq):
    kb = KEY_BLOCK
    tq = min(1024, seq)
    qblocks = tq // kb
    assert qblocks % 2 == 0, "the key-block loop is unrolled by two"
    nq = seq // tq
    pos = np.arange(kb) // CHUNK
    bias = jnp.asarray(np.where(pos[:, None] <= pos[None, :], 0.0, -np.inf), F32)
    fix = lambda b, h, i: (0, 0)
    return pl.pallas_call(
        functools.partial(_attn_prompt_kernel, qblocks=qblocks),
        grid=(batch, A_HEADS, nq),
        in_specs=[pl.BlockSpec((4, A_HEAD_DIM), fix), pl.BlockSpec((HEAD_W, 1), fix),
                  pl.BlockSpec((kb, kb), fix),
                  pl.BlockSpec((qblocks, HEAD_W, kb), lambda b, h, i: (b * nq + i, h, 0)),
                  pl.BlockSpec((seq, HEAD_W), lambda b, h, i: (b, h)),
                  pl.BlockSpec((seq // kb, VT_ROWS, kb), lambda b, h, i: (b, h, 0))],
        out_specs=pl.BlockSpec((tq, HEAD_W), lambda b, h, i: (b * nq + i, h)),
        out_shape=jax.ShapeDtypeStruct(k.shape, BF16),
        scratch_shapes=[pltpu.VMEM((2 * qblocks, HEAD_W, kb), BF16), pltpu.VMEM((2 * qblocks, kb), F32),
                        pltpu.VMEM((2 * qblocks, VT_ROWS, kb), F32),
                        pltpu.VMEM((2 * qblocks, kb, kb), F32), pltpu.VMEM((2 * qblocks, kb, kb), F32)],
        compiler_params=pltpu.CompilerParams(
            dimension_semantics=("arbitrary", "arbitrary", "arbitrary"), vmem_limit_bytes=VMEM_LIMIT),
        name="attn_prompt",
    )(lam_vecs, sub_col, bias, qt, k, vt)


def _attn_sample_kernel(lam_ref, sub_ref, q_ref, kc_ref, vc_ref, kn_ref, vn_ref, o_ref, *, t):
    lam = _lam_value(lam_ref)
    for h in range(A_HEADS):
        cs = slice(h * HEAD_W, (h + 1) * HEAD_W)
        qbd = _split_maps(q_ref[:, cs])
        kc = kc_ref[0, :, cs].astype(BF16)
        vc = vc_ref[0, :, cs].astype(BF16)
        s_past = _dot_nt(qbd, kc)
        s_new = _dot_nt(qbd, kn_ref[:, cs])
        m = jnp.maximum(jnp.max(s_past, axis=-1, keepdims=True), jnp.max(s_new, axis=-1, keepdims=True))
        p_past = jnp.exp(s_past - m)
        p_new = jnp.exp(s_new - m)
        l = jnp.sum(p_past, axis=-1, keepdims=True) + jnp.sum(p_new, axis=-1, keepdims=True)
        o_num = _dot(p_past.astype(BF16), vc) + _dot(p_new.astype(BF16), vn_ref[:, cs])
        o_ref[:, cs] = _attn_finish(o_num, l, lam, sub_ref[...], t)


def _attn_sample(q, cache_k, cache_v, k_new, v_new, lam_vecs, sub_w, batch, t):
    past = cache_k.shape[1]
    fix = lambda b: (0, 0)
    row = lambda b: (b, 0)
    return pl.pallas_call(
        functools.partial(_attn_sample_kernel, t=t),
        grid=(batch,),
        in_specs=[pl.BlockSpec((4, A_HEAD_DIM), fix), pl.BlockSpec((1, HEAD_W), fix),
                  pl.BlockSpec((t, SEG_W), row),
                  pl.BlockSpec((1, past, SEG_W), lambda b: (b, 0, 0)),
                  pl.BlockSpec((1, past, SEG_W), lambda b: (b, 0, 0)),
                  pl.BlockSpec((t, SEG_W), row), pl.BlockSpec((t, SEG_W), row)],
        out_specs=pl.BlockSpec((t, SEG_W), row),
        out_shape=jax.ShapeDtypeStruct(q.shape, BF16),
        compiler_params=pltpu.CompilerParams(dimension_semantics=("arbitrary",),
                                             vmem_limit_bytes=VMEM_LIMIT),
        name="attn_sample",
    )(lam_vecs, sub_w, q, cache_k, cache_v, k_new, v_new)


def _hgrn_consts(length):
    t = np.arange(length)[:, None]
    j = np.arange(length)[None, :]
    sums = [j <= t, j > t]
    masks = [j == t]
    blk = length
    while blk >= 2:
        half = blk // 2
        mid_t = (t // blk) * blk + half
        mid_j = (j // blk) * blk + half
        sums.append(np.where(t >= mid_t, (j >= mid_t) & (j <= t), (j > t) & (j < mid_t)))
        masks.append((t // blk == j // blk) & (t >= mid_t) & (j < mid_j))
        blk = half
    sums = np.concatenate(sums, axis=0).astype(np.float32)
    masks = np.concatenate(masks, axis=0).astype(np.float32)
    return jnp.asarray(sums, BF16), jnp.asarray(masks, F32)


def _hgrn_kernel(*refs, length, chunks, has_state_in):
    if has_state_in:
        (sums_ref, masks_ref, gn_ref, q_ref, logf_ref, k_ref, v_ref, g_ref, s0_ref,
         r_ref, sout_ref, st_sc) = refs
    else:
        (sums_ref, masks_ref, gn_ref, q_ref, logf_ref, k_ref, v_ref, g_ref,
         r_ref, sout_ref, st_sc) = refs
    step = pl.program_id(1)
    levels = int(math.log2(length))
    L = length

    @pl.when(step == 0)
    def _():
        for h in range(R_HEADS):
            if has_state_in:
                st_sc[h] = s0_ref[0, h].T
            else:
                st_sc[h] = jnp.zeros(st_sc.shape[1:], F32)

    def chunk(c, carry):
        off = pl.multiple_of(c * L, L)
        rows = pl.ds(off, L)
        logf = logf_ref[rows, :]
        hi = logf.astype(BF16)
        lo = (logf - hi.astype(F32)).astype(BF16)
        expo = _dot(sums_ref[...], hi) + _dot(sums_ref[...], lo)
        for h in range(R_HEADS):
            cs = slice(h * HEAD_W, (h + 1) * HEAD_W)
            q = q_ref[rows, cs]
            k = k_ref[rows, cs]
            v = v_ref[rows, cs]
            b = expo[0:L, cs]
            a = masks_ref[0:L, :] * _dot_nt(q.astype(BF16), k.astype(BF16))
            for lv in range(levels):
                x = jnp.exp(expo[(2 + lv) * L:(3 + lv) * L, cs])
                a = a + masks_ref[(1 + lv) * L:(2 + lv) * L, :] * _dot_nt(
                    (q * x).astype(BF16), (k * x).astype(BF16))
            st = st_sc[h]
            o = _dot(a.astype(BF16), v) + _dot_nt((q * jnp.exp(b)).astype(BF16), st.astype(BF16))
            k_dec = (k * jnp.exp(expo[L:2 * L, cs])).astype(BF16)
            st_sc[h] = st * jnp.exp(b[L - 1:L, :]) + _dot_tn(v, k_dec)
            g = g_ref[rows, cs]
            r_ref[rows, cs] = (_rms(o, gn_ref[...]) * (g * _sigmoid(g))).astype(BF16)
        return carry

    lax.fori_loop(0, chunks, chunk, 0)

    @pl.when(step == pl.num_programs(1) - 1)
    def _():
        for h in range(R_HEADS):
            sout_ref[0, h] = st_sc[h].T


def _hgrn(q, logf, k, v, g, gn_w, state_in, batch, seq, length, chunks):
    sums, masks = _hgrn_consts(length)
    tm = length * chunks
    steps = seq // tm
    fix = lambda b, s: (0, 0)
    row = lambda b, s: (b * steps + s, 0)
    tok = pl.BlockSpec((tm, SEG_W), row)
    in_specs = [pl.BlockSpec(sums.shape, fix), pl.BlockSpec(masks.shape, fix), pl.BlockSpec((1, HEAD_W), fix),
                tok, tok, tok, tok, tok]
    args = [sums, masks, gn_w, q, logf, k, v, g]
    state_spec = pl.BlockSpec((1, R_HEADS, HEAD_W, HEAD_W), lambda b, s: (b, 0, 0, 0))
    if state_in is not None:
        in_specs.append(state_spec)
        args.append(state_in)
    return pl.pallas_call(
        functools.partial(_hgrn_kernel, length=length, chunks=chunks, has_state_in=state_in is not None),
        grid=(batch, steps),
        in_specs=in_specs,
        out_specs=[tok, state_spec],
        out_shape=[jax.ShapeDtypeStruct(q.shape, BF16),
                   jax.ShapeDtypeStruct((batch, R_HEADS, HEAD_W, HEAD_W), F32)],
        scratch_shapes=[pltpu.VMEM((R_HEADS, HEAD_W, HEAD_W), F32)],
        compiler_params=pltpu.CompilerParams(dimension_semantics=("arbitrary", "arbitrary"),
                                             vmem_limit_bytes=VMEM_LIMIT),
        name="hgrn",
    )(*args)


def _merge_kernel(x_ref, a_ref, r_ref, wo_ref, n2_ref, wr_hi_ref, wr_lo_ref, tri_ref, cnt0_ref,
                  ymid_ref, h2_ref, route_ref, cnt_ref, cnt_sc):
    step = pl.program_id(0)

    @pl.when(step == 0)
    def _():
        cnt_sc[...] = cnt0_ref[...]

    half = a_ref.shape[1]
    y = x_ref[...] + _dot(a_ref[...], wo_ref[0:half, :]) + _dot(r_ref[...], wo_ref[half:, :])
    ymid_ref[...] = y
    hn = _rms(y, n2_ref[...])
    h2_ref[...] = hn
    hi = hn.astype(BF16)
    lo = (hn - hi.astype(F32)).astype(BF16)
    logits = _dot(hi, wr_hi_ref[...]) + _dot(lo, wr_hi_ref[...]) + _dot(hi, wr_lo_ref[...])
    lane = lax.broadcasted_iota(jnp.int32, logits.shape, 1)
    neg = -jnp.inf
    big = jnp.int32(ROUTE_W)

    def top1(mask):
        val = jnp.max(jnp.where(mask, logits, neg), axis=-1, keepdims=True)
        idx = jnp.min(jnp.where(mask & (logits == val), lane, big), axis=-1, keepdims=True)
        return val, idx

    is_group = lane < N_GROUPS
    g_max, g_sel = top1(is_group)
    p_group = 1.0 / jnp.sum(jnp.where(is_group, jnp.exp(logits - g_max), 0.0), axis=-1, keepdims=True)
    first = N_GROUPS + g_sel * EXPERTS_PER_GROUP
    in_group = (lane >= first) & (lane < first + EXPERTS_PER_GROUP)
    v1, i1 = top1(in_group)
    v2, i2 = top1(in_group & (lane != i1))
    e2 = jnp.exp(v2 - v1)
    gate1 = p_group * (1.0 / (1.0 + e2))
    gate2 = p_group * (e2 / (1.0 + e2))
    hot1 = lane == i1
    hot2 = lane == i2
    hot = jnp.where(hot1 | hot2, 1.0, 0.0)
    before = cnt_sc[...] + _dot(tri_ref[...], hot.astype(BF16))
    rank1 = jnp.sum(jnp.where(hot1, before, 0.0), axis=-1, keepdims=True)
    rank2 = jnp.sum(jnp.where(hot2, before, 0.0), axis=-1, keepdims=True)
    cnt_sc[...] = cnt_sc[...] + jnp.sum(hot, axis=0, keepdims=True)
    cnt_ref[...] = cnt_sc[...]
    col = lax.broadcasted_iota(jnp.int32, route_ref.shape, 1)
    cols = [(i1 - N_GROUPS).astype(F32), (i2 - N_GROUPS).astype(F32), gate1, gate2, rank1, rank2]
    route = jnp.zeros(route_ref.shape, F32)
    for c, val in enumerate(cols):
        route = jnp.where(col == c, val, route)
    route_ref[...] = route


def _merge(x, a, r, wo_bf, n2, wr_hi, wr_lo, counts_in, tm):
    n = x.shape[0]
    row = lambda i: (i, 0)
    fix = lambda i: (0, 0)
    tri = jnp.asarray(np.tril(np.ones((tm, tm), np.float32), -1), BF16)
    return pl.pallas_call(
        _merge_kernel,
        grid=(n // tm,),
        in_specs=[pl.BlockSpec((tm, D_MODEL), row), pl.BlockSpec((tm, SEG_W), row),
                  pl.BlockSpec((tm, SEG_W), row), pl.BlockSpec((D_MODEL, D_MODEL), fix),
                  pl.BlockSpec((1, D_MODEL), fix), pl.BlockSpec((D_MODEL, ROUTE_W), fix),
                  pl.BlockSpec((D_MODEL, ROUTE_W), fix), pl.BlockSpec((tm, tm), fix),
                  pl.BlockSpec((1, ROUTE_W), fix)],
        out_specs=[pl.BlockSpec((tm, D_MODEL), row), pl.BlockSpec((tm, D_MODEL), row),
                   pl.BlockSpec((tm, 8), row), pl.BlockSpec((1, ROUTE_W), fix)],
        out_shape=[jax.ShapeDtypeStruct((n, D_MODEL), F32), jax.ShapeDtypeStruct((n, D_MODEL), F32),
                   jax.ShapeDtypeStruct((n, 8), F32), jax.ShapeDtypeStruct((1, ROUTE_W), F32)],
        scratch_shapes=[pltpu.VMEM((1, ROUTE_W), F32)],
        compiler_params=pltpu.CompilerParams(dimension_semantics=("arbitrary",),
                                             vmem_limit_bytes=VMEM_LIMIT),
        name="merge",
    )(x, a, r, wo_bf, n2, wr_hi, wr_lo, tri, counts_in)


def _dispatch_kernel(pend_ref, pcnt_ref, n_used_ref, dest_hbm, hp_ref, hs_ref, xs_hbm,
                     idx_sm, zero_sc, sem_idx, sem_out, *, tm, steps_p, steps_s, first_spare, n_blocks):
    i = pl.program_id(0)
    slot = i % 2
    n_steps = steps_p + steps_s

    def idx_copy(step, s):
        return pltpu.make_async_copy(dest_hbm.at[step], idx_sm.at[s], sem_idx.at[s])

    @pl.when(i == 0)
    def _():
        idx_copy(0, 0).start()
        zero_sc[...] = jnp.zeros(zero_sc.shape, F32)

        def fill(start):
            start = pl.multiple_of(start, EXPERT_ROWS)
            return pltpu.make_async_copy(zero_sc, xs_hbm.at[pl.ds(start, EXPERT_ROWS)], sem_out)

        for action in ("start", "wait"):
            for e in range(N_EXPERTS):
                pl.when(pcnt_ref[e] > 0)(
                    lambda e=e: getattr(fill(pend_ref[e] - EXPERT_ROWS), action)())
            for blk in range(first_spare, n_blocks):
                pl.when(blk >= n_used_ref[0])(
                    lambda blk=blk: getattr(fill(blk * EXPERT_ROWS), action)())

    idx_copy(i, slot).wait()

    @pl.when(i + 1 < n_steps)
    def _():
        idx_copy(i + 1, 1 - slot).start()

    def scatter(h_ref):
        for r in range(2 * tm):
            pltpu.make_async_copy(h_ref.at[pl.ds(r % tm, 1)], xs_hbm.at[pl.ds(idx_sm[slot, r], 1)],
                                  sem_out).start()
        for _ in range(2):
            pltpu.make_async_copy(h_ref, xs_hbm.at[pl.ds(0, tm)], sem_out).wait()

    pl.when(i < steps_p)(lambda: scatter(hp_ref))
    pl.when(i >= steps_p)(lambda: scatter(hs_ref))


def _dispatch(h2_p, h2_s, dest_tbl, pends, pcounts, n_used, n_rows, tm):
    steps_p = h2_p.shape[0] // tm
    steps_s = h2_s.shape[0] // tm
    n_blocks = n_rows // EXPERT_ROWS
    first_spare = (h2_p.shape[0] + h2_s.shape[0]) * TOP_K // EXPERT_ROWS
    grid_spec = pltpu.PrefetchScalarGridSpec(
        num_scalar_prefetch=3, grid=(steps_p + steps_s,),
        in_specs=[pl.BlockSpec(memory_space=pl.ANY),
                  pl.BlockSpec((tm, D_MODEL), lambda i, pe, pc, nu: (jnp.minimum(i, steps_p - 1), 0)),
                  pl.BlockSpec((tm, D_MODEL), lambda i, pe, pc, nu: (jnp.maximum(i - steps_p, 0), 0))],
        out_specs=pl.BlockSpec(memory_space=pl.ANY),
        scratch_shapes=[pltpu.SMEM((2, 2 * tm), jnp.int32), pltpu.VMEM((EXPERT_ROWS, D_MODEL), F32),
                        pltpu.SemaphoreType.DMA((2,)), pltpu.SemaphoreType.DMA])
    return pl.pallas_call(
        functools.partial(_dispatch_kernel, tm=tm, steps_p=steps_p, steps_s=steps_s,
                          first_spare=first_spare, n_blocks=n_blocks),
        grid_spec=grid_spec,
        out_shape=jax.ShapeDtypeStruct((n_rows, D_MODEL), F32),
        compiler_params=pltpu.CompilerParams(dimension_semantics=("arbitrary",),
                                             vmem_limit_bytes=VMEM_LIMIT),
        name="dispatch",
    )(pends, pcounts, n_used, dest_tbl, h2_p, h2_s)


def _experts_kernel(blk_exp_ref, n_used_ref, x_ref, wg_ref, wu_ref, wd_ref, y_ref, wg_sc, wu_sc, wd_sc):
    i = pl.program_id(0)

    @pl.when(i < n_used_ref[0])
    def _():
        prev = blk_exp_ref[jnp.maximum(i - 1, 0)]

        @pl.when((i == 0) | (prev != blk_exp_ref[i]))
        def _():
            wg_sc[...] = wg_ref[0].astype(BF16)
            wu_sc[...] = wu_ref[0].astype(BF16)
            wd_sc[...] = wd_ref[0].astype(BF16)

        xb = x_ref[...].astype(BF16)
        gate = _dot(xb, wg_sc[...])
        up = _dot(xb, wu_sc[...])
        act = (gate * _sigmoid(gate) * up).astype(BF16)
        y_ref[...] = _dot(act, wd_sc[...])

    @pl.when(i >= n_used_ref[0])
    def _():
        y_ref[...] = jnp.zeros(y_ref.shape, F32)


def _experts(xs, blk_exp, n_used, wg, wu, wd):
    n_blocks = int(blk_exp.shape[0])
    wmap = lambda i, be, nu: (be[i], 0, 0)
    xmap = lambda i, be, nu: (jnp.minimum(i, nu[0] - 1), 0)
    grid_spec = pltpu.PrefetchScalarGridSpec(
        num_scalar_prefetch=2,
        grid=(n_blocks,),
        in_specs=[pl.BlockSpec((EXPERT_ROWS, D_MODEL), xmap),
                  pl.BlockSpec((1, D_MODEL, D_EXPERT), wmap),
                  pl.BlockSpec((1, D_MODEL, D_EXPERT), wmap),
                  pl.BlockSpec((1, D_EXPERT, D_MODEL), wmap)],
        out_specs=pl.BlockSpec((EXPERT_ROWS, D_MODEL), lambda i, be, nu: (i, 0)),
        scratch_shapes=[pltpu.VMEM((D_MODEL, D_EXPERT), BF16), pltpu.VMEM((D_MODEL, D_EXPERT), BF16),
                        pltpu.VMEM((D_EXPERT, D_MODEL), BF16)],
    )
    return pl.pallas_call(
        _experts_kernel,
        grid_spec=grid_spec,
        out_shape=jax.ShapeDtypeStruct(xs.shape, F32),
        compiler_params=pltpu.CompilerParams(dimension_semantics=("arbitrary",),
                                             vmem_limit_bytes=VMEM_LIMIT),
        name="experts",
    )(blk_exp, n_used, xs, wg, wu, wd)


def _expert_layout(counts_row, n_tok):
    counts = counts_row[0, N_GROUPS:N_GROUPS + N_EXPERTS].astype(jnp.int32)
    pcounts = (counts + EXPERT_ROWS - 1) // EXPERT_ROWS * EXPERT_ROWS
    pends = jnp.cumsum(pcounts).astype(jnp.int32)
    pstarts = pends - pcounts
    n_blocks = -(-(n_tok * TOP_K) // EXPERT_ROWS) + N_EXPERTS
    blk_start = jnp.arange(n_blocks, dtype=jnp.int32) * EXPERT_ROWS
    blk_exp = jnp.minimum(jnp.sum(blk_start[:, None] >= pends[None, :], axis=1), N_EXPERTS - 1).astype(jnp.int32)
    n_used = (pends[-1:] // EXPERT_ROWS).astype(jnp.int32)
    return pstarts, pends, pcounts, blk_exp, n_used, n_blocks * EXPERT_ROWS


def _dest_table(route, pstarts, tm):
    experts = route[:, 0:TOP_K].astype(jnp.int32)
    ranks = route[:, 2 * TOP_K:3 * TOP_K].astype(jnp.int32)
    dest = pstarts[experts] + ranks
    return dest.reshape(-1, tm, TOP_K).swapaxes(1, 2).reshape(-1, TOP_K * tm)


def _final_kernel(dest_hbm, ys_hbm, y_ref, route_ref, w_ref, o_ref, idx_sm, gbuf, sem_idx, sem_g,
                  *, tm, n_steps):
    i = pl.program_id(0)

    def idx_copy(step):
        return pltpu.make_async_copy(dest_hbm.at[step], idx_sm.at[step % 3], sem_idx.at[step % 3])

    def gather(step):
        for r in range(2 * tm):
            pltpu.make_async_copy(ys_hbm.at[pl.ds(idx_sm[step % 3, r], 1)],
                                  gbuf.at[step % 2, pl.ds(r, 1)], sem_g.at[step % 2]).start()

    @pl.when(i == 0)
    def _():
        idx_copy(0).start()
        idx_copy(0).wait()
        gather(0)
        if n_steps > 1:
            idx_copy(1).start()

    if n_steps > 1:
        @pl.when(i + 1 < n_steps)
        def _():
            idx_copy(i + 1).wait()
            gather(i + 1)

        if n_steps > 2:
            @pl.when(i + 2 < n_steps)
            def _():
                idx_copy(i + 2).start()

    pltpu.make_async_copy(ys_hbm.at[pl.ds(0, 2 * tm)], gbuf.at[i % 2], sem_g.at[i % 2]).wait()
    route = route_ref[...]
    rows = gbuf[i % 2]
    y = y_ref[...] + route[:, 2:3] * rows[0:tm] + route[:, 3:4] * rows[tm:]
    o_ref[...] = _rms(y, w_ref[...])


def _final(ymid, route, dest_tbl, ys, w, tm):
    n = ymid.shape[0]
    n_steps = n // tm
    return pl.pallas_call(
        functools.partial(_final_kernel, tm=tm, n_steps=n_steps),
        grid=(n_steps,),
        in_specs=[pl.BlockSpec(memory_space=pl.ANY), pl.BlockSpec(memory_space=pl.ANY),
                  pl.BlockSpec((tm, D_MODEL), lambda i: (i, 0)), pl.BlockSpec((tm, 8), lambda i: (i, 0)),
                  pl.BlockSpec((1, D_MODEL), lambda i: (0, 0))],
        out_specs=pl.BlockSpec((tm, D_MODEL), lambda i: (i, 0)),
        out_shape=jax.ShapeDtypeStruct((n, D_MODEL), F32),
        scratch_shapes=[pltpu.SMEM((3, 2 * tm), jnp.int32), pltpu.VMEM((2, 2 * tm, D_MODEL), F32),
                        pltpu.SemaphoreType.DMA((3,)), pltpu.SemaphoreType.DMA((2,))],
        compiler_params=pltpu.CompilerParams(dimension_semantics=("arbitrary",),
                                             vmem_limit_bytes=VMEM_LIMIT),
        name="final",
    )(dest_tbl, ys, ymid, route, w)


def kernel(x_prompt, x_sample, cache_k, cache_v, state_rec, w_in, lam_q1, lam_k1, lam_q2, lam_k2,
           subln_w, lb_param, gnorm_w, w_out, norm1_w, norm2_w, w_group, w_router,
           w_e_gate, w_e_up, w_e_down, final_w):
    assert w_in.shape[0] == 1 and lb_param.shape[0] == 2, "single-layer model"
    bp, sp, _ = x_prompt.shape
    bs, ts, _ = x_sample.shape
    past = cache_k.shape[2]
    assert sp % CHUNK == 0 and past % CHUNK == 0 and ts <= CHUNK and ts & (ts - 1) == 0
    n_p, n_s = bp * sp, bs * ts
    n_all = n_p + n_s
    tm = 512 if (n_p % 512 == 0 and n_s % 512 == 0) else math.gcd(n_p, n_s)

    w_in_bf = w_in[0].astype(BF16)
    w_out_bf = w_out[0].astype(BF16)
    n1 = norm1_w[0].reshape(1, D_MODEL)
    n2 = norm2_w[0].reshape(1, D_MODEL)
    lam_vecs = jnp.stack([lam_q1[0], lam_k1[0], lam_q2[0], lam_k2[0]]).astype(F32)
    sub_w = subln_w[0].reshape(1, HEAD_W)
    gn_w = gnorm_w[0].reshape(1, HEAD_W)
    w_route = jnp.zeros((D_MODEL, ROUTE_W), F32)
    w_route = w_route.at[:, 0:N_GROUPS].set(w_group[0]).at[:, N_GROUPS:N_GROUPS + N_EXPERTS].set(w_router[0])
    wr_hi = w_route.astype(BF16)
    wr_lo = (w_route - wr_hi.astype(F32)).astype(BF16)

    xp = x_prompt.reshape(n_p, D_MODEL)
    xs = x_sample.reshape(n_s, D_MODEL)

    _, kf, vf, kb, _, qr, logf, kr, ir, gr, qt, vt = _inproj(xp, n1, w_in_bf, lb_param, tm)
    a_p = _attn_prompt(qt, kb, vt, lam_vecs, sub_w.reshape(HEAD_W, 1), bp, sp)
    chunks = 8 if sp % (8 * CHUNK) == 0 else 1
    r_p, state_p = _hgrn(qr, logf, kr, ir, gr, gn_w, None, bp, sp, CHUNK, chunks)
    ymid_p, h2_p, route_p, counts_p = _merge(xp, a_p, r_p, w_out_bf, n2, wr_hi, wr_lo,
                                             jnp.zeros((1, ROUTE_W), F32), tm)
    k_prompt = kf.reshape(1, bp, sp, A_HEADS, HEAD_W)
    v_prompt = vf.reshape(1, bp, sp, A_HEADS, HEAD_W)

    q, kf, vf, kb, vb, qr, logf, kr, ir, gr, _, _ = _inproj(xs, n1, w_in_bf, lb_param, tm)
    a_s = _attn_sample(q, cache_k[0].reshape(bs, past, SEG_W), cache_v[0].reshape(bs, past, SEG_W),
                       kb, vb, lam_vecs, sub_w, bs, ts)
    r_s, state_s = _hgrn(qr, logf, kr, ir, gr, gn_w, state_rec[0], bs, ts, ts, 1)
    ymid_s, h2_s, route_s, counts = _merge(xs, a_s, r_s, w_out_bf, n2, wr_hi, wr_lo, counts_p, tm)
    k_sample = kf.reshape(1, bs, ts, A_HEADS, HEAD_W)
    v_sample = vf.reshape(1, bs, ts, A_HEADS, HEAD_W)

    pstarts, pends, pcounts, blk_exp, n_used, n_rows = _expert_layout(counts, n_all)
    td = min(256, math.gcd(n_p, n_s))
    dest_p = _dest_table(route_p, pstarts, td)
    dest_s = _dest_table(route_s, pstarts, td)
    xs_rows = _dispatch(h2_p, h2_s, jnp.concatenate([dest_p, dest_s], axis=0), pends, pcounts, n_used,
                        n_rows, td)
    ys_rows = _experts(xs_rows, blk_exp, n_used, w_e_gate[0], w_e_up[0], w_e_down[0])

    fw = final_w.reshape(1, D_MODEL)
    y_prompt = _final(ymid_p, route_p, dest_p, ys_rows, fw, td).reshape(bp, sp, D_MODEL)
    y_sample = _final(ymid_s, route_s, dest_s, ys_rows, fw, td).reshape(bs, ts, D_MODEL)
    return (y_prompt, y_sample, k_prompt, v_prompt, state_p[None], k_sample, v_sample, state_s[None])
```

```python
import functools
import math

import numpy as np
import jax
import jax.numpy as jnp
from jax import lax
from jax.experimental import pallas as pl
from jax.experimental.pallas import tpu as pltpu

F32 = jnp.float32
BF16 = jnp.bfloat16

D_MODEL = 1024
RMS_EPS = 1e-6
CHUNK = 64
A_HEADS = 4
A_HEAD_DIM = 64
HEAD_W = 2 * A_HEAD_DIM
KEY_BLOCK = 256
VT_ROWS = HEAD_W + 16
LOG2_E = math.log2(math.e)
R_HEADS = 4
SEG_W = 512
N_SEG = 7
N_GROUPS = 4
EXPERTS_PER_GROUP = 8
N_EXPERTS = N_GROUPS * EXPERTS_PER_GROUP
TOP_K = 2
D_EXPERT = 512
EXPERT_ROWS = 256
ROUTE_W = 128
LAM_INIT = 0.8 - 0.6 * math.exp(-0.3 * 0)
VMEM_LIMIT = 56 * 1024 * 1024


def _sigmoid(x):
    return 1.0 / (1.0 + jnp.exp(-x))


def _dot(a, b):
    return jnp.dot(a, b, preferred_element_type=F32)


def _dot_nt(a, b):
    return lax.dot_general(a, b, (((1,), (1,)), ((), ())), preferred_element_type=F32)


def _dot_tn(a, b):
    return lax.dot_general(a, b, (((0,), (0,)), ((), ())), preferred_element_type=F32)


def _rms(x, w):
    return x * lax.rsqrt(jnp.mean(x * x, axis=-1, keepdims=True) + RMS_EPS) * w


ROW_TILE = (D_MODEL // 128, 128)


def _row_tile_spec(rows, index_map):
    return pl.BlockSpec((rows,) + ROW_TILE, index_map)


def _store_row_tiles(ref, val):
    for c in range(ROW_TILE[0]):
        ref[:, c, :] = val[:, c * 128:(c + 1) * 128]


def _load_row_tiles(ref, *lead):
    return jnp.concatenate([ref[lead + (slice(None), c, slice(None))] for c in range(ROW_TILE[0])], axis=1)


def _inproj_kernel(x_ref, n1_ref, w_ref, lbp_ref, q_ref, kf_ref, vf_ref, kb_ref, vb_ref,
                   qr_ref, logf_ref, kr_ref, ir_ref, gr_ref, qt_ref, vt_ref):
    h = _rms(x_ref[...], n1_ref[...]).astype(BF16)

    def seg(i):
        return _dot(h, w_ref[:, i * SEG_W:(i + 1) * SEG_W])

    def transposed(val, s):
        return val[s * KEY_BLOCK:(s + 1) * KEY_BLOCK, :].T.astype(BF16)

    q = seg(0) * (A_HEAD_DIM ** -0.5)
    q_ref[...] = q.astype(BF16)
    k = seg(1)
    for hd in range(A_HEADS):
        kf_ref[:, hd, :] = k[:, hd * HEAD_W:(hd + 1) * HEAD_W]
    kb_ref[...] = k.astype(BF16)
    v = seg(2)
    for hd in range(A_HEADS):
        vf_ref[:, hd, :] = v[:, hd * HEAD_W:(hd + 1) * HEAD_W]
    vb_ref[...] = v.astype(BF16)
    ones = jnp.ones((VT_ROWS - HEAD_W, KEY_BLOCK), BF16)
    for s in range(qt_ref.shape[0]):
        qt_ref[s] = transposed(q * LOG2_E, s)
        vt = transposed(v, s)
        for hd in range(A_HEADS):
            vt_ref[s, hd * VT_ROWS:hd * VT_ROWS + HEAD_W, :] = vt[hd * HEAD_W:(hd + 1) * HEAD_W, :]
            vt_ref[s, hd * VT_ROWS + HEAD_W:(hd + 1) * VT_ROWS, :] = ones
    qr = seg(3)
    qr_ref[...] = qr * _sigmoid(qr)
    p = lbp_ref[...]
    e = jnp.exp(p - jnp.max(p, axis=0, keepdims=True))
    lb = e[0:1] / jnp.sum(e, axis=0, keepdims=True)
    f = lb + (1.0 - lb) * _sigmoid(seg(4))
    logf_ref[...] = jnp.log(f)
    kr_ref[...] = 1.0 - f
    ir_ref[...] = seg(5).astype(BF16)
    gr_ref[...] = seg(6)


def _inproj(x, n1, w_bf, lbp, tm):
    n = x.shape[0]
    row = lambda i: (i, 0)
    fix = lambda i: (0, 0)
    out = lambda dt: jax.ShapeDtypeStruct((n, SEG_W), dt)
    ospec = pl.BlockSpec((tm, SEG_W), row)
    hspec = pl.BlockSpec((tm, A_HEADS, HEAD_W), lambda i: (i, 0, 0))
    hout = jax.ShapeDtypeStruct((n, A_HEADS, HEAD_W), F32)
    slabs = tm // KEY_BLOCK
    tspec = lambda rows: pl.BlockSpec((slabs, rows, KEY_BLOCK), lambda i: (i, 0, 0))
    tout = lambda rows: jax.ShapeDtypeStruct((n // KEY_BLOCK, rows, KEY_BLOCK), BF16)
    return pl.pallas_call(
        _inproj_kernel,
        grid=(n // tm,),
        in_specs=[pl.BlockSpec((tm, D_MODEL), row), pl.BlockSpec((1, D_MODEL), fix),
                  pl.BlockSpec((D_MODEL, N_SEG * SEG_W), fix), pl.BlockSpec(lbp.shape, fix)],
        out_specs=[ospec, hspec, hspec] + [ospec] * 7 + [tspec(SEG_W), tspec(A_HEADS * VT_ROWS)],
        out_shape=[out(BF16), hout, hout, out(BF16), out(BF16),
                   out(F32), out(F32), out(F32), out(BF16), out(F32), tout(SEG_W), tout(A_HEADS * VT_ROWS)],
        compiler_params=pltpu.CompilerParams(dimension_semantics=("arbitrary",),
                                             vmem_limit_bytes=VMEM_LIMIT),
        name="inproj",
    )(x, n1, w_bf, lbp)


def _lam_value(lam_ref):
    l = lam_ref[...]
    s1 = jnp.sum(l[0:1] * l[1:2], axis=-1, keepdims=True)
    s2 = jnp.sum(l[2:3] * l[3:4], axis=-1, keepdims=True)
    return jnp.exp(s1) - jnp.exp(s2) + LAM_INIT


def _split_maps(q):
    lane = lax.broadcasted_iota(jnp.int32, q.shape, 1)
    zero = jnp.zeros_like(q)
    return jnp.concatenate([jnp.where(lane < A_HEAD_DIM, q, zero),
                            jnp.where(lane >= A_HEAD_DIM, q, zero)], axis=0)


def _attn_finish(o_num, l, lam, sub_w, tq):
    o = o_num[0:tq] / l[0:tq] - lam * (o_num[tq:] / l[tq:])
    return (_rms(o, sub_w) * (1.0 - LAM_INIT)).astype(BF16)


def _attn_prompt_kernel(lam_ref, sub_ref, bias_ref, qt_ref, k_ref, vt_ref, o_ref,
                        qbd_sc, m_sc, acc_sc, sa_sc, sb_sc, *, qblocks):
    i = pl.program_id(2)
    kb = KEY_BLOCK

    sub = lax.broadcasted_iota(jnp.int32, (HEAD_W, kb), 0)
    for r in range(2 * qblocks):
        qt = qt_ref[r % qblocks]
        keep = (sub < A_HEAD_DIM) if r < qblocks else (sub >= A_HEAD_DIM)
        qbd_sc[r] = jnp.where(keep, qt, jnp.zeros_like(qt))
    m_sc[...] = jnp.full(m_sc.shape, -jnp.inf, F32)
    acc_sc[...] = jnp.zeros(acc_sc.shape, F32)

    def scores(r, key_block):
        return _dot(k_ref[pl.ds(pl.multiple_of(key_block * kb, kb), kb), :], qbd_sc[r])

    def update(r, key_block, st):
        m_old = m_sc[r:r + 1, :]
        m_new = jnp.maximum(m_old, jnp.max(st, axis=0, keepdims=True))
        p = jnp.exp2(st - m_new).astype(BF16)
        acc_sc[r] = jnp.exp2(m_old - m_new) * acc_sc[r] + _dot(vt_ref[key_block], p)
        m_sc[r:r + 1, :] = m_new

    n_q = 2 * qblocks
    first = i * qblocks
    for r in range(n_q):
        sa_sc[r] = scores(r, 0)

    def visible_tile(t, carry):
        for j in range(qblocks):
            cur, nxt = (sa_sc, sb_sc) if j % 2 == 0 else (sb_sc, sa_sc)
            for r in range(n_q):
                nxt[r] = scores(r, t * qblocks + j + 1)
                update(r, t * qblocks + j, cur[r])
        return carry

    lax.fori_loop(0, i, visible_tile, 0)

    work = [(kl, r) for kl in range(qblocks) for r in range(n_q) if r % qblocks >= kl]
    ahead = 2
    pending = {}
    for n, (kl, r) in enumerate(work):
        for kl2, r2 in work[n:n + 1 + ahead]:
            if kl2 > 0 and (kl2, r2) not in pending:
                pending[(kl2, r2)] = scores(r2, first + kl2)
        st = sa_sc[r] if kl == 0 else pending.pop((kl, r))
        update(r, first + kl, st + bias_ref[...] if r % qblocks == kl else st)

    lam = _lam_value(lam_ref)
    for ql in range(qblocks):
        a1 = acc_sc[ql]
        a2 = acc_sc[qblocks + ql]
        ot = a1[0:HEAD_W] / a1[HEAD_W:HEAD_W + 1] - lam * (a2[0:HEAD_W] / a2[HEAD_W:HEAD_W + 1])
        ot = ot * lax.rsqrt(jnp.mean(ot * ot, axis=0, keepdims=True) + RMS_EPS) * sub_ref[...]
        o_ref[ql * kb:(ql + 1) * kb, :] = (ot * (1.0 - LAM_INIT)).T.astype(BF16)


def _attn_prompt(qt, k, vt, lam_vecs, sub_col, batch, seq):
    kb = KEY_BLOCK
    tq = min(1024, seq)
    qblocks = tq // kb
    assert qblocks % 2 == 0, "the key-block loop is unrolled by two"
    nq = seq // tq
    pos = np.arange(kb) // CHUNK
    bias = jnp.asarray(np.where(pos[:, None] <= pos[None, :], 0.0, -np.inf), F32)
    fix = lambda b, h, i: (0, 0)
    return pl.pallas_call(
        functools.partial(_attn_prompt_kernel, qblocks=qblocks),
        grid=(batch, A_HEADS, nq),
        in_specs=[pl.BlockSpec((4, A_HEAD_DIM), fix), pl.BlockSpec((HEAD_W, 1), fix),
                  pl.BlockSpec((kb, kb), fix),
                  pl.BlockSpec((qblocks, HEAD_W, kb), lambda b, h, i: (b * nq + i, h, 0)),
                  pl.BlockSpec((seq, HEAD_W), lambda b, h, i: (b, h)),
                  pl.BlockSpec((seq // kb, VT_ROWS, kb), lambda b, h, i: (b, h, 0))],
        out_specs=pl.BlockSpec((tq, HEAD_W), lambda b, h, i: (b * nq + i, h)),
        out_shape=jax.ShapeDtypeStruct(k.shape, BF16),
        scratch_shapes=[pltpu.VMEM((2 * qblocks, HEAD_W, kb), BF16), pltpu.VMEM((2 * qblocks, kb), F32),
                        pltpu.VMEM((2 * qblocks, VT_ROWS, kb), F32),
                        pltpu.VMEM((2 * qblocks, kb, kb), F32), pltpu.VMEM((2 * qblocks, kb, kb), F32)],
        compiler_params=pltpu.CompilerParams(
            dimension_semantics=("arbitrary", "arbitrary", "arbitrary"), vmem_limit_bytes=VMEM_LIMIT),
        name="attn_prompt",
    )(lam_vecs, sub_col, bias, qt, k, vt)


def _attn_sample_kernel(lam_ref, sub_ref, q_ref, kc_ref, vc_ref, kn_ref, vn_ref, o_ref, *, t):
    lam = _lam_value(lam_ref)
    for h in range(A_HEADS):
        cs = slice(h * HEAD_W, (h + 1) * HEAD_W)
        qbd = _split_maps(q_ref[:, cs])
        kc = kc_ref[0, 0, :, h, :].astype(BF16)
        vc = vc_ref[0, 0, :, h, :].astype(BF16)
        s_past = _dot_nt(qbd, kc)
        s_new = _dot_nt(qbd, kn_ref[:, cs])
        m = jnp.maximum(jnp.max(s_past, axis=-1, keepdims=True), jnp.max(s_new, axis=-1, keepdims=True))
        p_past = jnp.exp(s_past - m)
        p_new = jnp.exp(s_new - m)
        l = jnp.sum(p_past, axis=-1, keepdims=True) + jnp.sum(p_new, axis=-1, keepdims=True)
        o_num = _dot(p_past.astype(BF16), vc) + _dot(p_new.astype(BF16), vn_ref[:, cs])
        o_ref[:, cs] = _attn_finish(o_num, l, lam, sub_ref[...], t)


def _attn_sample(q, cache_k, cache_v, k_new, v_new, lam_vecs, sub_w, batch, t):
    past = cache_k.shape[2]
    fix = lambda b: (0, 0)
    row = lambda b: (b, 0)
    cache = pl.BlockSpec((1, 1, past, A_HEADS, HEAD_W), lambda b: (0, b, 0, 0, 0))
    return pl.pallas_call(
        functools.partial(_attn_sample_kernel, t=t),
        grid=(batch,),
        in_specs=[pl.BlockSpec((4, A_HEAD_DIM), fix), pl.BlockSpec((1, HEAD_W), fix),
                  pl.BlockSpec((t, SEG_W), row),
                  cache, cache, pl.BlockSpec((t, SEG_W), row), pl.BlockSpec((t, SEG_W), row)],
        out_specs=pl.BlockSpec((t, SEG_W), row),
        out_shape=jax.ShapeDtypeStruct(q.shape, BF16),
        compiler_params=pltpu.CompilerParams(dimension_semantics=("arbitrary",),
                                             vmem_limit_bytes=VMEM_LIMIT),
        name="attn_sample",
    )(lam_vecs, sub_w, q, cache_k, cache_v, k_new, v_new)


def _hgrn_consts(length):
    t = np.arange(length)[:, None]
    j = np.arange(length)[None, :]
    sums = [j <= t]
    masks = [j == t]
    blk = length
    while blk >= 2:
        half = blk // 2
        mid_t = (t // blk) * blk + half
        mid_j = (j // blk) * blk + half
        sums.append(np.where(t >= mid_t, (j >= mid_t) & (j <= t), (j > t) & (j < mid_t)))
        masks.append((t // blk == j // blk) & (t >= mid_t) & (j < mid_j))
        blk = half
    sums = np.tile(np.concatenate(sums, axis=0).astype(np.float32), (1, 2))
    masks = np.concatenate(masks, axis=0).astype(np.float32)
    return jnp.asarray(sums, BF16), jnp.asarray(masks, F32)


def _hgrn_kernel(*refs, length, chunks, has_state_in):
    if has_state_in:
        (sums_ref, masks_ref, gn_ref, q_ref, logf_ref, k_ref, v_ref, g_ref, s0_ref,
         r_ref, sout_ref, st_sc) = refs
    else:
        (sums_ref, masks_ref, gn_ref, q_ref, logf_ref, k_ref, v_ref, g_ref,
         r_ref, sout_ref, st_sc) = refs
    step = pl.program_id(1)
    levels = int(math.log2(length))
    L = length
    group = math.gcd(chunks, 4)

    @pl.when(step == 0)
    def _():
        for h in range(R_HEADS):
            if has_state_in:
                st_sc[h] = s0_ref[0, h].T
            else:
                st_sc[h] = jnp.zeros(st_sc.shape[1:], F32)

    heads = [slice(h * HEAD_W, (h + 1) * HEAD_W) for h in range(R_HEADS)]

    def chunk_group(c, carry):
        rows = [pl.ds(pl.multiple_of((c * group + g) * L, L), L) for g in range(group)]
        expo = []
        for g in range(group):
            logf = logf_ref[rows[g], :]
            hi = logf.astype(BF16)
            lo = (logf - hi.astype(F32)).astype(BF16)
            expo.append(_dot(sums_ref[...], jnp.concatenate([hi, lo], axis=0)))
        streams = [(g, h) for g in range(group) for h in range(R_HEADS)]
        intra, q_dec, upd = {}, {}, {}
        for g, h in streams:
            cs = heads[h]
            q = q_ref[rows[g], cs]
            k = k_ref[rows[g], cs]
            v = v_ref[rows[g], cs]
            a = masks_ref[0:L, :] * _dot_nt(q.astype(BF16), k.astype(BF16))
            for lv in range(levels):
                x = jnp.exp(expo[g][(1 + lv) * L:(2 + lv) * L, cs])
                a = a + masks_ref[(1 + lv) * L:(2 + lv) * L, :] * _dot_nt(
                    (q * x).astype(BF16), (k * x).astype(BF16))
            intra[g, h] = _dot(a.astype(BF16), v)
            b = expo[g][0:L, cs]
            q_dec[g, h] = (q * jnp.exp(b)).astype(BF16)
            k_dec = (k * jnp.exp(b[L - 1:L, :] - b)).astype(BF16)
            upd[g, h] = _dot_tn(v, k_dec)
        for g, h in streams:
            cs = heads[h]
            st = st_sc[h]
            o = intra[g, h] + _dot_nt(q_dec[g, h], st.astype(BF16))
            st_sc[h] = st * jnp.exp(expo[g][L - 1:L, cs]) + upd[g, h]
            gate = g_ref[rows[g], cs]
            r_ref[rows[g], cs] = (_rms(o, gn_ref[...]) * (gate * _sigmoid(gate))).astype(BF16)
        return carry

    lax.fori_loop(0, chunks // group, chunk_group, 0)

    @pl.when(step == pl.num_programs(1) - 1)
    def _():
        for h in range(R_HEADS):
            sout_ref[0, h] = st_sc[h].T


def _hgrn(q, logf, k, v, g, gn_w, state_in, batch, seq, length, chunks):
    sums, masks = _hgrn_consts(length)
    tm = length * chunks
    steps = seq // tm
    fix = lambda b, s: (0, 0)
    row = lambda b, s: (b * steps + s, 0)
    tok = pl.BlockSpec((tm, SEG_W), row)
    in_specs = [pl.BlockSpec(sums.shape, fix), pl.BlockSpec(masks.shape, fix), pl.BlockSpec((1, HEAD_W), fix),
                tok, tok, tok, tok, tok]
    args = [sums, masks, gn_w, q, logf, k, v, g]
    state_spec = pl.BlockSpec((1, R_HEADS, HEAD_W, HEAD_W), lambda b, s: (b, 0, 0, 0))
    if state_in is not None:
        in_specs.append(state_spec)
        args.append(state_in)
    return pl.pallas_call(
        functools.partial(_hgrn_kernel, length=length, chunks=chunks, has_state_in=state_in is not None),
        grid=(batch, steps),
        in_specs=in_specs,
        out_specs=[tok, state_spec],
        out_shape=[jax.ShapeDtypeStruct(q.shape, BF16),
                   jax.ShapeDtypeStruct((batch, R_HEADS, HEAD_W, HEAD_W), F32)],
        scratch_shapes=[pltpu.VMEM((R_HEADS, HEAD_W, HEAD_W), F32)],
        compiler_params=pltpu.CompilerParams(dimension_semantics=("arbitrary", "arbitrary"),
                                             vmem_limit_bytes=VMEM_LIMIT),
        name="hgrn",
    )(*args)


def _merge_kernel(x_ref, a_ref, r_ref, wo_ref, n2_ref, wr_hi_ref, wr_lo_ref, tri_ref, cnt0_ref,
                  ymid_ref, h2_ref, route_ref, cnt_ref, cnt_sc):
    step = pl.program_id(0)

    @pl.when(step == 0)
    def _():
        cnt_sc[...] = cnt0_ref[...]

    half = a_ref.shape[1]
    y = x_ref[...] + _dot(a_ref[...], wo_ref[0:half, :]) + _dot(r_ref[...], wo_ref[half:, :])
    ymid_ref[...] = y
    hn = _rms(y, n2_ref[...])
    _store_row_tiles(h2_ref, hn)
    hi = hn.astype(BF16)
    lo = (hn - hi.astype(F32)).astype(BF16)
    logits = _dot(hi, wr_hi_ref[...]) + _dot(lo, wr_hi_ref[...]) + _dot(hi, wr_lo_ref[...])
    lane = lax.broadcasted_iota(jnp.int32, logits.shape, 1)
    neg = -jnp.inf
    big = jnp.int32(ROUTE_W)

    def top1(mask):
        val = jnp.max(jnp.where(mask, logits, neg), axis=-1, keepdims=True)
        idx = jnp.min(jnp.where(mask & (logits == val), lane, big), axis=-1, keepdims=True)
        return val, idx

    is_group = lane < N_GROUPS
    g_max, g_sel = top1(is_group)
    p_group = 1.0 / jnp.sum(jnp.where(is_group, jnp.exp(logits - g_max), 0.0), axis=-1, keepdims=True)
    first = N_GROUPS + g_sel * EXPERTS_PER_GROUP
    in_group = (lane >= first) & (lane < first + EXPERTS_PER_GROUP)
    v1, i1 = top1(in_group)
    v2, i2 = top1(in_group & (lane != i1))
    e2 = jnp.exp(v2 - v1)
    gate1 = p_group * (1.0 / (1.0 + e2))
    gate2 = p_group * (e2 / (1.0 + e2))
    hot1 = lane == i1
    hot2 = lane == i2
    hot = jnp.where(hot1 | hot2, 1.0, 0.0)
    before = cnt_sc[...] + _dot(tri_ref[...], hot.astype(BF16))
    rank1 = jnp.sum(jnp.where(hot1, before, 0.0), axis=-1, keepdims=True)
    rank2 = jnp.sum(jnp.where(hot2, before, 0.0), axis=-1, keepdims=True)
    cnt_sc[...] = cnt_sc[...] + jnp.sum(hot, axis=0, keepdims=True)
    cnt_ref[...] = cnt_sc[...]
    col = lax.broadcasted_iota(jnp.int32, route_ref.shape, 1)
    cols = [(i1 - N_GROUPS).astype(F32), (i2 - N_GROUPS).astype(F32), gate1, gate2, rank1, rank2]
    route = jnp.zeros(route_ref.shape, F32)
    for c, val in enumerate(cols):
        route = jnp.where(col == c, val, route)
    route_ref[...] = route


def _merge(x, a, r, wo_bf, n2, wr_hi, wr_lo, counts_in, tm):
    n = x.shape[0]
    row = lambda i: (i, 0)
    fix = lambda i: (0, 0)
    tri = jnp.asarray(np.tril(np.ones((tm, tm), np.float32), -1), BF16)
    return pl.pallas_call(
        _merge_kernel,
        grid=(n // tm,),
        in_specs=[pl.BlockSpec((tm, D_MODEL), row), pl.BlockSpec((tm, SEG_W), row),
                  pl.BlockSpec((tm, SEG_W), row), pl.BlockSpec((D_MODEL, D_MODEL), fix),
                  pl.BlockSpec((1, D_MODEL), fix), pl.BlockSpec((D_MODEL, ROUTE_W), fix),
                  pl.BlockSpec((D_MODEL, ROUTE_W), fix), pl.BlockSpec((tm, tm), fix),
                  pl.BlockSpec((1, ROUTE_W), fix)],
        out_specs=[pl.BlockSpec((tm, D_MODEL), row), _row_tile_spec(tm, lambda i: (i, 0, 0)),
                   pl.BlockSpec((tm, 8), row), pl.BlockSpec((1, ROUTE_W), fix)],
        out_shape=[jax.ShapeDtypeStruct((n, D_MODEL), F32), jax.ShapeDtypeStruct((n,) + ROW_TILE, F32),
                   jax.ShapeDtypeStruct((n, 8), F32), jax.ShapeDtypeStruct((1, ROUTE_W), F32)],
        scratch_shapes=[pltpu.VMEM((1, ROUTE_W), F32)],
        compiler_params=pltpu.CompilerParams(dimension_semantics=("arbitrary",),
                                             vmem_limit_bytes=VMEM_LIMIT),
        name="merge",
    )(x, a, r, wo_bf, n2, wr_hi, wr_lo, tri, counts_in)


def _dispatch_kernel(pend_ref, pcnt_ref, n_used_ref, dest_hbm, hp_ref, hs_ref, xs_hbm,
                     idx_sm, zero_sc, sem_idx, sem_out, *, tm, steps_p, steps_s, first_spare, n_blocks):
    i = pl.program_id(0)
    slot = i % 2
    n_steps = steps_p + steps_s

    def idx_copy(step, s):
        return pltpu.make_async_copy(dest_hbm.at[step], idx_sm.at[s], sem_idx.at[s])

    @pl.when(i == 0)
    def _():
        idx_copy(0, 0).start()
        zero_sc[...] = jnp.zeros(zero_sc.shape, F32)

        def fill(start):
            start = pl.multiple_of(start, EXPERT_ROWS)
            return pltpu.make_async_copy(zero_sc, xs_hbm.at[pl.ds(start, EXPERT_ROWS)], sem_out)

        for action in ("start", "wait"):
            for e in range(N_EXPERTS):
                pl.when(pcnt_ref[e] > 0)(
                    lambda e=e: getattr(fill(pend_ref[e] - EXPERT_ROWS), action)())
            for blk in range(first_spare, n_blocks):
                pl.when(blk >= n_used_ref[0])(
                    lambda blk=blk: getattr(fill(blk * EXPERT_ROWS), action)())

    idx_copy(i, slot).wait()

    @pl.when(i + 1 < n_steps)
    def _():
        idx_copy(i + 1, 1 - slot).start()

    def scatter(h_ref):
        for r in range(2 * tm):
            pltpu.make_async_copy(h_ref.at[r % tm], xs_hbm.at[idx_sm[slot, r]], sem_out).start()
        for _ in range(2):
            pltpu.make_async_copy(h_ref, xs_hbm.at[pl.ds(0, tm)], sem_out).wait()

    pl.when(i < steps_p)(lambda: scatter(hp_ref))
    pl.when(i >= steps_p)(lambda: scatter(hs_ref))


def _dispatch(h2_p, h2_s, dest_tbl, pends, pcounts, n_used, n_rows, tm):
    steps_p = h2_p.shape[0] // tm
    steps_s = h2_s.shape[0] // tm
    n_blocks = n_rows // EXPERT_ROWS
    first_spare = (h2_p.shape[0] + h2_s.shape[0]) * TOP_K // EXPERT_ROWS
    grid_spec = pltpu.PrefetchScalarGridSpec(
        num_scalar_prefetch=3, grid=(steps_p + steps_s,),
        in_specs=[pl.BlockSpec(memory_space=pl.ANY),
                  _row_tile_spec(tm, lambda i, pe, pc, nu: (jnp.minimum(i, steps_p - 1), 0, 0)),
                  _row_tile_spec(tm, lambda i, pe, pc, nu: (jnp.maximum(i - steps_p, 0), 0, 0))],
        out_specs=pl.BlockSpec(memory_space=pl.ANY),
        scratch_shapes=[pltpu.SMEM((2, 2 * tm), jnp.int32), pltpu.VMEM((EXPERT_ROWS,) + ROW_TILE, F32),
                        pltpu.SemaphoreType.DMA((2,)), pltpu.SemaphoreType.DMA])
    return pl.pallas_call(
        functools.partial(_dispatch_kernel, tm=tm, steps_p=steps_p, steps_s=steps_s,
                          first_spare=first_spare, n_blocks=n_blocks),
        grid_spec=grid_spec,
        out_shape=jax.ShapeDtypeStruct((n_rows,) + ROW_TILE, F32),
        compiler_params=pltpu.CompilerParams(dimension_semantics=("arbitrary",),
                                             vmem_limit_bytes=VMEM_LIMIT),
        name="dispatch",
    )(pends, pcounts, n_used, dest_tbl, h2_p, h2_s)


def _experts_kernel(blk_exp_ref, n_used_ref, x_ref, wg_ref, wu_ref, wd_ref, y_ref, wg_sc, wu_sc, wd_sc):
    i = pl.program_id(0)

    @pl.when(i < n_used_ref[0])
    def _():
        prev = blk_exp_ref[jnp.maximum(i - 1, 0)]

        @pl.when((i == 0) | (prev != blk_exp_ref[i]))
        def _():
            wg_sc[...] = wg_ref[0].astype(BF16)
            wu_sc[...] = wu_ref[0].astype(BF16)
            wd_sc[...] = wd_ref[0].astype(BF16)

        xb = _load_row_tiles(x_ref).astype(BF16)
        gate = _dot(xb, wg_sc[...])
        up = _dot(xb, wu_sc[...])
        act = (gate * _sigmoid(gate) * up).astype(BF16)
        _store_row_tiles(y_ref, _dot(act, wd_sc[...]))

    @pl.when(i >= n_used_ref[0])
    def _():
        y_ref[...] = jnp.zeros(y_ref.shape, F32)


def _experts(xs, blk_exp, n_used, wg, wu, wd):
    n_blocks = int(blk_exp.shape[0])
    wmap = lambda i, be, nu: (be[i], 0, 0)
    xmap = lambda i, be, nu: (jnp.minimum(i, nu[0] - 1), 0, 0)
    grid_spec = pltpu.PrefetchScalarGridSpec(
        num_scalar_prefetch=2,
        grid=(n_blocks,),
        in_specs=[_row_tile_spec(EXPERT_ROWS, xmap),
                  pl.BlockSpec((1, D_MODEL, D_EXPERT), wmap),
                  pl.BlockSpec((1, D_MODEL, D_EXPERT), wmap),
                  pl.BlockSpec((1, D_EXPERT, D_MODEL), wmap)],
        out_specs=_row_tile_spec(EXPERT_ROWS, lambda i, be, nu: (i, 0, 0)),
        scratch_shapes=[pltpu.VMEM((D_MODEL, D_EXPERT), BF16), pltpu.VMEM((D_MODEL, D_EXPERT), BF16),
                        pltpu.VMEM((D_EXPERT, D_MODEL), BF16)],
    )
    return pl.pallas_call(
        _experts_kernel,
        grid_spec=grid_spec,
        out_shape=jax.ShapeDtypeStruct(xs.shape, F32),
        compiler_params=pltpu.CompilerParams(dimension_semantics=("arbitrary",),
                                             vmem_limit_bytes=VMEM_LIMIT),
        name="experts",
    )(blk_exp, n_used, xs, wg, wu, wd)


def _expert_layout(counts_row, n_tok):
    counts = counts_row[0, N_GROUPS:N_GROUPS + N_EXPERTS].astype(jnp.int32)
    pcounts = (counts + EXPERT_ROWS - 1) // EXPERT_ROWS * EXPERT_ROWS
    pends = jnp.cumsum(pcounts).astype(jnp.int32)
    pstarts = pends - pcounts
    n_blocks = -(-(n_tok * TOP_K) // EXPERT_ROWS) + N_EXPERTS
    blk_start = jnp.arange(n_blocks, dtype=jnp.int32) * EXPERT_ROWS
    blk_exp = jnp.minimum(jnp.sum(blk_start[:, None] >= pends[None, :], axis=1), N_EXPERTS - 1).astype(jnp.int32)
    n_used = (pends[-1:] // EXPERT_ROWS).astype(jnp.int32)
    return pstarts, pends, pcounts, blk_exp, n_used, n_blocks * EXPERT_ROWS


def _dest_table(route, pstarts, tm):
    experts = route[:, 0:TOP_K].astype(jnp.int32)
    ranks = route[:, 2 * TOP_K:3 * TOP_K].astype(jnp.int32)
    dest = pstarts[experts] + ranks
    return dest.reshape(-1, tm, TOP_K).swapaxes(1, 2).reshape(-1, TOP_K * tm)


def _final_kernel(dest_hbm, ys_hbm, y_ref, route_ref, w_ref, o_ref, idx_sm, gbuf, sem_idx, sem_g,
                  *, tm, n_steps):
    i = pl.program_id(0)

    def idx_copy(step):
        return pltpu.make_async_copy(dest_hbm.at[step], idx_sm.at[step % 3], sem_idx.at[step % 3])

    def gather(step):
        for r in range(2 * tm):
            pltpu.make_async_copy(ys_hbm.at[idx_sm[step % 3, r]], gbuf.at[step % 2, r],
                                  sem_g.at[step % 2]).start()

    @pl.when(i == 0)
    def _():
        idx_copy(0).start()
        idx_copy(0).wait()
        gather(0)
        if n_steps > 1:
            idx_copy(1).start()

    if n_steps > 1:
        @pl.when(i + 1 < n_steps)
        def _():
            idx_copy(i + 1).wait()
            gather(i + 1)

        if n_steps > 2:
            @pl.when(i + 2 < n_steps)
            def _():
                idx_copy(i + 2).start()

    pltpu.make_async_copy(ys_hbm.at[pl.ds(0, 2 * tm)], gbuf.at[i % 2], sem_g.at[i % 2]).wait()
    route = route_ref[...]
    rows = _load_row_tiles(gbuf, i % 2)
    y = y_ref[...] + route[:, 2:3] * rows[0:tm] + route[:, 3:4] * rows[tm:]
    o_ref[...] = _rms(y, w_ref[...])


def _final(ymid, route, dest_tbl, ys, w, tm):
    n = ymid.shape[0]
    n_steps = n // tm
    return pl.pallas_call(
        functools.partial(_final_kernel, tm=tm, n_steps=n_steps),
        grid=(n_steps,),
        in_specs=[pl.BlockSpec(memory_space=pl.ANY), pl.BlockSpec(memory_space=pl.ANY),
                  pl.BlockSpec((tm, D_MODEL), lambda i: (i, 0)), pl.BlockSpec((tm, 8), lambda i: (i, 0)),
                  pl.BlockSpec((1, D_MODEL), lambda i: (0, 0))],
        out_specs=pl.BlockSpec((tm, D_MODEL), lambda i: (i, 0)),
        out_shape=jax.ShapeDtypeStruct((n, D_MODEL), F32),
        scratch_shapes=[pltpu.SMEM((3, 2 * tm), jnp.int32), pltpu.VMEM((2, 2 * tm) + ROW_TILE, F32),
                        pltpu.SemaphoreType.DMA((3,)), pltpu.SemaphoreType.DMA((2,))],
        compiler_params=pltpu.CompilerParams(dimension_semantics=("arbitrary",),
                                             vmem_limit_bytes=VMEM_LIMIT),
        name="final",
    )(dest_tbl, ys, ymid, route, w)


def kernel(x_prompt, x_sample, cache_k, cache_v, state_rec, w_in, lam_q1, lam_k1, lam_q2, lam_k2,
           subln_w, lb_param, gnorm_w, w_out, norm1_w, norm2_w, w_group, w_router,
           w_e_gate, w_e_up, w_e_down, final_w):
    assert w_in.shape[0] == 1 and lb_param.shape[0] == 2, "single-layer model"
    bp, sp, _ = x_prompt.shape
    bs, ts, _ = x_sample.shape
    past = cache_k.shape[2]
    assert sp % CHUNK == 0 and past % CHUNK == 0 and ts <= CHUNK and ts & (ts - 1) == 0
    n_p, n_s = bp * sp, bs * ts
    n_all = n_p + n_s
    tm = 512 if (n_p % 512 == 0 and n_s % 512 == 0) else math.gcd(n_p, n_s)

    w_in_bf = w_in[0].astype(BF16)
    w_out_bf = w_out[0].astype(BF16)
    n1 = norm1_w[0].reshape(1, D_MODEL)
    n2 = norm2_w[0].reshape(1, D_MODEL)
    lam_vecs = jnp.stack([lam_q1[0], lam_k1[0], lam_q2[0], lam_k2[0]]).astype(F32)
    sub_w = subln_w[0].reshape(1, HEAD_W)
    gn_w = gnorm_w[0].reshape(1, HEAD_W)
    w_route = jnp.zeros((D_MODEL, ROUTE_W), F32)
    w_route = w_route.at[:, 0:N_GROUPS].set(w_group[0]).at[:, N_GROUPS:N_GROUPS + N_EXPERTS].set(w_router[0])
    wr_hi = w_route.astype(BF16)
    wr_lo = (w_route - wr_hi.astype(F32)).astype(BF16)

    xp = x_prompt.reshape(n_p, D_MODEL)
    xs = x_sample.reshape(n_s, D_MODEL)

    _, kf, vf, kb, _, qr, logf, kr, ir, gr, qt, vt = _inproj(xp, n1, w_in_bf, lb_param, tm)
    a_p = _attn_prompt(qt, kb, vt, lam_vecs, sub_w.reshape(HEAD_W, 1), bp, sp)
    chunks = 8 if sp % (8 * CHUNK) == 0 else 1
    r_p, state_p = _hgrn(qr, logf, kr, ir, gr, gn_w, None, bp, sp, CHUNK, chunks)
    ymid_p, h2_p, route_p, counts_p = _merge(xp, a_p, r_p, w_out_bf, n2, wr_hi, wr_lo,
                                             jnp.zeros((1, ROUTE_W), F32), tm)
    k_prompt = kf.reshape(1, bp, sp, A_HEADS, HEAD_W)
    v_prompt = vf.reshape(1, bp, sp, A_HEADS, HEAD_W)

    q, kf, vf, kb, vb, qr, logf, kr, ir, gr, _, _ = _inproj(xs, n1, w_in_bf, lb_param, tm)
    a_s = _attn_sample(q, cache_k, cache_v, kb, vb, lam_vecs, sub_w, bs, ts)
    r_s, state_s = _hgrn(qr, logf, kr, ir, gr, gn_w, state_rec[0], bs, ts, ts, 1)
    ymid_s, h2_s, route_s, counts = _merge(xs, a_s, r_s, w_out_bf, n2, wr_hi, wr_lo, counts_p, tm)
    k_sample = kf.reshape(1, bs, ts, A_HEADS, HEAD_W)
    v_sample = vf.reshape(1, bs, ts, A_HEADS, HEAD_W)

    pstarts, pends, pcounts, blk_exp, n_used, n_rows = _expert_layout(counts, n_all)
    td = min(256, math.gcd(n_p, n_s))
    dest_p = _dest_table(route_p, pstarts, td)
    dest_s = _dest_table(route_s, pstarts, td)
    xs_rows = _dispatch(h2_p, h2_s, jnp.concatenate([dest_p, dest_s], axis=0), pends, pcounts, n_used,
                        n_rows, td)
    ys_rows = _experts(xs_rows, blk_exp, n_used, w_e_gate[0], w_e_up[0], w_e_down[0])

    fw = final_w.reshape(1, D_MODEL)
    y_prompt = _final(ymid_p, route_p, dest_p, ys_rows, fw, td).reshape(bp, sp, D_MODEL)
    y_sample = _final(ymid_s, route_s, dest_s, ys_rows, fw, td).reshape(bs, ts, D_MODEL)
    return (y_prompt, y_sample, k_prompt, v_prompt, state_p[None], k_sample, v_sample, state_s[None])
```

```python
import functools
import math

import numpy as np
import jax
import jax.numpy as jnp
from jax import lax
from jax.experimental import pallas as pl
from jax.experimental.pallas import tpu as pltpu

F32 = jnp.float32
BF16 = jnp.bfloat16

D_MODEL = 1024
RMS_EPS = 1e-6
CHUNK = 64
A_HEADS = 4
A_HEAD_DIM = 64
HEAD_W = 2 * A_HEAD_DIM
KEY_BLOCK = 256
VT_ROWS = HEAD_W + 16
LOG2_E = math.log2(math.e)
R_HEADS = 4
SEG_W = 512
N_SEG = 7
N_GROUPS = 4
EXPERTS_PER_GROUP = 8
N_EXPERTS = N_GROUPS * EXPERTS_PER_GROUP
TOP_K = 2
D_EXPERT = 512
EXPERT_ROWS = 256
ROUTE_W = 128
LAM_INIT = 0.8 - 0.6 * math.exp(-0.3 * 0)
VMEM_LIMIT = 56 * 1024 * 1024


def _sigmoid(x):
    return 1.0 / (1.0 + jnp.exp(-x))


def _dot(a, b):
    return jnp.dot(a, b, preferred_element_type=F32)


def _dot_nt(a, b):
    return lax.dot_general(a, b, (((1,), (1,)), ((), ())), preferred_element_type=F32)


def _dot_tn(a, b):
    return lax.dot_general(a, b, (((0,), (0,)), ((), ())), preferred_element_type=F32)


def _rms(x, w):
    return x * lax.rsqrt(jnp.mean(x * x, axis=-1, keepdims=True) + RMS_EPS) * w


def _inproj_kernel(x_ref, n1_ref, w_ref, lbp_ref, q_ref, kf_ref, vf_ref, kb_ref, vb_ref,
                   qr_ref, logf_ref, kr_ref, ir_ref, gr_ref, qt_ref, vt_ref):
    h = _rms(x_ref[...], n1_ref[...]).astype(BF16)

    def seg(i):
        return _dot(h, w_ref[:, i * SEG_W:(i + 1) * SEG_W])

    def transposed(val, s):
        return val[s * KEY_BLOCK:(s + 1) * KEY_BLOCK, :].T.astype(BF16)

    q = seg(0) * (A_HEAD_DIM ** -0.5)
    q_ref[...] = q.astype(BF16)
    k = seg(1)
    for hd in range(A_HEADS):
        kf_ref[:, hd, :] = k[:, hd * HEAD_W:(hd + 1) * HEAD_W]
    kb_ref[...] = k.astype(BF16)
    v = seg(2)
    for hd in range(A_HEADS):
        vf_ref[:, hd, :] = v[:, hd * HEAD_W:(hd + 1) * HEAD_W]
    vb_ref[...] = v.astype(BF16)
    ones = jnp.ones((VT_ROWS - HEAD_W, KEY_BLOCK), BF16)
    for s in range(qt_ref.shape[0]):
        qt_ref[s] = transposed(q * LOG2_E, s)
        vt = transposed(v, s)
        for hd in range(A_HEADS):
            vt_ref[s, hd * VT_ROWS:hd * VT_ROWS + HEAD_W, :] = vt[hd * HEAD_W:(hd + 1) * HEAD_W, :]
            vt_ref[s, hd * VT_ROWS + HEAD_W:(hd + 1) * VT_ROWS, :] = ones
    qr = seg(3)
    qr_ref[...] = qr * _sigmoid(qr)
    p = lbp_ref[...]
    e = jnp.exp(p - jnp.max(p, axis=0, keepdims=True))
    lb = e[0:1] / jnp.sum(e, axis=0, keepdims=True)
    f = lb + (1.0 - lb) * _sigmoid(seg(4))
    logf_ref[...] = jnp.log(f)
    kr_ref[...] = 1.0 - f
    ir_ref[...] = seg(5).astype(BF16)
    gr_ref[...] = seg(6)


def _inproj(x, n1, w_bf, lbp, tm):
    n = x.shape[0]
    row = lambda i: (i, 0)
    fix = lambda i: (0, 0)
    out = lambda dt: jax.ShapeDtypeStruct((n, SEG_W), dt)
    ospec = pl.BlockSpec((tm, SEG_W), row)
    hspec = pl.BlockSpec((tm, A_HEADS, HEAD_W), lambda i: (i, 0, 0))
    hout = jax.ShapeDtypeStruct((n, A_HEADS, HEAD_W), F32)
    slabs = tm // KEY_BLOCK
    tspec = lambda rows: pl.BlockSpec((slabs, rows, KEY_BLOCK), lambda i: (i, 0, 0))
    tout = lambda rows: jax.ShapeDtypeStruct((n // KEY_BLOCK, rows, KEY_BLOCK), BF16)
    return pl.pallas_call(
        _inproj_kernel,
        grid=(n // tm,),
        in_specs=[pl.BlockSpec((tm, D_MODEL), row), pl.BlockSpec((1, D_MODEL), fix),
                  pl.BlockSpec((D_MODEL, N_SEG * SEG_W), fix), pl.BlockSpec(lbp.shape, fix)],
        out_specs=[ospec, hspec, hspec] + [ospec] * 7 + [tspec(SEG_W), tspec(A_HEADS * VT_ROWS)],
        out_shape=[out(BF16), hout, hout, out(BF16), out(BF16),
                   out(F32), out(F32), out(F32), out(BF16), out(F32), tout(SEG_W), tout(A_HEADS * VT_ROWS)],
        compiler_params=pltpu.CompilerParams(dimension_semantics=("arbitrary",),
                                             vmem_limit_bytes=VMEM_LIMIT),
        name="inproj",
    )(x, n1, w_bf, lbp)


def _lam_value(lam_ref):
    l = lam_ref[...]
    s1 = jnp.sum(l[0:1] * l[1:2], axis=-1, keepdims=True)
    s2 = jnp.sum(l[2:3] * l[3:4], axis=-1, keepdims=True)
    return jnp.exp(s1) - jnp.exp(s2) + LAM_INIT


def _split_maps(q):
    lane = lax.broadcasted_iota(jnp.int32, q.shape, 1)
    zero = jnp.zeros_like(q)
    return jnp.concatenate([jnp.where(lane < A_HEAD_DIM, q, zero),
                            jnp.where(lane >= A_HEAD_DIM, q, zero)], axis=0)


def _attn_finish(o_num, l, lam, sub_w, tq):
    o = o_num[0:tq] / l[0:tq] - lam * (o_num[tq:] / l[tq:])
    return (_rms(o, sub_w) * (1.0 - LAM_INIT)).astype(BF16)


def _attn_prompt_kernel(lam_ref, sub_ref, bias_ref, qt_ref, k_ref, vt_ref, o_ref,
                        qbd_sc, m_sc, acc_sc, sa_sc, sb_sc, *, qblocks):
    i = pl.program_id(2)
    kb = KEY_BLOCK

    sub = lax.broadcasted_iota(jnp.int32, (HEAD_W, kb), 0)
    for r in range(2 * qblocks):
        qt = qt_ref[r % qblocks]
        keep = (sub < A_HEAD_DIM) if r < qblocks else (sub >= A_HEAD_DIM)
        qbd_sc[r] = jnp.where(keep, qt, jnp.zeros_like(qt))
    m_sc[...] = jnp.full(m_sc.shape, -jnp.inf, F32)
    acc_sc[...] = jnp.zeros(acc_sc.shape, F32)

    def scores(r, key_block):
        return _dot(k_ref[pl.ds(pl.multiple_of(key_block * kb, kb), kb), :], qbd_sc[r])

    def update(r, key_block, st):
        m_old = m_sc[r:r + 1, :]
        m_new = jnp.maximum(m_old, jnp.max(st, axis=0, keepdims=True))
        p = jnp.exp2(st - m_new).astype(BF16)
        acc_sc[r] = jnp.exp2(m_old - m_new) * acc_sc[r] + _dot(vt_ref[key_block], p)
        m_sc[r:r + 1, :] = m_new

    n_q = 2 * qblocks
    first = i * qblocks
    for r in range(n_q):
        sa_sc[r] = scores(r, 0)

    def visible_tile(t, carry):
        for j in range(qblocks):
            cur, nxt = (sa_sc, sb_sc) if j % 2 == 0 else (sb_sc, sa_sc)
            for r in range(n_q):
                nxt[r] = scores(r, t * qblocks + j + 1)
                update(r, t * qblocks + j, cur[r])
        return carry

    lax.fori_loop(0, i, visible_tile, 0)

    work = [(kl, r) for kl in range(qblocks) for r in range(n_q) if r % qblocks >= kl]
    ahead = 2
    pending = {}
    for n, (kl, r) in enumerate(work):
        for kl2, r2 in work[n:n + 1 + ahead]:
            if kl2 > 0 and (kl2, r2) not in pending:
                pending[(kl2, r2)] = scores(r2, first + kl2)
        st = sa_sc[r] if kl == 0 else pending.pop((kl, r))
        update(r, first + kl, st + bias_ref[...] if r % qblocks == kl else st)

    lam = _lam_value(lam_ref)
    for ql in range(qblocks):
        a1 = acc_sc[ql]
        a2 = acc_sc[qblocks + ql]
        ot = a1[0:HEAD_W] / a1[HEAD_W:HEAD_W + 1] - lam * (a2[0:HEAD_W] / a2[HEAD_W:HEAD_W + 1])
        ot = ot * lax.rsqrt(jnp.mean(ot * ot, axis=0, keepdims=True) + RMS_EPS) * sub_ref[...]
        o_ref[ql * kb:(ql + 1) * kb, :] = (ot * (1.0 - LAM_INIT)).T.astype(BF16)


def _attn_prompt(qt, k, vt, lam_vecs, sub_col, batch, seq):
    kb = KEY_BLOCK
    tq = min(1024, seq)
    qblocks = tq // kb
    assert qblocks % 2 == 0, "the key-block loop is unrolled by two"
    nq = seq // tq
    pos = np.arange(kb) // CHUNK
    bias = jnp.asarray(np.where(pos[:, None] <= pos[None, :], 0.0, -np.inf), F32)
    fix = lambda b, h, i: (0, 0)
    return pl.pallas_call(
        functools.partial(_attn_prompt_kernel, qblocks=qblocks),
        grid=(batch, A_HEADS, nq),
        in_specs=[pl.BlockSpec((4, A_HEAD_DIM), fix), pl.BlockSpec((HEAD_W, 1), fix),
                  pl.BlockSpec((kb, kb), fix),
                  pl.BlockSpec((qblocks, HEAD_W, kb), lambda b, h, i: (b * nq + i, h, 0)),
                  pl.BlockSpec((seq, HEAD_W), lambda b, h, i: (b, h)),
                  pl.BlockSpec((seq // kb, VT_ROWS, kb), lambda b, h, i: (b, h, 0))],
        out_specs=pl.BlockSpec((tq, HEAD_W), lambda b, h, i: (b * nq + i, h)),
        out_shape=jax.ShapeDtypeStruct(k.shape, BF16),
        scratch_shapes=[pltpu.VMEM((2 * qblocks, HEAD_W, kb), BF16), pltpu.VMEM((2 * qblocks, kb), F32),
                        pltpu.VMEM((2 * qblocks, VT_ROWS, kb), F32),
                        pltpu.VMEM((2 * qblocks, kb, kb), F32), pltpu.VMEM((2 * qblocks, kb, kb), F32)],
        compiler_params=pltpu.CompilerParams(
            dimension_semantics=("arbitrary", "arbitrary", "arbitrary"), vmem_limit_bytes=VMEM_LIMIT),
        name="attn_prompt",
    )(lam_vecs, sub_col, bias, qt, k, vt)


def _attn_sample_kernel(lam_ref, sub_ref, q_ref, kc_ref, vc_ref, kn_ref, vn_ref, o_ref, *, t):
    lam = _lam_value(lam_ref)
    for h in range(A_HEADS):
        cs = slice(h * HEAD_W, (h + 1) * HEAD_W)
        qbd = _split_maps(q_ref[:, cs])
        kc = kc_ref[0, 0, :, h, :].astype(BF16)
        vc = vc_ref[0, 0, :, h, :].astype(BF16)
        s_past = _dot_nt(qbd, kc)
        s_new = _dot_nt(qbd, kn_ref[:, cs])
        m = jnp.maximum(jnp.max(s_past, axis=-1, keepdims=True), jnp.max(s_new, axis=-1, keepdims=True))
        p_past = jnp.exp(s_past - m)
        p_new = jnp.exp(s_new - m)
        l = jnp.sum(p_past, axis=-1, keepdims=True) + jnp.sum(p_new, axis=-1, keepdims=True)
        o_num = _dot(p_past.astype(BF16), vc) + _dot(p_new.astype(BF16), vn_ref[:, cs])
        o_ref[:, cs] = _attn_finish(o_num, l, lam, sub_ref[...], t)


def _attn_sample(q, cache_k, cache_v, k_new, v_new, lam_vecs, sub_w, batch, t):
    past = cache_k.shape[2]
    fix = lambda b: (0, 0)
    row = lambda b: (b, 0)
    cache = pl.BlockSpec((1, 1, past, A_HEADS, HEAD_W), lambda b: (0, b, 0, 0, 0))
    return pl.pallas_call(
        functools.partial(_attn_sample_kernel, t=t),
        grid=(batch,),
        in_specs=[pl.BlockSpec((4, A_HEAD_DIM), fix), pl.BlockSpec((1, HEAD_W), fix),
                  pl.BlockSpec((t, SEG_W), row),
                  cache, cache, pl.BlockSpec((t, SEG_W), row), pl.BlockSpec((t, SEG_W), row)],
        out_specs=pl.BlockSpec((t, SEG_W), row),
        out_shape=jax.ShapeDtypeStruct(q.shape, BF16),
        compiler_params=pltpu.CompilerParams(dimension_semantics=("arbitrary",),
                                             vmem_limit_bytes=VMEM_LIMIT),
        name="attn_sample",
    )(lam_vecs, sub_w, q, cache_k, cache_v, k_new, v_new)


def _hgrn_consts(length):
    t = np.arange(length)[:, None]
    j = np.arange(length)[None, :]
    sums = [j <= t]
    masks = [j == t]
    blk = length
    while blk >= 2:
        half = blk // 2
        mid_t = (t // blk) * blk + half
        mid_j = (j // blk) * blk + half
        sums.append(np.where(t >= mid_t, (j >= mid_t) & (j <= t), (j > t) & (j < mid_t)))
        masks.append((t // blk == j // blk) & (t >= mid_t) & (j < mid_j))
        blk = half
    sums = np.tile(np.concatenate(sums, axis=0).astype(np.float32), (1, 2))
    masks = np.concatenate(masks, axis=0).astype(np.float32)
    return jnp.asarray(sums, BF16), jnp.asarray(masks, F32)


def _hgrn_kernel(*refs, length, chunks, has_state_in):
    if has_state_in:
        (sums_ref, masks_ref, gn_ref, q_ref, logf_ref, k_ref, v_ref, g_ref, s0_ref,
         r_ref, sout_ref, st_sc) = refs
    else:
        (sums_ref, masks_ref, gn_ref, q_ref, logf_ref, k_ref, v_ref, g_ref,
         r_ref, sout_ref, st_sc) = refs
    step = pl.program_id(1)
    levels = int(math.log2(length))
    L = length
    group = math.gcd(chunks, 4)

    @pl.when(step == 0)
    def _():
        for h in range(R_HEADS):
            if has_state_in:
                st_sc[h] = s0_ref[0, h].T
            else:
                st_sc[h] = jnp.zeros(st_sc.shape[1:], F32)

    heads = [slice(h * HEAD_W, (h + 1) * HEAD_W) for h in range(R_HEADS)]

    def chunk_group(c, carry):
        rows = [pl.ds(pl.multiple_of((c * group + g) * L, L), L) for g in range(group)]
        expo = []
        for g in range(group):
            logf = logf_ref[rows[g], :]
            hi = logf.astype(BF16)
            lo = (logf - hi.astype(F32)).astype(BF16)
            expo.append(_dot(sums_ref[...], jnp.concatenate([hi, lo], axis=0)))
        streams = [(g, h) for g in range(group) for h in range(R_HEADS)]
        intra, q_dec, upd = {}, {}, {}
        for g, h in streams:
            cs = heads[h]
            q = q_ref[rows[g], cs]
            k = k_ref[rows[g], cs]
            v = v_ref[rows[g], cs]
            a = masks_ref[0:L, :] * _dot_nt(q.astype(BF16), k.astype(BF16))
            for lv in range(levels):
                x = jnp.exp(expo[g][(1 + lv) * L:(2 + lv) * L, cs])
                a = a + masks_ref[(1 + lv) * L:(2 + lv) * L, :] * _dot_nt(
                    (q * x).astype(BF16), (k * x).astype(BF16))
            intra[g, h] = _dot(a.astype(BF16), v)
            b = expo[g][0:L, cs]
            q_dec[g, h] = (q * jnp.exp(b)).astype(BF16)
            k_dec = (k * jnp.exp(b[L - 1:L, :] - b)).astype(BF16)
            upd[g, h] = _dot_tn(v, k_dec)
        for g, h in streams:
            cs = heads[h]
            st = st_sc[h]
            o = intra[g, h] + _dot_nt(q_dec[g, h], st.astype(BF16))
            st_sc[h] = st * jnp.exp(expo[g][L - 1:L, cs]) + upd[g, h]
            gate = g_ref[rows[g], cs]
            r_ref[rows[g], cs] = (_rms(o, gn_ref[...]) * (gate * _sigmoid(gate))).astype(BF16)
        return carry

    lax.fori_loop(0, chunks // group, chunk_group, 0)

    @pl.when(step == pl.num_programs(1) - 1)
    def _():
        for h in range(R_HEADS):
            sout_ref[0, h] = st_sc[h].T


def _hgrn(q, logf, k, v, g, gn_w, state_in, batch, seq, length, chunks):
    sums, masks = _hgrn_consts(length)
    tm = length * chunks
    steps = seq // tm
    fix = lambda b, s: (0, 0)
    row = lambda b, s: (b * steps + s, 0)
    tok = pl.BlockSpec((tm, SEG_W), row)
    in_specs = [pl.BlockSpec(sums.shape, fix), pl.BlockSpec(masks.shape, fix), pl.BlockSpec((1, HEAD_W), fix),
                tok, tok, tok, tok, tok]
    args = [sums, masks, gn_w, q, logf, k, v, g]
    state_spec = pl.BlockSpec((1, R_HEADS, HEAD_W, HEAD_W), lambda b, s: (b, 0, 0, 0))
    if state_in is not None:
        in_specs.append(state_spec)
        args.append(state_in)
    return pl.pallas_call(
        functools.partial(_hgrn_kernel, length=length, chunks=chunks, has_state_in=state_in is not None),
        grid=(batch, steps),
        in_specs=in_specs,
        out_specs=[tok, state_spec],
        out_shape=[jax.ShapeDtypeStruct(q.shape, BF16),
                   jax.ShapeDtypeStruct((batch, R_HEADS, HEAD_W, HEAD_W), F32)],
        scratch_shapes=[pltpu.VMEM((R_HEADS, HEAD_W, HEAD_W), F32)],
        compiler_params=pltpu.CompilerParams(dimension_semantics=("arbitrary", "arbitrary"),
                                             vmem_limit_bytes=VMEM_LIMIT),
        name="hgrn",
    )(*args)


def _merge_kernel(x_ref, a_ref, r_ref, wo_ref, n2_ref, wr_hi_ref, wr_lo_ref, tri_ref, cnt0_ref,
                  ymid_ref, h2_ref, route_ref, cnt_ref, cnt_sc):
    step = pl.program_id(0)

    @pl.when(step == 0)
    def _():
        cnt_sc[...] = cnt0_ref[...]

    sub = tri_ref.shape[0]
    for s in range(x_ref.shape[0] // sub):
        _merge_rows(slice(s * sub, (s + 1) * sub), x_ref, a_ref, r_ref, wo_ref, n2_ref, wr_hi_ref, wr_lo_ref,
                    tri_ref, ymid_ref, h2_ref, route_ref, cnt_sc)
    cnt_ref[...] = cnt_sc[...]


def _merge_rows(rows, x_ref, a_ref, r_ref, wo_ref, n2_ref, wr_hi_ref, wr_lo_ref, tri_ref,
                ymid_ref, h2_ref, route_ref, cnt_sc):
    half = a_ref.shape[1]
    y = x_ref[rows, :] + _dot(a_ref[rows, :], wo_ref[0:half, :]) + _dot(r_ref[rows, :], wo_ref[half:, :])
    ymid_ref[rows, :] = y
    hn = _rms(y, n2_ref[...])
    h2_ref[rows, :] = hn
    hi = hn.astype(BF16)
    lo = (hn - hi.astype(F32)).astype(BF16)
    logits = _dot(hi, wr_hi_ref[...]) + _dot(lo, wr_hi_ref[...]) + _dot(hi, wr_lo_ref[...])
    lane = lax.broadcasted_iota(jnp.int32, logits.shape, 1)
    neg = -jnp.inf
    big = jnp.int32(ROUTE_W)

    def top1(mask):
        val = jnp.max(jnp.where(mask, logits, neg), axis=-1, keepdims=True)
        idx = jnp.min(jnp.where(mask & (logits == val), lane, big), axis=-1, keepdims=True)
        return val, idx

    is_group = lane < N_GROUPS
    g_max, g_sel = top1(is_group)
    p_group = 1.0 / jnp.sum(jnp.where(is_group, jnp.exp(logits - g_max), 0.0), axis=-1, keepdims=True)
    first = N_GROUPS + g_sel * EXPERTS_PER_GROUP
    in_group = (lane >= first) & (lane < first + EXPERTS_PER_GROUP)
    v1, i1 = top1(in_group)
    v2, i2 = top1(in_group & (lane != i1))
    e2 = jnp.exp(v2 - v1)
    gate1 = p_group * (1.0 / (1.0 + e2))
    gate2 = p_group * (e2 / (1.0 + e2))
    hot1 = lane == i1
    hot2 = lane == i2
    hot = jnp.where(hot1 | hot2, 1.0, 0.0)
    before = cnt_sc[...] + _dot(tri_ref[...], hot.astype(BF16))
    rank1 = jnp.sum(jnp.where(hot1, before, 0.0), axis=-1, keepdims=True)
    rank2 = jnp.sum(jnp.where(hot2, before, 0.0), axis=-1, keepdims=True)
    cnt_sc[...] = cnt_sc[...] + jnp.sum(hot, axis=0, keepdims=True)
    col = lax.broadcasted_iota(jnp.int32, (hot.shape[0], route_ref.shape[1]), 1)
    cols = [(i1 - N_GROUPS).astype(F32), (i2 - N_GROUPS).astype(F32), gate1, gate2, rank1, rank2]
    route = jnp.zeros(col.shape, F32)
    for c, val in enumerate(cols):
        route = jnp.where(col == c, val, route)
    route_ref[rows, :] = route


def _merge(x, a, r, wo_bf, n2, wr_hi, wr_lo, counts_in, tm):
    n = x.shape[0]
    row = lambda i: (i, 0)
    fix = lambda i: (0, 0)
    sub = min(256, tm)
    tri = jnp.asarray(np.tril(np.ones((sub, sub), np.float32), -1), BF16)
    return pl.pallas_call(
        _merge_kernel,
        grid=(n // tm,),
        in_specs=[pl.BlockSpec((tm, D_MODEL), row), pl.BlockSpec((tm, SEG_W), row),
                  pl.BlockSpec((tm, SEG_W), row), pl.BlockSpec((D_MODEL, D_MODEL), fix),
                  pl.BlockSpec((1, D_MODEL), fix), pl.BlockSpec((D_MODEL, ROUTE_W), fix),
                  pl.BlockSpec((D_MODEL, ROUTE_W), fix), pl.BlockSpec((sub, sub), fix),
                  pl.BlockSpec((1, ROUTE_W), fix)],
        out_specs=[pl.BlockSpec((tm, D_MODEL), row), pl.BlockSpec((tm, D_MODEL), row),
                   pl.BlockSpec((tm, 8), row), pl.BlockSpec((1, ROUTE_W), fix)],
        out_shape=[jax.ShapeDtypeStruct((n, D_MODEL), F32), jax.ShapeDtypeStruct((n, D_MODEL), F32),
                   jax.ShapeDtypeStruct((n, 8), F32), jax.ShapeDtypeStruct((1, ROUTE_W), F32)],
        scratch_shapes=[pltpu.VMEM((1, ROUTE_W), F32)],
        compiler_params=pltpu.CompilerParams(dimension_semantics=("arbitrary",),
                                             vmem_limit_bytes=VMEM_LIMIT),
        name="merge",
    )(x, a, r, wo_bf, n2, wr_hi, wr_lo, tri, counts_in)


def _dispatch_kernel(pend_ref, pcnt_ref, n_used_ref, dest_hbm, hp_ref, hs_ref, xs_hbm,
                     idx_sm, zero_sc, sem_idx, sem_out, *, tm, steps_p, steps_s, first_spare, n_blocks):
    i = pl.program_id(0)
    slot = i % 2
    n_steps = steps_p + steps_s

    def idx_copy(step, s):
        return pltpu.make_async_copy(dest_hbm.at[step], idx_sm.at[s], sem_idx.at[s])

    @pl.when(i == 0)
    def _():
        idx_copy(0, 0).start()
        zero_sc[...] = jnp.zeros(zero_sc.shape, F32)

        def fill(start):
            start = pl.multiple_of(start, EXPERT_ROWS)
            return pltpu.make_async_copy(zero_sc, xs_hbm.at[pl.ds(start, EXPERT_ROWS)], sem_out)

        for action in ("start", "wait"):
            for e in range(N_EXPERTS):
                pl.when(pcnt_ref[e] > 0)(
                    lambda e=e: getattr(fill(pend_ref[e] - EXPERT_ROWS), action)())
            for blk in range(first_spare, n_blocks):
                pl.when(blk >= n_used_ref[0])(
                    lambda blk=blk: getattr(fill(blk * EXPERT_ROWS), action)())

    idx_copy(i, slot).wait()

    @pl.when(i + 1 < n_steps)
    def _():
        idx_copy(i + 1, 1 - slot).start()

    def scatter(h_ref):
        for r in range(2 * tm):
            pltpu.make_async_copy(h_ref.at[pl.ds(r % tm, 1)], xs_hbm.at[pl.ds(idx_sm[slot, r], 1)],
                                  sem_out).start(priority=r % 2)
        for _ in range(2):
            pltpu.make_async_copy(h_ref, xs_hbm.at[pl.ds(0, tm)], sem_out).wait()

    pl.when(i < steps_p)(lambda: scatter(hp_ref))
    pl.when(i >= steps_p)(lambda: scatter(hs_ref))


def _dispatch(h2_p, h2_s, dest_tbl, pends, pcounts, n_used, n_rows, tm):
    steps_p = h2_p.shape[0] // tm
    steps_s = h2_s.shape[0] // tm
    n_blocks = n_rows // EXPERT_ROWS
    first_spare = (h2_p.shape[0] + h2_s.shape[0]) * TOP_K // EXPERT_ROWS
    grid_spec = pltpu.PrefetchScalarGridSpec(
        num_scalar_prefetch=3, grid=(steps_p + steps_s,),
        in_specs=[pl.BlockSpec(memory_space=pl.ANY),
                  pl.BlockSpec((tm, D_MODEL), lambda i, pe, pc, nu: (jnp.minimum(i, steps_p - 1), 0)),
                  pl.BlockSpec((tm, D_MODEL), lambda i, pe, pc, nu: (jnp.maximum(i - steps_p, 0), 0))],
        out_specs=pl.BlockSpec(memory_space=pl.ANY),
        scratch_shapes=[pltpu.SMEM((2, 2 * tm), jnp.int32), pltpu.VMEM((EXPERT_ROWS, D_MODEL), F32),
                        pltpu.SemaphoreType.DMA((2,)), pltpu.SemaphoreType.DMA])
    return pl.pallas_call(
        functools.partial(_dispatch_kernel, tm=tm, steps_p=steps_p, steps_s=steps_s,
                          first_spare=first_spare, n_blocks=n_blocks),
        grid_spec=grid_spec,
        out_shape=jax.ShapeDtypeStruct((n_rows, D_MODEL), F32),
        compiler_params=pltpu.CompilerParams(dimension_semantics=("arbitrary",),
                                             vmem_limit_bytes=VMEM_LIMIT),
        name="dispatch",
    )(pends, pcounts, n_used, dest_tbl, h2_p, h2_s)


def _experts_kernel(blk_exp_ref, n_used_ref, x_ref, wg_ref, wu_ref, wd_ref, y_ref, wg_sc, wu_sc, wd_sc):
    i = pl.program_id(0)

    @pl.when(i < n_used_ref[0])
    def _():
        prev = blk_exp_ref[jnp.maximum(i - 1, 0)]

        @pl.when((i == 0) | (prev != blk_exp_ref[i]))
        def _():
            wg_sc[...] = wg_ref[0].astype(BF16)
            wu_sc[...] = wu_ref[0].astype(BF16)
            wd_sc[...] = wd_ref[0].astype(BF16)

        xb = x_ref[...].astype(BF16)
        gate = _dot(xb, wg_sc[...])
        up = _dot(xb, wu_sc[...])
        act = (gate * _sigmoid(gate) * up).astype(BF16)
        y_ref[...] = _dot(act, wd_sc[...])

    @pl.when(i >= n_used_ref[0])
    def _():
        y_ref[...] = jnp.zeros(y_ref.shape, F32)


def _experts(xs, blk_exp, n_used, wg, wu, wd):
    n_blocks = int(blk_exp.shape[0])
    wmap = lambda i, be, nu: (be[i], 0, 0)
    xmap = lambda i, be, nu: (jnp.minimum(i, nu[0] - 1), 0)
    grid_spec = pltpu.PrefetchScalarGridSpec(
        num_scalar_prefetch=2,
        grid=(n_blocks,),
        in_specs=[pl.BlockSpec((EXPERT_ROWS, D_MODEL), xmap),
                  pl.BlockSpec((1, D_MODEL, D_EXPERT), wmap),
                  pl.BlockSpec((1, D_MODEL, D_EXPERT), wmap),
                  pl.BlockSpec((1, D_EXPERT, D_MODEL), wmap)],
        out_specs=pl.BlockSpec((EXPERT_ROWS, D_MODEL), lambda i, be, nu: (i, 0)),
        scratch_shapes=[pltpu.VMEM((D_MODEL, D_EXPERT), BF16), pltpu.VMEM((D_MODEL, D_EXPERT), BF16),
                        pltpu.VMEM((D_EXPERT, D_MODEL), BF16)],
    )
    return pl.pallas_call(
        _experts_kernel,
        grid_spec=grid_spec,
        out_shape=jax.ShapeDtypeStruct(xs.shape, F32),
        compiler_params=pltpu.CompilerParams(dimension_semantics=("arbitrary",),
                                             vmem_limit_bytes=VMEM_LIMIT),
        name="experts",
    )(blk_exp, n_used, xs, wg, wu, wd)


def _expert_layout(counts_row, n_tok):
    counts = counts_row[0, N_GROUPS:N_GROUPS + N_EXPERTS].astype(jnp.int32)
    pcounts = (counts + EXPERT_ROWS - 1) // EXPERT_ROWS * EXPERT_ROWS
    pends = jnp.cumsum(pcounts).astype(jnp.int32)
    pstarts = pends - pcounts
    n_blocks = -(-(n_tok * TOP_K) // EXPERT_ROWS) + N_EXPERTS
    blk_start = jnp.arange(n_blocks, dtype=jnp.int32) * EXPERT_ROWS
    blk_exp = jnp.minimum(jnp.sum(blk_start[:, None] >= pends[None, :], axis=1), N_EXPERTS - 1).astype(jnp.int32)
    n_used = (pends[-1:] // EXPERT_ROWS).astype(jnp.int32)
    return pstarts, pends, pcounts, blk_exp, n_used, n_blocks * EXPERT_ROWS


def _dest_table(route, pstarts, tm):
    experts = route[:, 0:TOP_K].astype(jnp.int32)
    ranks = route[:, 2 * TOP_K:3 * TOP_K].astype(jnp.int32)
    dest = pstarts[experts] + ranks
    return dest.reshape(-1, tm, TOP_K).swapaxes(1, 2).reshape(-1, TOP_K * tm)


def _final_kernel(dest_hbm, ys_hbm, y_ref, route_ref, w_ref, o_ref, idx_sm, gbuf, sem_idx, sem_g,
                  *, tm, n_steps):
    i = pl.program_id(0)

    def idx_copy(step):
        return pltpu.make_async_copy(dest_hbm.at[step], idx_sm.at[step % 3], sem_idx.at[step % 3])

    def gather(step):
        for r in range(2 * tm):
            pltpu.make_async_copy(ys_hbm.at[pl.ds(idx_sm[step % 3, r], 1)],
                                  gbuf.at[step % 2, pl.ds(r, 1)], sem_g.at[step % 2]).start(priority=r % 2)

    @pl.when(i == 0)
    def _():
        idx_copy(0).start()
        idx_copy(0).wait()
        gather(0)
        if n_steps > 1:
            idx_copy(1).start()

    if n_steps > 1:
        @pl.when(i + 1 < n_steps)
        def _():
            idx_copy(i + 1).wait()
            gather(i + 1)

        if n_steps > 2:
            @pl.when(i + 2 < n_steps)
            def _():
                idx_copy(i + 2).start()

    pltpu.make_async_copy(ys_hbm.at[pl.ds(0, 2 * tm)], gbuf.at[i % 2], sem_g.at[i % 2]).wait()
    route = route_ref[...]
    rows = gbuf[i % 2]
    y = y_ref[...] + route[:, 2:3] * rows[0:tm] + route[:, 3:4] * rows[tm:]
    o_ref[...] = _rms(y, w_ref[...])


def _final(ymid, route, dest_tbl, ys, w, tm):
    n = ymid.shape[0]
    n_steps = n // tm
    return pl.pallas_call(
        functools.partial(_final_kernel, tm=tm, n_steps=n_steps),
        grid=(n_steps,),
        in_specs=[pl.BlockSpec(memory_space=pl.ANY), pl.BlockSpec(memory_space=pl.ANY),
                  pl.BlockSpec((tm, D_MODEL), lambda i: (i, 0)), pl.BlockSpec((tm, 8), lambda i: (i, 0)),
                  pl.BlockSpec((1, D_MODEL), lambda i: (0, 0))],
        out_specs=pl.BlockSpec((tm, D_MODEL), lambda i: (i, 0)),
        out_shape=jax.ShapeDtypeStruct((n, D_MODEL), F32),
        scratch_shapes=[pltpu.SMEM((3, 2 * tm), jnp.int32), pltpu.VMEM((2, 2 * tm, D_MODEL), F32),
                        pltpu.SemaphoreType.DMA((3,)), pltpu.SemaphoreType.DMA((2,))],
        compiler_params=pltpu.CompilerParams(dimension_semantics=("arbitrary",),
                                             vmem_limit_bytes=VMEM_LIMIT),
        name="final",
    )(dest_tbl, ys, ymid, route, w)


def kernel(x_prompt, x_sample, cache_k, cache_v, state_rec, w_in, lam_q1, lam_k1, lam_q2, lam_k2,
           subln_w, lb_param, gnorm_w, w_out, norm1_w, norm2_w, w_group, w_router,
           w_e_gate, w_e_up, w_e_down, final_w):
    assert w_in.shape[0] == 1 and lb_param.shape[0] == 2, "single-layer model"
    bp, sp, _ = x_prompt.shape
    bs, ts, _ = x_sample.shape
    past = cache_k.shape[2]
    assert sp % CHUNK == 0 and past % CHUNK == 0 and ts <= CHUNK and ts & (ts - 1) == 0
    n_p, n_s = bp * sp, bs * ts
    n_all = n_p + n_s
    tm = 512 if (n_p % 512 == 0 and n_s % 512 == 0) else math.gcd(n_p, n_s)

    w_in_bf = w_in[0].astype(BF16)
    w_out_bf = w_out[0].astype(BF16)
    n1 = norm1_w[0].reshape(1, D_MODEL)
    n2 = norm2_w[0].reshape(1, D_MODEL)
    lam_vecs = jnp.stack([lam_q1[0], lam_k1[0], lam_q2[0], lam_k2[0]]).astype(F32)
    sub_w = subln_w[0].reshape(1, HEAD_W)
    gn_w = gnorm_w[0].reshape(1, HEAD_W)
    w_route = jnp.zeros((D_MODEL, ROUTE_W), F32)
    w_route = w_route.at[:, 0:N_GROUPS].set(w_group[0]).at[:, N_GROUPS:N_GROUPS + N_EXPERTS].set(w_router[0])
    wr_hi = w_route.astype(BF16)
    wr_lo = (w_route - wr_hi.astype(F32)).astype(BF16)

    xp = x_prompt.reshape(n_p, D_MODEL)
    xs = x_sample.reshape(n_s, D_MODEL)

    _, kf, vf, kb, _, qr, logf, kr, ir, gr, qt, vt = _inproj(xp, n1, w_in_bf, lb_param, tm)
    a_p = _attn_prompt(qt, kb, vt, lam_vecs, sub_w.reshape(HEAD_W, 1), bp, sp)
    chunks = 8 if sp % (8 * CHUNK) == 0 else 1
    r_p, state_p = _hgrn(qr, logf, kr, ir, gr, gn_w, None, bp, sp, CHUNK, chunks)
    ymid_p, h2_p, route_p, counts_p = _merge(xp, a_p, r_p, w_out_bf, n2, wr_hi, wr_lo,
                                             jnp.zeros((1, ROUTE_W), F32), tm)
    k_prompt = kf.reshape(1, bp, sp, A_HEADS, HEAD_W)
    v_prompt = vf.reshape(1, bp, sp, A_HEADS, HEAD_W)

    q, kf, vf, kb, vb, qr, logf, kr, ir, gr, _, _ = _inproj(xs, n1, w_in_bf, lb_param, tm)
    a_s = _attn_sample(q, cache_k, cache_v, kb, vb, lam_vecs, sub_w, bs, ts)
    r_s, state_s = _hgrn(qr, logf, kr, ir, gr, gn_w, state_rec[0], bs, ts, ts, 1)
    ymid_s, h2_s, route_s, counts = _merge(xs, a_s, r_s, w_out_bf, n2, wr_hi, wr_lo, counts_p, tm)
    k_sample = kf.reshape(1, bs, ts, A_HEADS, HEAD_W)
    v_sample = vf.reshape(1, bs, ts, A_HEADS, HEAD_W)

    pstarts, pends, pcounts, blk_exp, n_used, n_rows = _expert_layout(counts, n_all)
    td = min(256, math.gcd(n_p, n_s))
    dest_p = _dest_table(route_p, pstarts, td)
    dest_s = _dest_table(route_s, pstarts, td)
    xs_rows = _dispatch(h2_p, h2_s, jnp.concatenate([dest_p, dest_s], axis=0), pends, pcounts, n_used,
                        n_rows, td)
    ys_rows = _experts(xs_rows, blk_exp, n_used, w_e_gate[0], w_e_up[0], w_e_down[0])

    fw = final_w.reshape(1, D_MODEL)
    y_prompt = _final(ymid_p, route_p, dest_p, ys_rows, fw, td).reshape(bp, sp, D_MODEL)
    y_sample = _final(ymid_s, route_s, dest_s, ys_rows, fw, td).reshape(bs, ts, D_MODEL)
    return (y_prompt, y_sample, k_prompt, v_prompt, state_p[None], k_sample, v_sample, state_s[None])
```

```python
import functools
import math

import numpy as np
import jax
import jax.numpy as jnp
from jax import lax
from jax.experimental import pallas as pl
from jax.experimental.pallas import tpu as pltpu

F32 = jnp.float32
BF16 = jnp.bfloat16

D_MODEL = 1024
RMS_EPS = 1e-6
CHUNK = 64
A_HEADS = 4
A_HEAD_DIM = 64
HEAD_W = 2 * A_HEAD_DIM
KEY_BLOCK = 256
VT_ROWS = HEAD_W + 16
LOG2_E = math.log2(math.e)
R_HEADS = 4
SEG_W = 512
N_SEG = 7
N_GROUPS = 4
EXPERTS_PER_GROUP = 8
N_EXPERTS = N_GROUPS * EXPERTS_PER_GROUP
TOP_K = 2
D_EXPERT = 512
EXPERT_ROWS = 256
ROUTE_W = 128
LAM_INIT = 0.8 - 0.6 * math.exp(-0.3 * 0)
VMEM_LIMIT = 56 * 1024 * 1024


def _sigmoid(x):
    return 1.0 / (1.0 + jnp.exp(-x))


def _dot(a, b):
    return jnp.dot(a, b, preferred_element_type=F32)


def _dot_nt(a, b):
    return lax.dot_general(a, b, (((1,), (1,)), ((), ())), preferred_element_type=F32)


def _dot_tn(a, b):
    return lax.dot_general(a, b, (((0,), (0,)), ((), ())), preferred_element_type=F32)


def _rms(x, w):
    return x * lax.rsqrt(jnp.mean(x * x, axis=-1, keepdims=True) + RMS_EPS) * w


def _inproj_kernel(x_ref, n1_ref, w_ref, lbp_ref, kf_ref, vf_ref, kb_ref, qr_ref, logf_ref, kr_ref,
                   ir_ref, gr_ref, qa_ref, va_ref, *, transposed_qv):
    h = _rms(x_ref[...], n1_ref[...]).astype(BF16)

    def seg(i):
        return _dot(h, w_ref[:, i * SEG_W:(i + 1) * SEG_W])

    def transposed(val, s):
        return val[s * KEY_BLOCK:(s + 1) * KEY_BLOCK, :].T.astype(BF16)

    q = seg(0) * (A_HEAD_DIM ** -0.5)
    k = seg(1)
    for hd in range(A_HEADS):
        kf_ref[:, hd, :] = k[:, hd * HEAD_W:(hd + 1) * HEAD_W]
    kb_ref[...] = k.astype(BF16)
    v = seg(2)
    for hd in range(A_HEADS):
        vf_ref[:, hd, :] = v[:, hd * HEAD_W:(hd + 1) * HEAD_W]
    if transposed_qv:
        ones = jnp.ones((VT_ROWS - HEAD_W, KEY_BLOCK), BF16)
        for s in range(qa_ref.shape[0]):
            qa_ref[s] = transposed(q * LOG2_E, s)
            vt = transposed(v, s)
            for hd in range(A_HEADS):
                va_ref[s, hd * VT_ROWS:hd * VT_ROWS + HEAD_W, :] = vt[hd * HEAD_W:(hd + 1) * HEAD_W, :]
                va_ref[s, hd * VT_ROWS + HEAD_W:(hd + 1) * VT_ROWS, :] = ones
    else:
        qa_ref[...] = q.astype(BF16)
        va_ref[...] = v.astype(BF16)
    qr = seg(3)
    qr_ref[...] = qr * _sigmoid(qr)
    p = lbp_ref[...]
    e = jnp.exp(p - jnp.max(p, axis=0, keepdims=True))
    lb = e[0:1] / jnp.sum(e, axis=0, keepdims=True)
    f = lb + (1.0 - lb) * _sigmoid(seg(4))
    logf_ref[...] = jnp.log(f)
    kr_ref[...] = 1.0 - f
    ir_ref[...] = seg(5).astype(BF16)
    gr_ref[...] = seg(6)


def _inproj(x, n1, w_bf, lbp, tm, transposed_qv):
    n = x.shape[0]
    row = lambda i: (i, 0)
    fix = lambda i: (0, 0)
    out = lambda dt: jax.ShapeDtypeStruct((n, SEG_W), dt)
    ospec = pl.BlockSpec((tm, SEG_W), row)
    hspec = pl.BlockSpec((tm, A_HEADS, HEAD_W), lambda i: (i, 0, 0))
    hout = jax.ShapeDtypeStruct((n, A_HEADS, HEAD_W), F32)
    slabs = tm // KEY_BLOCK
    tspec = lambda rows: pl.BlockSpec((slabs, rows, KEY_BLOCK), lambda i: (i, 0, 0))
    tout = lambda rows: jax.ShapeDtypeStruct((n // KEY_BLOCK, rows, KEY_BLOCK), BF16)
    if transposed_qv:
        qv_specs = [tspec(SEG_W), tspec(A_HEADS * VT_ROWS)]
        qv_shapes = [tout(SEG_W), tout(A_HEADS * VT_ROWS)]
    else:
        qv_specs = [ospec, ospec]
        qv_shapes = [out(BF16), out(BF16)]
    return pl.pallas_call(
        functools.partial(_inproj_kernel, transposed_qv=transposed_qv),
        grid=(n // tm,),
        in_specs=[pl.BlockSpec((tm, D_MODEL), row), pl.BlockSpec((1, D_MODEL), fix),
                  pl.BlockSpec((D_MODEL, N_SEG * SEG_W), fix), pl.BlockSpec(lbp.shape, fix)],
        out_specs=[hspec, hspec] + [ospec] * 6 + qv_specs,
        out_shape=[hout, hout, out(BF16), out(F32), out(F32), out(F32), out(BF16), out(F32)] + qv_shapes,
        compiler_params=pltpu.CompilerParams(dimension_semantics=("arbitrary",),
                                             vmem_limit_bytes=VMEM_LIMIT),
        name="inproj",
    )(x, n1, w_bf, lbp)


def _lam_value(lam_ref):
    l = lam_ref[...]
    s1 = jnp.sum(l[0:1] * l[1:2], axis=-1, keepdims=True)
    s2 = jnp.sum(l[2:3] * l[3:4], axis=-1, keepdims=True)
    return jnp.exp(s1) - jnp.exp(s2) + LAM_INIT


def _split_maps(q):
    lane = lax.broadcasted_iota(jnp.int32, q.shape, 1)
    zero = jnp.zeros_like(q)
    return jnp.concatenate([jnp.where(lane < A_HEAD_DIM, q, zero),
                            jnp.where(lane >= A_HEAD_DIM, q, zero)], axis=0)


def _attn_finish(o_num, l, lam, sub_w, tq):
    o = o_num[0:tq] / l[0:tq] - lam * (o_num[tq:] / l[tq:])
    return (_rms(o, sub_w) * (1.0 - LAM_INIT)).astype(BF16)


def _attn_prompt_kernel(lam_ref, sub_ref, bias_ref, qt_ref, k_ref, vt_ref, o_ref,
                        qbd_sc, m_sc, acc_sc, sa_sc, sb_sc, *, qblocks):
    i = pl.program_id(2)
    kb = KEY_BLOCK

    sub = lax.broadcasted_iota(jnp.int32, (HEAD_W, kb), 0)
    for r in range(2 * qblocks):
        qt = qt_ref[r % qblocks]
        keep = (sub < A_HEAD_DIM) if r < qblocks else (sub >= A_HEAD_DIM)
        qbd_sc[r] = jnp.where(keep, qt, jnp.zeros_like(qt))
    m_sc[...] = jnp.full(m_sc.shape, -jnp.inf, F32)
    acc_sc[...] = jnp.zeros(acc_sc.shape, F32)

    def scores(r, key_block):
        return _dot(k_ref[pl.ds(pl.multiple_of(key_block * kb, kb), kb), :], qbd_sc[r])

    def update(r, key_block, st):
        m_old = m_sc[r:r + 1, :]
        m_new = jnp.maximum(m_old, jnp.max(st, axis=0, keepdims=True))
        p = jnp.exp2(st - m_new).astype(BF16)
        acc_sc[r] = jnp.exp2(m_old - m_new) * acc_sc[r] + _dot(vt_ref[key_block], p)
        m_sc[r:r + 1, :] = m_new

    n_q = 2 * qblocks
    first = i * qblocks
    for r in range(n_q):
        sa_sc[r] = scores(r, 0)

    def visible_tile(t, carry):
        lead = 2
        chains = [(j, r) for j in range(qblocks) for r in range(n_q)]
        for c in range(len(chains) + lead):
            if c < len(chains):
                j, r = chains[c]
                (sb_sc if j % 2 == 0 else sa_sc)[r] = scores(r, t * qblocks + j + 1)
            if c >= lead:
                j, r = chains[c - lead]
                update(r, t * qblocks + j, (sa_sc if j % 2 == 0 else sb_sc)[r])
        return carry

    lax.fori_loop(0, i, visible_tile, 0)

    work = [(kl, r) for kl in range(qblocks) for r in range(n_q) if r % qblocks >= kl]
    ahead = 4
    pending = {}
    for n, (kl, r) in enumerate(work):
        for kl2, r2 in work[n:n + 1 + ahead]:
            if kl2 > 0 and (kl2, r2) not in pending:
                pending[(kl2, r2)] = scores(r2, first + kl2)
        st = sa_sc[r] if kl == 0 else pending.pop((kl, r))
        update(r, first + kl, st + bias_ref[...] if r % qblocks == kl else st)

    lam = _lam_value(lam_ref)
    for ql in range(qblocks):
        a1 = acc_sc[ql]
        a2 = acc_sc[qblocks + ql]
        ot = a1[0:HEAD_W] / a1[HEAD_W:HEAD_W + 1] - lam * (a2[0:HEAD_W] / a2[HEAD_W:HEAD_W + 1])
        ot = ot * lax.rsqrt(jnp.mean(ot * ot, axis=0, keepdims=True) + RMS_EPS) * sub_ref[...]
        o_ref[ql * kb:(ql + 1) * kb, :] = (ot * (1.0 - LAM_INIT)).T.astype(BF16)


def _attn_prompt(qt, k, vt, lam_vecs, sub_col, batch, seq):
    kb = KEY_BLOCK
    tq = min(1024, seq)
    qblocks = tq // kb
    assert qblocks % 2 == 0, "the key-block loop is unrolled by two"
    nq = seq // tq
    pos = np.arange(kb) // CHUNK
    bias = jnp.asarray(np.where(pos[:, None] <= pos[None, :], 0.0, -np.inf), F32)
    fix = lambda b, h, i: (0, 0)
    return pl.pallas_call(
        functools.partial(_attn_prompt_kernel, qblocks=qblocks),
        grid=(batch, A_HEADS, nq),
        in_specs=[pl.BlockSpec((4, A_HEAD_DIM), fix), pl.BlockSpec((HEAD_W, 1), fix),
                  pl.BlockSpec((kb, kb), fix),
                  pl.BlockSpec((qblocks, HEAD_W, kb), lambda b, h, i: (b * nq + i, h, 0)),
                  pl.BlockSpec((seq, HEAD_W), lambda b, h, i: (b, h)),
                  pl.BlockSpec((seq // kb, VT_ROWS, kb), lambda b, h, i: (b, h, 0))],
        out_specs=pl.BlockSpec((tq, HEAD_W), lambda b, h, i: (b * nq + i, h)),
        out_shape=jax.ShapeDtypeStruct(k.shape, BF16),
        scratch_shapes=[pltpu.VMEM((2 * qblocks, HEAD_W, kb), BF16), pltpu.VMEM((2 * qblocks, kb), F32),
                        pltpu.VMEM((2 * qblocks, VT_ROWS, kb), F32),
                        pltpu.VMEM((2 * qblocks, kb, kb), F32), pltpu.VMEM((2 * qblocks, kb, kb), F32)],
        compiler_params=pltpu.CompilerParams(
            dimension_semantics=("arbitrary", "arbitrary", "arbitrary"), vmem_limit_bytes=VMEM_LIMIT),
        name="attn_prompt",
    )(lam_vecs, sub_col, bias, qt, k, vt)


def _attn_sample_kernel(lam_ref, sub_ref, q_ref, kc_ref, vc_ref, kn_ref, vn_ref, o_ref, *, t):
    lam = _lam_value(lam_ref)
    for h in range(A_HEADS):
        cs = slice(h * HEAD_W, (h + 1) * HEAD_W)
        qbd = _split_maps(q_ref[:, cs])
        keys = [kc_ref[0, :, h, :].astype(BF16), kc_ref[0, :, A_HEADS + h, :].astype(BF16), kn_ref[:, cs]]
        vals = [vc_ref[0, :, h, :].astype(BF16), vc_ref[0, :, A_HEADS + h, :].astype(BF16), vn_ref[:, cs]]
        scores = [_dot_nt(qbd, k) for k in keys]
        m = functools.reduce(jnp.maximum, [jnp.max(sc, axis=-1, keepdims=True) for sc in scores])
        probs = [jnp.exp(sc - m) for sc in scores]
        l = sum(jnp.sum(p, axis=-1, keepdims=True) for p in probs)
        o_num = sum(_dot(p.astype(BF16), v) for p, v in zip(probs, vals))
        o_ref[:, cs] = _attn_finish(o_num, l, lam, sub_ref[...], t)


def _attn_sample(q, cache_k, cache_v, k_new, v_new, lam_vecs, sub_w, batch, t):
    fix = lambda b: (0, 0)
    row = lambda b: (b, 0)
    cache = pl.BlockSpec((1,) + cache_k.shape[1:], lambda b: (b, 0, 0, 0))
    return pl.pallas_call(
        functools.partial(_attn_sample_kernel, t=t),
        grid=(batch,),
        in_specs=[pl.BlockSpec((4, A_HEAD_DIM), fix), pl.BlockSpec((1, HEAD_W), fix),
                  pl.BlockSpec((t, SEG_W), row),
                  cache, cache, pl.BlockSpec((t, SEG_W), row), pl.BlockSpec((t, SEG_W), row)],
        out_specs=pl.BlockSpec((t, SEG_W), row),
        out_shape=jax.ShapeDtypeStruct(q.shape, BF16),
        compiler_params=pltpu.CompilerParams(dimension_semantics=("arbitrary",),
                                             vmem_limit_bytes=VMEM_LIMIT),
        name="attn_sample",
    )(lam_vecs, sub_w, q, cache_k, cache_v, k_new, v_new)


def _hgrn_consts(length):
    t = np.arange(length)[:, None]
    j = np.arange(length)[None, :]
    sums = [j <= t]
    masks = [j == t]
    blk = length
    while blk >= 2:
        half = blk // 2
        mid_t = (t // blk) * blk + half
        mid_j = (j // blk) * blk + half
        sums.append(np.where(t >= mid_t, (j >= mid_t) & (j <= t), (j > t) & (j < mid_t)))
        masks.append((t // blk == j // blk) & (t >= mid_t) & (j < mid_j))
        blk = half
    sums = np.tile(np.concatenate(sums, axis=0).astype(np.float32), (1, 2))
    masks = np.concatenate(masks, axis=0).astype(np.float32)
    return jnp.asarray(sums, BF16), jnp.asarray(masks, F32)


def _hgrn_kernel(*refs, length, chunks, has_state_in):
    if has_state_in:
        (sums_ref, masks_ref, gn_ref, q_ref, logf_ref, k_ref, v_ref, g_ref, s0_ref,
         r_ref, sout_ref, st_sc) = refs
    else:
        (sums_ref, masks_ref, gn_ref, q_ref, logf_ref, k_ref, v_ref, g_ref,
         r_ref, sout_ref, st_sc) = refs
    step = pl.program_id(1)
    levels = int(math.log2(length))
    L = length
    group = math.gcd(chunks, 4)

    @pl.when(step == 0)
    def _():
        for h in range(R_HEADS):
            if has_state_in:
                st_sc[h] = s0_ref[0, h].T
            else:
                st_sc[h] = jnp.zeros(st_sc.shape[1:], F32)

    heads = [slice(h * HEAD_W, (h + 1) * HEAD_W) for h in range(R_HEADS)]

    def chunk_group(c, carry):
        rows = [pl.ds(pl.multiple_of((c * group + g) * L, L), L) for g in range(group)]
        expo = []
        for g in range(group):
            logf = logf_ref[rows[g], :]
            hi = logf.astype(BF16)
            lo = (logf - hi.astype(F32)).astype(BF16)
            expo.append(_dot(sums_ref[...], jnp.concatenate([hi, lo], axis=0)))
        streams = [(g, h) for g in range(group) for h in range(R_HEADS)]
        intra, q_dec, upd = {}, {}, {}
        for g, h in streams:
            cs = heads[h]
            q = q_ref[rows[g], cs]
            k = k_ref[rows[g], cs]
            v = v_ref[rows[g], cs]
            a = masks_ref[0:L, :] * _dot_nt(q.astype(BF16), k.astype(BF16))
            for lv in range(levels):
                x = jnp.exp(expo[g][(1 + lv) * L:(2 + lv) * L, cs])
                a = a + masks_ref[(1 + lv) * L:(2 + lv) * L, :] * _dot_nt(
                    (q * x).astype(BF16), (k * x).astype(BF16))
            intra[g, h] = _dot(a.astype(BF16), v)
            b = expo[g][0:L, cs]
            q_dec[g, h] = (q * jnp.exp(b)).astype(BF16)
            k_dec = (k * jnp.exp(b[L - 1:L, :] - b)).astype(BF16)
            upd[g, h] = _dot_tn(v, k_dec)
        for g, h in streams:
            cs = heads[h]
            st = st_sc[h]
            o = intra[g, h] + _dot_nt(q_dec[g, h], st.astype(BF16))
            st_sc[h] = st * jnp.exp(expo[g][L - 1:L, cs]) + upd[g, h]
            gate = g_ref[rows[g], cs]
            r_ref[rows[g], cs] = (_rms(o, gn_ref[...]) * (gate * _sigmoid(gate))).astype(BF16)
        return carry

    lax.fori_loop(0, chunks // group, chunk_group, 0)

    @pl.when(step == pl.num_programs(1) - 1)
    def _():
        for h in range(R_HEADS):
            sout_ref[0, h] = st_sc[h].T


def _hgrn(q, logf, k, v, g, gn_w, state_in, batch, seq, length, chunks):
    sums, masks = _hgrn_consts(length)
    tm = length * chunks
    steps = seq // tm
    fix = lambda b, s: (0, 0)
    row = lambda b, s: (b * steps + s, 0)
    tok = pl.BlockSpec((tm, SEG_W), row)
    in_specs = [pl.BlockSpec(sums.shape, fix), pl.BlockSpec(masks.shape, fix), pl.BlockSpec((1, HEAD_W), fix),
                tok, tok, tok, tok, tok]
    args = [sums, masks, gn_w, q, logf, k, v, g]
    state_spec = pl.BlockSpec((1, R_HEADS, HEAD_W, HEAD_W), lambda b, s: (b, 0, 0, 0))
    if state_in is not None:
        in_specs.append(state_spec)
        args.append(state_in)
    return pl.pallas_call(
        functools.partial(_hgrn_kernel, length=length, chunks=chunks, has_state_in=state_in is not None),
        grid=(batch, steps),
        in_specs=in_specs,
        out_specs=[tok, state_spec],
        out_shape=[jax.ShapeDtypeStruct(q.shape, BF16),
                   jax.ShapeDtypeStruct((batch, R_HEADS, HEAD_W, HEAD_W), F32)],
        scratch_shapes=[pltpu.VMEM((R_HEADS, HEAD_W, HEAD_W), F32)],
        compiler_params=pltpu.CompilerParams(dimension_semantics=("arbitrary", "arbitrary"),
                                             vmem_limit_bytes=VMEM_LIMIT),
        name="hgrn",
    )(*args)


def _merge_kernel(x_ref, a_ref, r_ref, wo_ref, n2_ref, wr_hi_ref, wr_lo_ref, tri_ref, cnt0_ref,
                  ymid_ref, h2_ref, route_ref, cnt_ref, cnt_sc):
    step = pl.program_id(0)

    @pl.when(step == 0)
    def _():
        cnt_sc[...] = cnt0_ref[...]

    sub = tri_ref.shape[0]
    for s in range(x_ref.shape[0] // sub):
        _merge_rows(slice(s * sub, (s + 1) * sub), x_ref, a_ref, r_ref, wo_ref, n2_ref, wr_hi_ref, wr_lo_ref,
                    tri_ref, ymid_ref, h2_ref, route_ref, cnt_sc)
    cnt_ref[...] = cnt_sc[...]


def _merge_rows(rows, x_ref, a_ref, r_ref, wo_ref, n2_ref, wr_hi_ref, wr_lo_ref, tri_ref,
                ymid_ref, h2_ref, route_ref, cnt_sc):
    half = a_ref.shape[1]
    y = x_ref[rows, :] + _dot(a_ref[rows, :], wo_ref[0:half, :]) + _dot(r_ref[rows, :], wo_ref[half:, :])
    ymid_ref[rows, :] = y
    hn = _rms(y, n2_ref[...])
    h2_ref[rows, :] = hn
    hi = hn.astype(BF16)
    lo = (hn - hi.astype(F32)).astype(BF16)
    logits = _dot(hi, wr_hi_ref[...]) + _dot(lo, wr_hi_ref[...]) + _dot(hi, wr_lo_ref[...])
    lane = lax.broadcasted_iota(jnp.int32, logits.shape, 1)
    neg = -jnp.inf
    big = jnp.int32(ROUTE_W)

    def top1(mask):
        val = jnp.max(jnp.where(mask, logits, neg), axis=-1, keepdims=True)
        idx = jnp.min(jnp.where(mask & (logits == val), lane, big), axis=-1, keepdims=True)
        return val, idx

    is_group = lane < N_GROUPS
    g_max, g_sel = top1(is_group)
    p_group = 1.0 / jnp.sum(jnp.where(is_group, jnp.exp(logits - g_max), 0.0), axis=-1, keepdims=True)
    first = N_GROUPS + g_sel * EXPERTS_PER_GROUP
    in_group = (lane >= first) & (lane < first + EXPERTS_PER_GROUP)
    v1, i1 = top1(in_group)
    v2, i2 = top1(in_group & (lane != i1))
    e2 = jnp.exp(v2 - v1)
    gate1 = p_group * (1.0 / (1.0 + e2))
    gate2 = p_group * (e2 / (1.0 + e2))
    hot1 = lane == i1
    hot2 = lane == i2
    hot = jnp.where(hot1 | hot2, 1.0, 0.0)
    before = cnt_sc[...] + _dot(tri_ref[...], hot.astype(BF16))
    rank1 = jnp.sum(jnp.where(hot1, before, 0.0), axis=-1, keepdims=True)
    rank2 = jnp.sum(jnp.where(hot2, before, 0.0), axis=-1, keepdims=True)
    cnt_sc[...] = cnt_sc[...] + jnp.sum(hot, axis=0, keepdims=True)
    col = lax.broadcasted_iota(jnp.int32, (hot.shape[0], route_ref.shape[1]), 1)
    cols = [(i1 - N_GROUPS).astype(F32), (i2 - N_GROUPS).astype(F32), gate1, gate2, rank1, rank2]
    route = jnp.zeros(col.shape, F32)
    for c, val in enumerate(cols):
        route = jnp.where(col == c, val, route)
    route_ref[rows, :] = route


def _merge(x, a, r, wo_bf, n2, wr_hi, wr_lo, counts_in, tm):
    n = x.shape[0]
    row = lambda i: (i, 0)
    fix = lambda i: (0, 0)
    sub = min(256, tm)
    tri = jnp.asarray(np.tril(np.ones((sub, sub), np.float32), -1), BF16)
    return pl.pallas_call(
        _merge_kernel,
        grid=(n // tm,),
        in_specs=[pl.BlockSpec((tm, D_MODEL), row), pl.BlockSpec((tm, SEG_W), row),
                  pl.BlockSpec((tm, SEG_W), row), pl.BlockSpec((D_MODEL, D_MODEL), fix),
                  pl.BlockSpec((1, D_MODEL), fix), pl.BlockSpec((D_MODEL, ROUTE_W), fix),
                  pl.BlockSpec((D_MODEL, ROUTE_W), fix), pl.BlockSpec((sub, sub), fix),
                  pl.BlockSpec((1, ROUTE_W), fix)],
        out_specs=[pl.BlockSpec((tm, D_MODEL), row), pl.BlockSpec((tm, D_MODEL), row),
                   pl.BlockSpec((tm, 8), row), pl.BlockSpec((1, ROUTE_W), fix)],
        out_shape=[jax.ShapeDtypeStruct((n, D_MODEL), F32), jax.ShapeDtypeStruct((n, D_MODEL), F32),
                   jax.ShapeDtypeStruct((n, 8), F32), jax.ShapeDtypeStruct((1, ROUTE_W), F32)],
        scratch_shapes=[pltpu.VMEM((1, ROUTE_W), F32)],
        compiler_params=pltpu.CompilerParams(dimension_semantics=("arbitrary",),
                                             vmem_limit_bytes=VMEM_LIMIT),
        name="merge",
    )(x, a, r, wo_bf, n2, wr_hi, wr_lo, tri, counts_in)


def _dispatch_kernel(pend_ref, pcnt_ref, n_used_ref, dest_hbm, hp_ref, hs_ref, xs_hbm,
                     idx_sm, zero_sc, sem_idx, sem_out, *, tm, steps_p, steps_s, first_spare, n_blocks):
    i = pl.program_id(0)
    slot = i % 2
    n_steps = steps_p + steps_s

    def idx_copy(step, s):
        return pltpu.make_async_copy(dest_hbm.at[step], idx_sm.at[s], sem_idx.at[s])

    @pl.when(i == 0)
    def _():
        idx_copy(0, 0).start()
        zero_sc[...] = jnp.zeros(zero_sc.shape, F32)

        def fill(start):
            start = pl.multiple_of(start, EXPERT_ROWS)
            return pltpu.make_async_copy(zero_sc, xs_hbm.at[pl.ds(start, EXPERT_ROWS)], sem_out)

        for action in ("start", "wait"):
            for e in range(N_EXPERTS):
                pl.when(pcnt_ref[e] > 0)(
                    lambda e=e: getattr(fill(pend_ref[e] - EXPERT_ROWS), action)())
            for blk in range(first_spare, n_blocks):
                pl.when(blk >= n_used_ref[0])(
                    lambda blk=blk: getattr(fill(blk * EXPERT_ROWS), action)())

    idx_copy(i, slot).wait()

    @pl.when(i + 1 < n_steps)
    def _():
        idx_copy(i + 1, 1 - slot).start()

    def scatter(h_ref):
        for r in range(2 * tm):
            pltpu.make_async_copy(h_ref.at[pl.ds(r % tm, 1)], xs_hbm.at[pl.ds(idx_sm[slot, r], 1)],
                                  sem_out).start(priority=r % 2)
        for _ in range(2):
            pltpu.make_async_copy(h_ref, xs_hbm.at[pl.ds(0, tm)], sem_out).wait()

    pl.when(i < steps_p)(lambda: scatter(hp_ref))
    pl.when(i >= steps_p)(lambda: scatter(hs_ref))


def _dispatch(h2_p, h2_s, dest_tbl, pends, pcounts, n_used, n_rows, tm):
    steps_p = h2_p.shape[0] // tm
    steps_s = h2_s.shape[0] // tm
    n_blocks = n_rows // EXPERT_ROWS
    first_spare = (h2_p.shape[0] + h2_s.shape[0]) * TOP_K // EXPERT_ROWS
    grid_spec = pltpu.PrefetchScalarGridSpec(
        num_scalar_prefetch=3, grid=(steps_p + steps_s,),
        in_specs=[pl.BlockSpec(memory_space=pl.ANY),
                  pl.BlockSpec((tm, D_MODEL), lambda i, pe, pc, nu: (jnp.minimum(i, steps_p - 1), 0)),
                  pl.BlockSpec((tm, D_MODEL), lambda i, pe, pc, nu: (jnp.maximum(i - steps_p, 0), 0))],
        out_specs=pl.BlockSpec(memory_space=pl.ANY),
        scratch_shapes=[pltpu.SMEM((2, 2 * tm), jnp.int32), pltpu.VMEM((EXPERT_ROWS, D_MODEL), F32),
                        pltpu.SemaphoreType.DMA((2,)), pltpu.SemaphoreType.DMA])
    return pl.pallas_call(
        functools.partial(_dispatch_kernel, tm=tm, steps_p=steps_p, steps_s=steps_s,
                          first_spare=first_spare, n_blocks=n_blocks),
        grid_spec=grid_spec,
        out_shape=jax.ShapeDtypeStruct((n_rows, D_MODEL), F32),
        compiler_params=pltpu.CompilerParams(dimension_semantics=("arbitrary",),
                                             vmem_limit_bytes=VMEM_LIMIT),
        name="dispatch",
    )(pends, pcounts, n_used, dest_tbl, h2_p, h2_s)


def _experts_kernel(blk_exp_ref, n_used_ref, x_ref, wg_ref, wu_ref, wd_ref, y_ref, wg_sc, wu_sc, wd_sc):
    i = pl.program_id(0)

    @pl.when(i < n_used_ref[0])
    def _():
        prev = blk_exp_ref[jnp.maximum(i - 1, 0)]

        @pl.when((i == 0) | (prev != blk_exp_ref[i]))
        def _():
            wg_sc[...] = wg_ref[0].astype(BF16)
            wu_sc[...] = wu_ref[0].astype(BF16)
            wd_sc[...] = wd_ref[0].astype(BF16)

        xb = x_ref[...].astype(BF16)
        gate = _dot(xb, wg_sc[...])
        up = _dot(xb, wu_sc[...])
        act = (gate * _sigmoid(gate) * up).astype(BF16)
        y_ref[...] = _dot(act, wd_sc[...])

    @pl.when(i >= n_used_ref[0])
    def _():
        y_ref[...] = jnp.zeros(y_ref.shape, F32)


def _experts(xs, blk_exp, n_used, wg, wu, wd):
    n_blocks = int(blk_exp.shape[0])
    wmap = lambda i, be, nu: (be[i], 0, 0)
    xmap = lambda i, be, nu: (jnp.minimum(i, nu[0] - 1), 0)
    grid_spec = pltpu.PrefetchScalarGridSpec(
        num_scalar_prefetch=2,
        grid=(n_blocks,),
        in_specs=[pl.BlockSpec((EXPERT_ROWS, D_MODEL), xmap),
                  pl.BlockSpec((1, D_MODEL, D_EXPERT), wmap),
                  pl.BlockSpec((1, D_MODEL, D_EXPERT), wmap),
                  pl.BlockSpec((1, D_EXPERT, D_MODEL), wmap)],
        out_specs=pl.BlockSpec((EXPERT_ROWS, D_MODEL), lambda i, be, nu: (i, 0)),
        scratch_shapes=[pltpu.VMEM((D_MODEL, D_EXPERT), BF16), pltpu.VMEM((D_MODEL, D_EXPERT), BF16),
                        pltpu.VMEM((D_EXPERT, D_MODEL), BF16)],
    )
    return pl.pallas_call(
        _experts_kernel,
        grid_spec=grid_spec,
        out_shape=jax.ShapeDtypeStruct(xs.shape, F32),
        compiler_params=pltpu.CompilerParams(dimension_semantics=("arbitrary",),
                                             vmem_limit_bytes=VMEM_LIMIT),
        name="experts",
    )(blk_exp, n_used, xs, wg, wu, wd)


def _expert_layout(counts_row, n_tok):
    counts = counts_row[0, N_GROUPS:N_GROUPS + N_EXPERTS].astype(jnp.int32)
    pcounts = (counts + EXPERT_ROWS - 1) // EXPERT_ROWS * EXPERT_ROWS
    pends = jnp.cumsum(pcounts).astype(jnp.int32)
    pstarts = pends - pcounts
    n_blocks = -(-(n_tok * TOP_K) // EXPERT_ROWS) + N_EXPERTS
    blk_start = jnp.arange(n_blocks, dtype=jnp.int32) * EXPERT_ROWS
    blk_exp = jnp.minimum(jnp.sum(blk_start[:, None] >= pends[None, :], axis=1), N_EXPERTS - 1).astype(jnp.int32)
    n_used = (pends[-1:] // EXPERT_ROWS).astype(jnp.int32)
    return pstarts, pends, pcounts, blk_exp, n_used, n_blocks * EXPERT_ROWS


def _dest_table(route, pstarts, tm):
    experts = route[:, 0:TOP_K].astype(jnp.int32)
    ranks = route[:, 2 * TOP_K:3 * TOP_K].astype(jnp.int32)
    dest = pstarts[experts] + ranks
    return dest.reshape(-1, tm, TOP_K).swapaxes(1, 2).reshape(-1, TOP_K * tm)


def _final_kernel(dest_hbm, ys_hbm, y_ref, route_ref, w_ref, o_ref, idx_sm, gbuf, sem_idx, sem_g,
                  *, tm, n_steps):
    i = pl.program_id(0)

    def idx_copy(step):
        return pltpu.make_async_copy(dest_hbm.at[step], idx_sm.at[step % 3], sem_idx.at[step % 3])

    def gather(step):
        for r in range(2 * tm):
            pltpu.make_async_copy(ys_hbm.at[pl.ds(idx_sm[step % 3, r], 1)],
                                  gbuf.at[step % 2, pl.ds(r, 1)], sem_g.at[step % 2]).start(priority=r % 2)

    @pl.when(i == 0)
    def _():
        idx_copy(0).start()
        idx_copy(0).wait()
        gather(0)
        if n_steps > 1:
            idx_copy(1).start()

    if n_steps > 1:
        @pl.when(i + 1 < n_steps)
        def _():
            idx_copy(i + 1).wait()
            gather(i + 1)

        if n_steps > 2:
            @pl.when(i + 2 < n_steps)
            def _():
                idx_copy(i + 2).start()

    pltpu.make_async_copy(ys_hbm.at[pl.ds(0, 2 * tm)], gbuf.at[i % 2], sem_g.at[i % 2]).wait()
    route = route_ref[...]
    rows = gbuf[i % 2]
    y = y_ref[...] + route[:, 2:3] * rows[0:tm] + route[:, 3:4] * rows[tm:]
    o_ref[...] = _rms(y, w_ref[...])


def _final(ymid, route, dest_tbl, ys, w, tm):
    n = ymid.shape[0]
    n_steps = n // tm
    return pl.pallas_call(
        functools.partial(_final_kernel, tm=tm, n_steps=n_steps),
        grid=(n_steps,),
        in_specs=[pl.BlockSpec(memory_space=pl.ANY), pl.BlockSpec(memory_space=pl.ANY),
                  pl.BlockSpec((tm, D_MODEL), lambda i: (i, 0)), pl.BlockSpec((tm, 8), lambda i: (i, 0)),
                  pl.BlockSpec((1, D_MODEL), lambda i: (0, 0))],
        out_specs=pl.BlockSpec((tm, D_MODEL), lambda i: (i, 0)),
        out_shape=jax.ShapeDtypeStruct((n, D_MODEL), F32),
        scratch_shapes=[pltpu.SMEM((3, 2 * tm), jnp.int32), pltpu.VMEM((2, 2 * tm, D_MODEL), F32),
                        pltpu.SemaphoreType.DMA((3,)), pltpu.SemaphoreType.DMA((2,))],
        compiler_params=pltpu.CompilerParams(dimension_semantics=("arbitrary",),
                                             vmem_limit_bytes=VMEM_LIMIT),
        name="final",
    )(dest_tbl, ys, ymid, route, w)


def kernel(x_prompt, x_sample, cache_k, cache_v, state_rec, w_in, lam_q1, lam_k1, lam_q2, lam_k2,
           subln_w, lb_param, gnorm_w, w_out, norm1_w, norm2_w, w_group, w_router,
           w_e_gate, w_e_up, w_e_down, final_w):
    assert w_in.shape[0] == 1 and lb_param.shape[0] == 2, "single-layer model"
    bp, sp, _ = x_prompt.shape
    bs, ts, _ = x_sample.shape
    past = cache_k.shape[2]
    assert sp % CHUNK == 0 and past % CHUNK == 0 and ts <= CHUNK and ts & (ts - 1) == 0
    n_p, n_s = bp * sp, bs * ts
    n_all = n_p + n_s
    tm = 512 if (n_p % 512 == 0 and n_s % 512 == 0) else math.gcd(n_p, n_s)

    w_in_bf = w_in[0].astype(BF16)
    w_out_bf = w_out[0].astype(BF16)
    n1 = norm1_w[0].reshape(1, D_MODEL)
    n2 = norm2_w[0].reshape(1, D_MODEL)
    lam_vecs = jnp.stack([lam_q1[0], lam_k1[0], lam_q2[0], lam_k2[0]]).astype(F32)
    sub_w = subln_w[0].reshape(1, HEAD_W)
    gn_w = gnorm_w[0].reshape(1, HEAD_W)
    w_route = jnp.zeros((D_MODEL, ROUTE_W), F32)
    w_route = w_route.at[:, 0:N_GROUPS].set(w_group[0]).at[:, N_GROUPS:N_GROUPS + N_EXPERTS].set(w_router[0])
    wr_hi = w_route.astype(BF16)
    wr_lo = (w_route - wr_hi.astype(F32)).astype(BF16)

    xp = x_prompt.reshape(n_p, D_MODEL)
    xs = x_sample.reshape(n_s, D_MODEL)

    kf, vf, kb, qr, logf, kr, ir, gr, qt, vt = _inproj(xp, n1, w_in_bf, lb_param, tm, True)
    a_p = _attn_prompt(qt, kb, vt, lam_vecs, sub_w.reshape(HEAD_W, 1), bp, sp)
    chunks = 8 if sp % (8 * CHUNK) == 0 else 1
    r_p, state_p = _hgrn(qr, logf, kr, ir, gr, gn_w, None, bp, sp, CHUNK, chunks)
    ymid_p, h2_p, route_p, counts_p = _merge(xp, a_p, r_p, w_out_bf, n2, wr_hi, wr_lo,
                                             jnp.zeros((1, ROUTE_W), F32), tm)
    k_prompt = kf.reshape(1, bp, sp, A_HEADS, HEAD_W)
    v_prompt = vf.reshape(1, bp, sp, A_HEADS, HEAD_W)

    kf, vf, kb, qr, logf, kr, ir, gr, q, vb = _inproj(xs, n1, w_in_bf, lb_param, tm, False)
    pair_rows = (bs, past // 2, 2 * A_HEADS, HEAD_W)
    a_s = _attn_sample(q, cache_k.reshape(pair_rows), cache_v.reshape(pair_rows), kb, vb,
                       lam_vecs, sub_w, bs, ts)
    r_s, state_s = _hgrn(qr, logf, kr, ir, gr, gn_w, state_rec[0], bs, ts, ts, 1)
    ymid_s, h2_s, route_s, counts = _merge(xs, a_s, r_s, w_out_bf, n2, wr_hi, wr_lo, counts_p, tm)
    k_sample = kf.reshape(1, bs, ts, A_HEADS, HEAD_W)
    v_sample = vf.reshape(1, bs, ts, A_HEADS, HEAD_W)

    pstarts, pends, pcounts, blk_exp, n_used, n_rows = _expert_layout(counts, n_all)
    td = min(256, math.gcd(n_p, n_s))
    dest_p = _dest_table(route_p, pstarts, td)
    dest_s = _dest_table(route_s, pstarts, td)
    xs_rows = _dispatch(h2_p, h2_s, jnp.concatenate([dest_p, dest_s], axis=0), pends, pcounts, n_used,
                        n_rows, td)
    ys_rows = _experts(xs_rows, blk_exp, n_used, w_e_gate[0], w_e_up[0], w_e_down[0])

    fw = final_w.reshape(1, D_MODEL)
    y_prompt = _final(ymid_p, route_p, dest_p, ys_rows, fw, td).reshape(bp, sp, D_MODEL)
    y_sample = _final(ymid_s, route_s, dest_s, ys_rows, fw, td).reshape(bs, ts, D_MODEL)
    return (y_prompt, y_sample, k_prompt, v_prompt, state_p[None], k_sample, v_sample, state_s[None])
```

```python
import functools
import math

import numpy as np
import jax
import jax.numpy as jnp
from jax import lax
from jax.experimental import pallas as pl
from jax.experimental.pallas import tpu as pltpu

F32 = jnp.float32
BF16 = jnp.bfloat16

D_MODEL = 1024
RMS_EPS = 1e-6
CHUNK = 64
A_HEADS = 4
A_HEAD_DIM = 64
HEAD_W = 2 * A_HEAD_DIM
KEY_BLOCK = 256
VT_ROWS = HEAD_W + 16
LOG2_E = math.log2(math.e)
R_HEADS = 4
SEG_W = 512
N_SEG = 7
N_GROUPS = 4
EXPERTS_PER_GROUP = 8
N_EXPERTS = N_GROUPS * EXPERTS_PER_GROUP
TOP_K = 2
D_EXPERT = 512
EXPERT_ROWS = 256
ROUTE_W = 128
LAM_INIT = 0.8 - 0.6 * math.exp(-0.3 * 0)
VMEM_LIMIT = 56 * 1024 * 1024


def _sigmoid(x):
    return 1.0 / (1.0 + jnp.exp(-x))


def _dot(a, b):
    return jnp.dot(a, b, preferred_element_type=F32)


def _dot_nt(a, b):
    return lax.dot_general(a, b, (((1,), (1,)), ((), ())), preferred_element_type=F32)


def _dot_tn(a, b):
    return lax.dot_general(a, b, (((0,), (0,)), ((), ())), preferred_element_type=F32)


def _rms(x, w):
    return x * lax.rsqrt(jnp.mean(x * x, axis=-1, keepdims=True) + RMS_EPS) * w


def _inproj_kernel(x_ref, n1_ref, w_ref, lbp_ref, kf_ref, vf_ref, kb_ref, qr_ref, logf_ref, kr_ref,
                   ir_ref, gr_ref, qa_ref, va_ref, *, transposed_qv):
    h = _rms(x_ref[...], n1_ref[...]).astype(BF16)

    def seg(i):
        return _dot(h, w_ref[:, i * SEG_W:(i + 1) * SEG_W])

    def transposed(val, s):
        return val[s * KEY_BLOCK:(s + 1) * KEY_BLOCK, :].T.astype(BF16)

    q = seg(0) * (A_HEAD_DIM ** -0.5)
    k = seg(1)
    for hd in range(A_HEADS):
        kf_ref[:, hd, :] = k[:, hd * HEAD_W:(hd + 1) * HEAD_W]
    kb_ref[...] = k.astype(BF16)
    v = seg(2)
    for hd in range(A_HEADS):
        vf_ref[:, hd, :] = v[:, hd * HEAD_W:(hd + 1) * HEAD_W]
    if transposed_qv:
        ones = jnp.ones((VT_ROWS - HEAD_W, KEY_BLOCK), BF16)
        for s in range(qa_ref.shape[0]):
            qa_ref[s] = transposed(q * LOG2_E, s)
            vt = transposed(v, s)
            for hd in range(A_HEADS):
                va_ref[s, hd * VT_ROWS:hd * VT_ROWS + HEAD_W, :] = vt[hd * HEAD_W:(hd + 1) * HEAD_W, :]
                va_ref[s, hd * VT_ROWS + HEAD_W:(hd + 1) * VT_ROWS, :] = ones
    else:
        qa_ref[...] = q.astype(BF16)
        va_ref[...] = v.astype(BF16)
    qr = seg(3)
    qr_ref[...] = qr * _sigmoid(qr)
    p = lbp_ref[...]
    e = jnp.exp(p - jnp.max(p, axis=0, keepdims=True))
    lb = e[0:1] / jnp.sum(e, axis=0, keepdims=True)
    f = lb + (1.0 - lb) * _sigmoid(seg(4))
    logf_ref[...] = jnp.log(f)
    kr_ref[...] = 1.0 - f
    ir_ref[...] = seg(5).astype(BF16)
    gr_ref[...] = seg(6)


def _inproj(x, n1, w_bf, lbp, tm, transposed_qv):
    n = x.shape[0]
    row = lambda i: (i, 0)
    fix = lambda i: (0, 0)
    out = lambda dt: jax.ShapeDtypeStruct((n, SEG_W), dt)
    ospec = pl.BlockSpec((tm, SEG_W), row)
    hspec = pl.BlockSpec((tm, A_HEADS, HEAD_W), lambda i: (i, 0, 0))
    hout = jax.ShapeDtypeStruct((n, A_HEADS, HEAD_W), F32)
    slabs = tm // KEY_BLOCK
    tspec = lambda rows: pl.BlockSpec((slabs, rows, KEY_BLOCK), lambda i: (i, 0, 0))
    tout = lambda rows: jax.ShapeDtypeStruct((n // KEY_BLOCK, rows, KEY_BLOCK), BF16)
    if transposed_qv:
        qv_specs = [tspec(SEG_W), tspec(A_HEADS * VT_ROWS)]
        qv_shapes = [tout(SEG_W), tout(A_HEADS * VT_ROWS)]
    else:
        qv_specs = [ospec, ospec]
        qv_shapes = [out(BF16), out(BF16)]
    return pl.pallas_call(
        functools.partial(_inproj_kernel, transposed_qv=transposed_qv),
        grid=(n // tm,),
        in_specs=[pl.BlockSpec((tm, D_MODEL), row), pl.BlockSpec((1, D_MODEL), fix),
                  pl.BlockSpec((D_MODEL, N_SEG * SEG_W), fix), pl.BlockSpec(lbp.shape, fix)],
        out_specs=[hspec, hspec] + [ospec] * 6 + qv_specs,
        out_shape=[hout, hout, out(BF16), out(F32), out(F32), out(F32), out(BF16), out(F32)] + qv_shapes,
        compiler_params=pltpu.CompilerParams(dimension_semantics=("arbitrary",),
                                             vmem_limit_bytes=VMEM_LIMIT),
        name="inproj",
    )(x, n1, w_bf, lbp)


def _lam_value(lam_ref):
    l = lam_ref[...]
    s1 = jnp.sum(l[0:1] * l[1:2], axis=-1, keepdims=True)
    s2 = jnp.sum(l[2:3] * l[3:4], axis=-1, keepdims=True)
    return jnp.exp(s1) - jnp.exp(s2) + LAM_INIT


def _split_maps(q):
    lane = lax.broadcasted_iota(jnp.int32, q.shape, 1)
    zero = jnp.zeros_like(q)
    return jnp.concatenate([jnp.where(lane < A_HEAD_DIM, q, zero),
                            jnp.where(lane >= A_HEAD_DIM, q, zero)], axis=0)


def _attn_finish(o_num, l, lam, sub_w, tq):
    o = o_num[0:tq] / l[0:tq] - lam * (o_num[tq:] / l[tq:])
    return (_rms(o, sub_w) * (1.0 - LAM_INIT)).astype(BF16)


def _attn_prompt_kernel(lam_ref, sub_ref, bias_ref, qt_ref, k_ref, vt_ref, o_ref,
                        qbd_sc, m_sc, acc_sc, sa_sc, sb_sc, *, qblocks):
    i = pl.program_id(2)
    kb = KEY_BLOCK

    sub = lax.broadcasted_iota(jnp.int32, (HEAD_W, kb), 0)
    for r in range(2 * qblocks):
        qt = qt_ref[r % qblocks]
        keep = (sub < A_HEAD_DIM) if r < qblocks else (sub >= A_HEAD_DIM)
        qbd_sc[r] = jnp.where(keep, qt, jnp.zeros_like(qt))
    m_sc[...] = jnp.full(m_sc.shape, -jnp.inf, F32)
    acc_sc[...] = jnp.zeros(acc_sc.shape, F32)

    def scores(r, key_block):
        return _dot(k_ref[pl.ds(pl.multiple_of(key_block * kb, kb), kb), :], qbd_sc[r])

    def update(r, key_block, st):
        m_old = m_sc[r:r + 1, :]
        m_new = jnp.maximum(m_old, jnp.max(st, axis=0, keepdims=True))
        p = jnp.exp2(st - m_new).astype(BF16)
        acc_sc[r] = jnp.exp2(m_old - m_new) * acc_sc[r] + _dot(vt_ref[key_block], p)
        m_sc[r:r + 1, :] = m_new

    n_q = 2 * qblocks
    first = i * qblocks
    for r in range(n_q):
        sa_sc[r] = scores(r, 0)

    def visible_blocks(start, count):
        lead = 2
        chains = [(j, r) for j in range(count) for r in range(n_q)]
        for c in range(len(chains) + lead):
            if c < len(chains):
                j, r = chains[c]
                (sb_sc if j % 2 == 0 else sa_sc)[r] = scores(r, start + j + 1)
            if c >= lead:
                j, r = chains[c - lead]
                update(r, start + j, (sa_sc if j % 2 == 0 else sb_sc)[r])

    def tile_pair(t, carry):
        visible_blocks(2 * t * qblocks, 2 * qblocks)
        return carry

    lax.fori_loop(0, i // 2, tile_pair, 0)
    pl.when(i % 2 == 1)(lambda: visible_blocks((i - 1) * qblocks, qblocks))

    work = [(kl, r) for kl in range(qblocks) for r in range(n_q) if r % qblocks >= kl]
    ahead = 4
    pending = {}
    for n, (kl, r) in enumerate(work):
        for kl2, r2 in work[n:n + 1 + ahead]:
            if kl2 > 0 and (kl2, r2) not in pending:
                pending[(kl2, r2)] = scores(r2, first + kl2)
        st = sa_sc[r] if kl == 0 else pending.pop((kl, r))
        update(r, first + kl, st + bias_ref[...] if r % qblocks == kl else st)

    lam = _lam_value(lam_ref)
    for ql in range(qblocks):
        a1 = acc_sc[ql]
        a2 = acc_sc[qblocks + ql]
        ot = a1[0:HEAD_W] / a1[HEAD_W:HEAD_W + 1] - lam * (a2[0:HEAD_W] / a2[HEAD_W:HEAD_W + 1])
        ot = ot * lax.rsqrt(jnp.mean(ot * ot, axis=0, keepdims=True) + RMS_EPS) * sub_ref[...]
        o_ref[ql * kb:(ql + 1) * kb, :] = (ot * (1.0 - LAM_INIT)).T.astype(BF16)


def _attn_prompt(qt, k, vt, lam_vecs, sub_col, batch, seq):
    kb = KEY_BLOCK
    tq = min(1024, seq)
    qblocks = tq // kb
    assert qblocks % 2 == 0, "the key-block loop is unrolled by two"
    nq = seq // tq
    pos = np.arange(kb) // CHUNK
    bias = jnp.asarray(np.where(pos[:, None] <= pos[None, :], 0.0, -np.inf), F32)
    fix = lambda b, h, i: (0, 0)
    return pl.pallas_call(
        functools.partial(_attn_prompt_kernel, qblocks=qblocks),
        grid=(batch, A_HEADS, nq),
        in_specs=[pl.BlockSpec((4, A_HEAD_DIM), fix), pl.BlockSpec((HEAD_W, 1), fix),
                  pl.BlockSpec((kb, kb), fix),
                  pl.BlockSpec((qblocks, HEAD_W, kb), lambda b, h, i: (b * nq + i, h, 0)),
                  pl.BlockSpec((seq, HEAD_W), lambda b, h, i: (b, h)),
                  pl.BlockSpec((seq // kb, VT_ROWS, kb), lambda b, h, i: (b, h, 0))],
        out_specs=pl.BlockSpec((tq, HEAD_W), lambda b, h, i: (b * nq + i, h)),
        out_shape=jax.ShapeDtypeStruct(k.shape, BF16),
        scratch_shapes=[pltpu.VMEM((2 * qblocks, HEAD_W, kb), BF16), pltpu.VMEM((2 * qblocks, kb), F32),
                        pltpu.VMEM((2 * qblocks, VT_ROWS, kb), F32),
                        pltpu.VMEM((2 * qblocks, kb, kb), F32), pltpu.VMEM((2 * qblocks, kb, kb), F32)],
        compiler_params=pltpu.CompilerParams(
            dimension_semantics=("arbitrary", "arbitrary", "arbitrary"), vmem_limit_bytes=VMEM_LIMIT),
        name="attn_prompt",
    )(lam_vecs, sub_col, bias, qt, k, vt)


def _attn_sample_kernel(lam_ref, sub_ref, q_ref, kc_ref, vc_ref, kn_ref, vn_ref, o_ref, *, t):
    lam = _lam_value(lam_ref)
    for h in range(A_HEADS):
        cs = slice(h * HEAD_W, (h + 1) * HEAD_W)
        qbd = _split_maps(q_ref[:, cs])
        keys = [kc_ref[0, :, h, :].astype(BF16), kc_ref[0, :, A_HEADS + h, :].astype(BF16), kn_ref[:, cs]]
        vals = [vc_ref[0, :, h, :].astype(BF16), vc_ref[0, :, A_HEADS + h, :].astype(BF16), vn_ref[:, cs]]
        scores = [_dot_nt(qbd, k) for k in keys]
        m = functools.reduce(jnp.maximum, [jnp.max(sc, axis=-1, keepdims=True) for sc in scores])
        probs = [jnp.exp(sc - m) for sc in scores]
        l = sum(jnp.sum(p, axis=-1, keepdims=True) for p in probs)
        o_num = sum(_dot(p.astype(BF16), v) for p, v in zip(probs, vals))
        o_ref[:, cs] = _attn_finish(o_num, l, lam, sub_ref[...], t)


def _attn_sample(q, cache_k, cache_v, k_new, v_new, lam_vecs, sub_w, batch, t):
    fix = lambda b: (0, 0)
    row = lambda b: (b, 0)
    cache = pl.BlockSpec((1,) + cache_k.shape[1:], lambda b: (b, 0, 0, 0))
    return pl.pallas_call(
        functools.partial(_attn_sample_kernel, t=t),
        grid=(batch,),
        in_specs=[pl.BlockSpec((4, A_HEAD_DIM), fix), pl.BlockSpec((1, HEAD_W), fix),
                  pl.BlockSpec((t, SEG_W), row),
                  cache, cache, pl.BlockSpec((t, SEG_W), row), pl.BlockSpec((t, SEG_W), row)],
        out_specs=pl.BlockSpec((t, SEG_W), row),
        out_shape=jax.ShapeDtypeStruct(q.shape, BF16),
        compiler_params=pltpu.CompilerParams(dimension_semantics=("arbitrary",),
                                             vmem_limit_bytes=VMEM_LIMIT),
        name="attn_sample",
    )(lam_vecs, sub_w, q, cache_k, cache_v, k_new, v_new)


def _hgrn_consts(length):
    t = np.arange(length)[:, None]
    j = np.arange(length)[None, :]
    sums = [j <= t]
    masks = [j == t]
    blk = length
    while blk >= 2:
        half = blk // 2
        mid_t = (t // blk) * blk + half
        mid_j = (j // blk) * blk + half
        sums.append(np.where(t >= mid_t, (j >= mid_t) & (j <= t), (j > t) & (j < mid_t)))
        masks.append((t // blk == j // blk) & (t >= mid_t) & (j < mid_j))
        blk = half
    sums = np.tile(np.concatenate(sums, axis=0).astype(np.float32), (1, 2))
    masks = np.concatenate(masks, axis=0).astype(np.float32)
    return jnp.asarray(sums, BF16), jnp.asarray(masks, F32)


def _hgrn_kernel(*refs, length, chunks, has_state_in):
    if has_state_in:
        (sums_ref, masks_ref, gn_ref, q_ref, logf_ref, k_ref, v_ref, g_ref, s0_ref,
         r_ref, sout_ref, st_sc) = refs
    else:
        (sums_ref, masks_ref, gn_ref, q_ref, logf_ref, k_ref, v_ref, g_ref,
         r_ref, sout_ref, st_sc) = refs
    step = pl.program_id(1)
    levels = int(math.log2(length))
    L = length
    group = math.gcd(chunks, 4)

    @pl.when(step == 0)
    def _():
        for h in range(R_HEADS):
            if has_state_in:
                st_sc[h] = s0_ref[0, h].T
            else:
                st_sc[h] = jnp.zeros(st_sc.shape[1:], F32)

    heads = [slice(h * HEAD_W, (h + 1) * HEAD_W) for h in range(R_HEADS)]

    def chunk_group(c, carry):
        rows = [pl.ds(pl.multiple_of((c * group + g) * L, L), L) for g in range(group)]
        expo = []
        for g in range(group):
            logf = logf_ref[rows[g], :]
            hi = logf.astype(BF16)
            lo = (logf - hi.astype(F32)).astype(BF16)
            expo.append(_dot(sums_ref[...], jnp.concatenate([hi, lo], axis=0)))
        streams = [(g, h) for g in range(group) for h in range(R_HEADS)]
        intra, q_dec, upd = {}, {}, {}
        for g, h in streams:
            cs = heads[h]
            q = q_ref[rows[g], cs]
            k = k_ref[rows[g], cs]
            v = v_ref[rows[g], cs]
            a = masks_ref[0:L, :] * _dot_nt(q.astype(BF16), k.astype(BF16))
            for lv in range(levels):
                x = jnp.exp(expo[g][(1 + lv) * L:(2 + lv) * L, cs])
                a = a + masks_ref[(1 + lv) * L:(2 + lv) * L, :] * _dot_nt(
                    (q * x).astype(BF16), (k * x).astype(BF16))
            intra[g, h] = _dot(a.astype(BF16), v)
            b = expo[g][0:L, cs]
            q_dec[g, h] = (q * jnp.exp(b)).astype(BF16)
            k_dec = (k * jnp.exp(b[L - 1:L, :] - b)).astype(BF16)
            upd[g, h] = _dot_tn(v, k_dec)
        for g, h in streams:
            cs = heads[h]
            st = st_sc[h]
            o = intra[g, h] + _dot_nt(q_dec[g, h], st.astype(BF16))
            st_sc[h] = st * jnp.exp(expo[g][L - 1:L, cs]) + upd[g, h]
            gate = g_ref[rows[g], cs]
            r_ref[rows[g], cs] = (_rms(o, gn_ref[...]) * (gate * _sigmoid(gate))).astype(BF16)
        return carry

    lax.fori_loop(0, chunks // group, chunk_group, 0)

    @pl.when(step == pl.num_programs(1) - 1)
    def _():
        for h in range(R_HEADS):
            sout_ref[0, h] = st_sc[h].T


def _hgrn(q, logf, k, v, g, gn_w, state_in, batch, seq, length, chunks):
    sums, masks = _hgrn_consts(length)
    tm = length * chunks
    steps = seq // tm
    fix = lambda b, s: (0, 0)
    row = lambda b, s: (b * steps + s, 0)
    tok = pl.BlockSpec((tm, SEG_W), row)
    in_specs = [pl.BlockSpec(sums.shape, fix), pl.BlockSpec(masks.shape, fix), pl.BlockSpec((1, HEAD_W), fix),
                tok, tok, tok, tok, tok]
    args = [sums, masks, gn_w, q, logf, k, v, g]
    state_spec = pl.BlockSpec((1, R_HEADS, HEAD_W, HEAD_W), lambda b, s: (b, 0, 0, 0))
    if state_in is not None:
        in_specs.append(state_spec)
        args.append(state_in)
    return pl.pallas_call(
        functools.partial(_hgrn_kernel, length=length, chunks=chunks, has_state_in=state_in is not None),
        grid=(batch, steps),
        in_specs=in_specs,
        out_specs=[tok, state_spec],
        out_shape=[jax.ShapeDtypeStruct(q.shape, BF16),
                   jax.ShapeDtypeStruct((batch, R_HEADS, HEAD_W, HEAD_W), F32)],
        scratch_shapes=[pltpu.VMEM((R_HEADS, HEAD_W, HEAD_W), F32)],
        compiler_params=pltpu.CompilerParams(dimension_semantics=("arbitrary", "arbitrary"),
                                             vmem_limit_bytes=VMEM_LIMIT),
        name="hgrn",
    )(*args)


def _merge_kernel(x_ref, a_ref, r_ref, wo_ref, n2_ref, wr_hi_ref, wr_lo_ref, tri_ref, cnt0_ref,
                  ymid_ref, h2_ref, route_ref, cnt_ref, cnt_sc):
    step = pl.program_id(0)

    @pl.when(step == 0)
    def _():
        cnt_sc[...] = cnt0_ref[...]

    sub = tri_ref.shape[0]
    for s in range(x_ref.shape[0] // sub):
        _merge_rows(slice(s * sub, (s + 1) * sub), x_ref, a_ref, r_ref, wo_ref, n2_ref, wr_hi_ref, wr_lo_ref,
                    tri_ref, ymid_ref, h2_ref, route_ref, cnt_sc)
    cnt_ref[...] = cnt_sc[...]


def _merge_rows(rows, x_ref, a_ref, r_ref, wo_ref, n2_ref, wr_hi_ref, wr_lo_ref, tri_ref,
                ymid_ref, h2_ref, route_ref, cnt_sc):
    half = a_ref.shape[1]
    y = x_ref[rows, :] + _dot(a_ref[rows, :], wo_ref[0:half, :]) + _dot(r_ref[rows, :], wo_ref[half:, :])
    ymid_ref[rows, :] = y
    hn = _rms(y, n2_ref[...])
    h2_ref[rows, :] = hn
    hi = hn.astype(BF16)
    lo = (hn - hi.astype(F32)).astype(BF16)
    logits = _dot(hi, wr_hi_ref[...]) + _dot(lo, wr_hi_ref[...]) + _dot(hi, wr_lo_ref[...])
    lane = lax.broadcasted_iota(jnp.int32, logits.shape, 1)
    neg = -jnp.inf
    big = jnp.int32(ROUTE_W)

    def top1(mask):
        val = jnp.max(jnp.where(mask, logits, neg), axis=-1, keepdims=True)
        idx = jnp.min(jnp.where(mask & (logits == val), lane, big), axis=-1, keepdims=True)
        return val, idx

    is_group = lane < N_GROUPS
    g_max, g_sel = top1(is_group)
    p_group = 1.0 / jnp.sum(jnp.where(is_group, jnp.exp(logits - g_max), 0.0), axis=-1, keepdims=True)
    first = N_GROUPS + g_sel * EXPERTS_PER_GROUP
    in_group = (lane >= first) & (lane < first + EXPERTS_PER_GROUP)
    v1, i1 = top1(in_group)
    v2, i2 = top1(in_group & (lane != i1))
    e2 = jnp.exp(v2 - v1)
    gate1 = p_group * (1.0 / (1.0 + e2))
    gate2 = p_group * (e2 / (1.0 + e2))
    hot1 = lane == i1
    hot2 = lane == i2
    hot = jnp.where(hot1 | hot2, 1.0, 0.0)
    before = cnt_sc[...] + _dot(tri_ref[...], hot.astype(BF16))
    rank1 = jnp.sum(jnp.where(hot1, before, 0.0), axis=-1, keepdims=True)
    rank2 = jnp.sum(jnp.where(hot2, before, 0.0), axis=-1, keepdims=True)
    cnt_sc[...] = cnt_sc[...] + jnp.sum(hot, axis=0, keepdims=True)
    col = lax.broadcasted_iota(jnp.int32, (hot.shape[0], route_ref.shape[1]), 1)
    cols = [(i1 - N_GROUPS).astype(F32), (i2 - N_GROUPS).astype(F32), gate1, gate2, rank1, rank2]
    route = jnp.zeros(col.shape, F32)
    for c, val in enumerate(cols):
        route = jnp.where(col == c, val, route)
    route_ref[rows, :] = route


def _merge(x, a, r, wo_bf, n2, wr_hi, wr_lo, counts_in, tm):
    n = x.shape[0]
    row = lambda i: (i, 0)
    fix = lambda i: (0, 0)
    sub = min(256, tm)
    tri = jnp.asarray(np.tril(np.ones((sub, sub), np.float32), -1), BF16)
    return pl.pallas_call(
        _merge_kernel,
        grid=(n // tm,),
        in_specs=[pl.BlockSpec((tm, D_MODEL), row), pl.BlockSpec((tm, SEG_W), row),
                  pl.BlockSpec((tm, SEG_W), row), pl.BlockSpec((D_MODEL, D_MODEL), fix),
                  pl.BlockSpec((1, D_MODEL), fix), pl.BlockSpec((D_MODEL, ROUTE_W), fix),
                  pl.BlockSpec((D_MODEL, ROUTE_W), fix), pl.BlockSpec((sub, sub), fix),
                  pl.BlockSpec((1, ROUTE_W), fix)],
        out_specs=[pl.BlockSpec((tm, D_MODEL), row), pl.BlockSpec((tm, D_MODEL), row),
                   pl.BlockSpec((tm, 8), row), pl.BlockSpec((1, ROUTE_W), fix)],
        out_shape=[jax.ShapeDtypeStruct((n, D_MODEL), F32), jax.ShapeDtypeStruct((n, D_MODEL), F32),
                   jax.ShapeDtypeStruct((n, 8), F32), jax.ShapeDtypeStruct((1, ROUTE_W), F32)],
        scratch_shapes=[pltpu.VMEM((1, ROUTE_W), F32)],
        compiler_params=pltpu.CompilerParams(dimension_semantics=("arbitrary",),
                                             vmem_limit_bytes=VMEM_LIMIT),
        name="merge",
    )(x, a, r, wo_bf, n2, wr_hi, wr_lo, tri, counts_in)


def _dispatch_kernel(pend_ref, pcnt_ref, n_used_ref, dest_hbm, hp_ref, hs_ref, xs_hbm,
                     idx_sm, zero_sc, sem_idx, sem_out, *, tm, steps_p, steps_s, first_spare, n_blocks):
    i = pl.program_id(0)
    slot = i % 2
    n_steps = steps_p + steps_s

    def idx_copy(step, s):
        return pltpu.make_async_copy(dest_hbm.at[step], idx_sm.at[s], sem_idx.at[s])

    @pl.when(i == 0)
    def _():
        idx_copy(0, 0).start()
        zero_sc[...] = jnp.zeros(zero_sc.shape, F32)

        def fill(start):
            start = pl.multiple_of(start, EXPERT_ROWS)
            return pltpu.make_async_copy(zero_sc, xs_hbm.at[pl.ds(start, EXPERT_ROWS)], sem_out)

        for action in ("start", "wait"):
            for e in range(N_EXPERTS):
                pl.when(pcnt_ref[e] > 0)(
                    lambda e=e: getattr(fill(pend_ref[e] - EXPERT_ROWS), action)())
            for blk in range(first_spare, n_blocks):
                pl.when(blk >= n_used_ref[0])(
                    lambda blk=blk: getattr(fill(blk * EXPERT_ROWS), action)())

    idx_copy(i, slot).wait()

    @pl.when(i + 1 < n_steps)
    def _():
        idx_copy(i + 1, 1 - slot).start()

    def scatter(h_ref):
        for r in range(2 * tm):
            pltpu.make_async_copy(h_ref.at[pl.ds(r % tm, 1)], xs_hbm.at[pl.ds(idx_sm[slot, r], 1)],
                                  sem_out).start(priority=r % 2)
        for _ in range(2):
            pltpu.make_async_copy(h_ref, xs_hbm.at[pl.ds(0, tm)], sem_out).wait()

    pl.when(i < steps_p)(lambda: scatter(hp_ref))
    pl.when(i >= steps_p)(lambda: scatter(hs_ref))


def _dispatch(h2_p, h2_s, dest_tbl, pends, pcounts, n_used, n_rows, tm):
    steps_p = h2_p.shape[0] // tm
    steps_s = h2_s.shape[0] // tm
    n_blocks = n_rows // EXPERT_ROWS
    first_spare = (h2_p.shape[0] + h2_s.shape[0]) * TOP_K // EXPERT_ROWS
    grid_spec = pltpu.PrefetchScalarGridSpec(
        num_scalar_prefetch=3, grid=(steps_p + steps_s,),
        in_specs=[pl.BlockSpec(memory_space=pl.ANY),
                  pl.BlockSpec((tm, D_MODEL), lambda i, pe, pc, nu: (jnp.minimum(i, steps_p - 1), 0)),
                  pl.BlockSpec((tm, D_MODEL), lambda i, pe, pc, nu: (jnp.maximum(i - steps_p, 0), 0))],
        out_specs=pl.BlockSpec(memory_space=pl.ANY),
        scratch_shapes=[pltpu.SMEM((2, 2 * tm), jnp.int32), pltpu.VMEM((EXPERT_ROWS, D_MODEL), F32),
                        pltpu.SemaphoreType.DMA((2,)), pltpu.SemaphoreType.DMA])
    return pl.pallas_call(
        functools.partial(_dispatch_kernel, tm=tm, steps_p=steps_p, steps_s=steps_s,
                          first_spare=first_spare, n_blocks=n_blocks),
        grid_spec=grid_spec,
        out_shape=jax.ShapeDtypeStruct((n_rows, D_MODEL), F32),
        compiler_params=pltpu.CompilerParams(dimension_semantics=("arbitrary",),
                                             vmem_limit_bytes=VMEM_LIMIT),
        name="dispatch",
    )(pends, pcounts, n_used, dest_tbl, h2_p, h2_s)


def _experts_kernel(blk_exp_ref, n_used_ref, x_ref, wg_ref, wu_ref, wd_ref, y_ref, wg_sc, wu_sc, wd_sc):
    i = pl.program_id(0)

    @pl.when(i < n_used_ref[0])
    def _():
        prev = blk_exp_ref[jnp.maximum(i - 1, 0)]

        @pl.when((i == 0) | (prev != blk_exp_ref[i]))
        def _():
            wg_sc[...] = wg_ref[0].astype(BF16)
            wu_sc[...] = wu_ref[0].astype(BF16)
            wd_sc[...] = wd_ref[0].astype(BF16)

        xb = x_ref[...].astype(BF16)
        gate = _dot(xb, wg_sc[...])
        up = _dot(xb, wu_sc[...])
        act = (gate * _sigmoid(gate) * up).astype(BF16)
        y_ref[...] = _dot(act, wd_sc[...])

    @pl.when(i >= n_used_ref[0])
    def _():
        y_ref[...] = jnp.zeros(y_ref.shape, F32)


def _experts(xs, blk_exp, n_used, wg, wu, wd):
    n_blocks = int(blk_exp.shape[0])
    wmap = lambda i, be, nu: (be[i], 0, 0)
    xmap = lambda i, be, nu: (jnp.minimum(i, nu[0] - 1), 0)
    grid_spec = pltpu.PrefetchScalarGridSpec(
        num_scalar_prefetch=2,
        grid=(n_blocks,),
        in_specs=[pl.BlockSpec((EXPERT_ROWS, D_MODEL), xmap),
                  pl.BlockSpec((1, D_MODEL, D_EXPERT), wmap),
                  pl.BlockSpec((1, D_MODEL, D_EXPERT), wmap),
                  pl.BlockSpec((1, D_EXPERT, D_MODEL), wmap)],
        out_specs=pl.BlockSpec((EXPERT_ROWS, D_MODEL), lambda i, be, nu: (i, 0)),
        scratch_shapes=[pltpu.VMEM((D_MODEL, D_EXPERT), BF16), pltpu.VMEM((D_MODEL, D_EXPERT), BF16),
                        pltpu.VMEM((D_EXPERT, D_MODEL), BF16)],
    )
    return pl.pallas_call(
        _experts_kernel,
        grid_spec=grid_spec,
        out_shape=jax.ShapeDtypeStruct(xs.shape, F32),
        compiler_params=pltpu.CompilerParams(dimension_semantics=("arbitrary",),
                                             vmem_limit_bytes=VMEM_LIMIT),
        name="experts",
    )(blk_exp, n_used, xs, wg, wu, wd)


def _expert_layout(counts_row, n_tok):
    counts = counts_row[0, N_GROUPS:N_GROUPS + N_EXPERTS].astype(jnp.int32)
    pcounts = (counts + EXPERT_ROWS - 1) // EXPERT_ROWS * EXPERT_ROWS
    pends = jnp.cumsum(pcounts).astype(jnp.int32)
    pstarts = pends - pcounts
    n_blocks = -(-(n_tok * TOP_K) // EXPERT_ROWS) + N_EXPERTS
    blk_start = jnp.arange(n_blocks, dtype=jnp.int32) * EXPERT_ROWS
    blk_exp = jnp.minimum(jnp.sum(blk_start[:, None] >= pends[None, :], axis=1), N_EXPERTS - 1).astype(jnp.int32)
    n_used = (pends[-1:] // EXPERT_ROWS).astype(jnp.int32)
    return pstarts, pends, pcounts, blk_exp, n_used, n_blocks * EXPERT_ROWS


def _dest_table(route, pstarts, tm):
    experts = route[:, 0:TOP_K].astype(jnp.int32)
    ranks = route[:, 2 * TOP_K:3 * TOP_K].astype(jnp.int32)
    dest = pstarts[experts] + ranks
    return dest.reshape(-1, tm, TOP_K).swapaxes(1, 2).reshape(-1, TOP_K * tm)


def _final_kernel(dest_hbm, ys_hbm, y_ref, route_ref, w_ref, o_ref, idx_sm, gbuf, sem_idx, sem_g,
                  *, tm, n_steps):
    i = pl.program_id(0)

    def idx_copy(step):
        return pltpu.make_async_copy(dest_hbm.at[step], idx_sm.at[step % 3], sem_idx.at[step % 3])

    def gather(step):
        for r in range(2 * tm):
            pltpu.make_async_copy(ys_hbm.at[pl.ds(idx_sm[step % 3, r], 1)],
                                  gbuf.at[step % 2, pl.ds(r, 1)], sem_g.at[step % 2]).start(priority=r % 2)

    @pl.when(i == 0)
    def _():
        idx_copy(0).start()
        idx_copy(0).wait()
        gather(0)
        if n_steps > 1:
            idx_copy(1).start()

    if n_steps > 1:
        @pl.when(i + 1 < n_steps)
        def _():
            idx_copy(i + 1).wait()
            gather(i + 1)

        if n_steps > 2:
            @pl.when(i + 2 < n_steps)
            def _():
                idx_copy(i + 2).start()

    pltpu.make_async_copy(ys_hbm.at[pl.ds(0, 2 * tm)], gbuf.at[i % 2], sem_g.at[i % 2]).wait()
    route = route_ref[...]
    rows = gbuf[i % 2]
    y = y_ref[...] + route[:, 2:3] * rows[0:tm] + route[:, 3:4] * rows[tm:]
    o_ref[...] = _rms(y, w_ref[...])


def _final(ymid, route, dest_tbl, ys, w, tm):
    n = ymid.shape[0]
    n_steps = n // tm
    return pl.pallas_call(
        functools.partial(_final_kernel, tm=tm, n_steps=n_steps),
        grid=(n_steps,),
        in_specs=[pl.BlockSpec(memory_space=pl.ANY), pl.BlockSpec(memory_space=pl.ANY),
                  pl.BlockSpec((tm, D_MODEL), lambda i: (i, 0)), pl.BlockSpec((tm, 8), lambda i: (i, 0)),
                  pl.BlockSpec((1, D_MODEL), lambda i: (0, 0))],
        out_specs=pl.BlockSpec((tm, D_MODEL), lambda i: (i, 0)),
        out_shape=jax.ShapeDtypeStruct((n, D_MODEL), F32),
        scratch_shapes=[pltpu.SMEM((3, 2 * tm), jnp.int32), pltpu.VMEM((2, 2 * tm, D_MODEL), F32),
                        pltpu.SemaphoreType.DMA((3,)), pltpu.SemaphoreType.DMA((2,))],
        compiler_params=pltpu.CompilerParams(dimension_semantics=("arbitrary",),
                                             vmem_limit_bytes=VMEM_LIMIT),
        name="final",
    )(dest_tbl, ys, ymid, route, w)


def kernel(x_prompt, x_sample, cache_k, cache_v, state_rec, w_in, lam_q1, lam_k1, lam_q2, lam_k2,
           subln_w, lb_param, gnorm_w, w_out, norm1_w, norm2_w, w_group, w_router,
           w_e_gate, w_e_up, w_e_down, final_w):
    assert w_in.shape[0] == 1 and lb_param.shape[0] == 2, "single-layer model"
    bp, sp, _ = x_prompt.shape
    bs, ts, _ = x_sample.shape
    past = cache_k.shape[2]
    assert sp % CHUNK == 0 and past % CHUNK == 0 and ts <= CHUNK and ts & (ts - 1) == 0
    n_p, n_s = bp * sp, bs * ts
    n_all = n_p + n_s
    tm = 512 if (n_p % 512 == 0 and n_s % 512 == 0) else math.gcd(n_p, n_s)

    w_in_bf = w_in[0].astype(BF16)
    w_out_bf = w_out[0].astype(BF16)
    n1 = norm1_w[0].reshape(1, D_MODEL)
    n2 = norm2_w[0].reshape(1, D_MODEL)
    lam_vecs = jnp.stack([lam_q1[0], lam_k1[0], lam_q2[0], lam_k2[0]]).astype(F32)
    sub_w = subln_w[0].reshape(1, HEAD_W)
    gn_w = gnorm_w[0].reshape(1, HEAD_W)
    w_route = jnp.zeros((D_MODEL, ROUTE_W), F32)
    w_route = w_route.at[:, 0:N_GROUPS].set(w_group[0]).at[:, N_GROUPS:N_GROUPS + N_EXPERTS].set(w_router[0])
    wr_hi = w_route.astype(BF16)
    wr_lo = (w_route - wr_hi.astype(F32)).astype(BF16)

    xp = x_prompt.reshape(n_p, D_MODEL)
    xs = x_sample.reshape(n_s, D_MODEL)

    kf, vf, kb, qr, logf, kr, ir, gr, qt, vt = _inproj(xp, n1, w_in_bf, lb_param, tm, True)
    a_p = _attn_prompt(qt, kb, vt, lam_vecs, sub_w.reshape(HEAD_W, 1), bp, sp)
    chunks = 8 if sp % (8 * CHUNK) == 0 else 1
    r_p, state_p = _hgrn(qr, logf, kr, ir, gr, gn_w, None, bp, sp, CHUNK, chunks)
    ymid_p, h2_p, route_p, counts_p = _merge(xp, a_p, r_p, w_out_bf, n2, wr_hi, wr_lo,
                                             jnp.zeros((1, ROUTE_W), F32), tm)
    k_prompt = kf.reshape(1, bp, sp, A_HEADS, HEAD_W)
    v_prompt = vf.reshape(1, bp, sp, A_HEADS, HEAD_W)

    kf, vf, kb, qr, logf, kr, ir, gr, q, vb = _inproj(xs, n1, w_in_bf, lb_param, tm, False)
    pair_rows = (bs, past // 2, 2 * A_HEADS, HEAD_W)
    a_s = _attn_sample(q, cache_k.reshape(pair_rows), cache_v.reshape(pair_rows), kb, vb,
                       lam_vecs, sub_w, bs, ts)
    r_s, state_s = _hgrn(qr, logf, kr, ir, gr, gn_w, state_rec[0], bs, ts, ts, 1)
    ymid_s, h2_s, route_s, counts = _merge(xs, a_s, r_s, w_out_bf, n2, wr_hi, wr_lo, counts_p, tm)
    k_sample = kf.reshape(1, bs, ts, A_HEADS, HEAD_W)
    v_sample = vf.reshape(1, bs, ts, A_HEADS, HEAD_W)

    pstarts, pends, pcounts, blk_exp, n_used, n_rows = _expert_layout(counts, n_all)
    td = min(256, math.gcd(n_p, n_s))
    dest_p = _dest_table(route_p, pstarts, td)
    dest_s = _dest_table(route_s, pstarts, td)
    xs_rows = _dispatch(h2_p, h2_s, jnp.concatenate([dest_p, dest_s], axis=0), pends, pcounts, n_used,
                        n_rows, td)
    ys_rows = _experts(xs_rows, blk_exp, n_used, w_e_gate[0], w_e_up[0], w_e_down[0])

    fw = final_w.reshape(1, D_MODEL)
    y_prompt = _final(ymid_p, route_p, dest_p, ys_rows, fw, td).reshape(bp, sp, D_MODEL)
    y_sample = _final(ymid_s, route_s, dest_s, ys_rows, fw, td).reshape(bs, ts, D_MODEL)
    return (y_prompt, y_sample, k_prompt, v_prompt, state_p[None], k_sample, v_sample, state_s[None])
```

```python
import functools
import math

import numpy as np
import jax
import jax.numpy as jnp
from jax import lax
from jax.experimental import pallas as pl
from jax.experimental.pallas import tpu as pltpu

F32 = jnp.float32
BF16 = jnp.bfloat16

D_MODEL = 1024
RMS_EPS = 1e-6
CHUNK = 64
A_HEADS = 4
A_HEAD_DIM = 64
HEAD_W = 2 * A_HEAD_DIM
KEY_BLOCK = 256
VT_ROWS = HEAD_W + 16
LOG2_E = math.log2(math.e)
R_HEADS = 4
SEG_W = 512
N_SEG = 7
N_GROUPS = 4
EXPERTS_PER_GROUP = 8
N_EXPERTS = N_GROUPS * EXPERTS_PER_GROUP
TOP_K = 2
D_EXPERT = 512
EXPERT_ROWS = 256
ROUTE_W = 128
LAM_INIT = 0.8 - 0.6 * math.exp(-0.3 * 0)
VMEM_LIMIT = 56 * 1024 * 1024


def _sigmoid(x):
    return 1.0 / (1.0 + jnp.exp(-x))


def _dot(a, b):
    return jnp.dot(a, b, preferred_element_type=F32)


def _dot_nt(a, b):
    return lax.dot_general(a, b, (((1,), (1,)), ((), ())), preferred_element_type=F32)


def _dot_tn(a, b):
    return lax.dot_general(a, b, (((0,), (0,)), ((), ())), preferred_element_type=F32)


def _rms(x, w):
    return x * lax.rsqrt(jnp.mean(x * x, axis=-1, keepdims=True) + RMS_EPS) * w


def _inproj_kernel(x_ref, n1_ref, w_ref, lbp_ref, kf_ref, vf_ref, kb_ref, qr_ref, logf_ref, kr_ref,
                   ir_ref, gr_ref, qa_ref, va_ref, *, transposed_qv):
    h = _rms(x_ref[...], n1_ref[...]).astype(BF16)

    def seg(i):
        return _dot(h, w_ref[:, i * SEG_W:(i + 1) * SEG_W])

    def transposed(val, s):
        return val[s * KEY_BLOCK:(s + 1) * KEY_BLOCK, :].T.astype(BF16)

    q = seg(0) * (A_HEAD_DIM ** -0.5)
    k = seg(1)
    for hd in range(A_HEADS):
        kf_ref[:, hd, :] = k[:, hd * HEAD_W:(hd + 1) * HEAD_W]
    kb_ref[...] = k.astype(BF16)
    v = seg(2)
    for hd in range(A_HEADS):
        vf_ref[:, hd, :] = v[:, hd * HEAD_W:(hd + 1) * HEAD_W]
    if transposed_qv:
        ones = jnp.ones((VT_ROWS - HEAD_W, KEY_BLOCK), BF16)
        for s in range(qa_ref.shape[0]):
            qa_ref[s] = transposed(q * LOG2_E, s)
            vt = transposed(v, s)
            for hd in range(A_HEADS):
                va_ref[s, hd * VT_ROWS:hd * VT_ROWS + HEAD_W, :] = vt[hd * HEAD_W:(hd + 1) * HEAD_W, :]
                va_ref[s, hd * VT_ROWS + HEAD_W:(hd + 1) * VT_ROWS, :] = ones
    else:
        qa_ref[...] = q.astype(BF16)
        va_ref[...] = v.astype(BF16)
    qr = seg(3)
    qr_ref[...] = (qr * _sigmoid(qr)).astype(BF16)
    p = lbp_ref[...]
    e = jnp.exp(p - jnp.max(p, axis=0, keepdims=True))
    lb = e[0:1] / jnp.sum(e, axis=0, keepdims=True)
    f = lb + (1.0 - lb) * _sigmoid(seg(4))
    logf_ref[...] = jnp.log(f) * LOG2_E
    kr_ref[...] = (1.0 - f).astype(BF16)
    ir_ref[...] = seg(5).astype(BF16)
    gr_ref[...] = seg(6)


def _inproj(x, n1, w_bf, lbp, tm, transposed_qv):
    n = x.shape[0]
    row = lambda i: (i, 0)
    fix = lambda i: (0, 0)
    out = lambda dt: jax.ShapeDtypeStruct((n, SEG_W), dt)
    ospec = pl.BlockSpec((tm, SEG_W), row)
    hspec = pl.BlockSpec((tm, A_HEADS, HEAD_W), lambda i: (i, 0, 0))
    hout = jax.ShapeDtypeStruct((n, A_HEADS, HEAD_W), F32)
    slabs = tm // KEY_BLOCK
    tspec = lambda rows: pl.BlockSpec((slabs, rows, KEY_BLOCK), lambda i: (i, 0, 0))
    tout = lambda rows: jax.ShapeDtypeStruct((n // KEY_BLOCK, rows, KEY_BLOCK), BF16)
    if transposed_qv:
        qv_specs = [tspec(SEG_W), tspec(A_HEADS * VT_ROWS)]
        qv_shapes = [tout(SEG_W), tout(A_HEADS * VT_ROWS)]
    else:
        qv_specs = [ospec, ospec]
        qv_shapes = [out(BF16), out(BF16)]
    return pl.pallas_call(
        functools.partial(_inproj_kernel, transposed_qv=transposed_qv),
        grid=(n // tm,),
        in_specs=[pl.BlockSpec((tm, D_MODEL), row), pl.BlockSpec((1, D_MODEL), fix),
                  pl.BlockSpec((D_MODEL, N_SEG * SEG_W), fix), pl.BlockSpec(lbp.shape, fix)],
        out_specs=[hspec, hspec] + [ospec] * 6 + qv_specs,
        out_shape=[hout, hout, out(BF16), out(BF16), out(F32), out(BF16), out(BF16), out(F32)] + qv_shapes,
        compiler_params=pltpu.CompilerParams(dimension_semantics=("arbitrary",),
                                             vmem_limit_bytes=VMEM_LIMIT),
        name="inproj",
    )(x, n1, w_bf, lbp)


def _lam_value(lam_ref):
    l = lam_ref[...]
    s1 = jnp.sum(l[0:1] * l[1:2], axis=-1, keepdims=True)
    s2 = jnp.sum(l[2:3] * l[3:4], axis=-1, keepdims=True)
    return jnp.exp(s1) - jnp.exp(s2) + LAM_INIT


def _split_maps(q):
    lane = lax.broadcasted_iota(jnp.int32, q.shape, 1)
    zero = jnp.zeros_like(q)
    return jnp.concatenate([jnp.where(lane < A_HEAD_DIM, q, zero),
                            jnp.where(lane >= A_HEAD_DIM, q, zero)], axis=0)


def _attn_finish(o_num, l, lam, sub_w, tq):
    o = o_num[0:tq] / l[0:tq] - lam * (o_num[tq:] / l[tq:])
    return (_rms(o, sub_w) * (1.0 - LAM_INIT)).astype(BF16)


def _attn_prompt_kernel(lam_ref, sub_ref, bias_ref, qt_ref, k_ref, vt_ref, o_ref,
                        qbd_sc, m_sc, acc_sc, sa_sc, sb_sc, *, qblocks):
    i = pl.program_id(2)
    kb = KEY_BLOCK

    sub = lax.broadcasted_iota(jnp.int32, (HEAD_W, kb), 0)
    for r in range(2 * qblocks):
        qt = qt_ref[r % qblocks]
        keep = (sub < A_HEAD_DIM) if r < qblocks else (sub >= A_HEAD_DIM)
        qbd_sc[r] = jnp.where(keep, qt, jnp.zeros_like(qt))
    m_sc[...] = jnp.full(m_sc.shape, -jnp.inf, F32)
    acc_sc[...] = jnp.zeros(acc_sc.shape, F32)

    def scores(r, key_block):
        return _dot(k_ref[pl.ds(pl.multiple_of(key_block * kb, kb), kb), :], qbd_sc[r])

    def update(r, key_block, st):
        m_old = m_sc[r:r + 1, :]
        m_new = jnp.maximum(m_old, jnp.max(st, axis=0, keepdims=True))
        p = jnp.exp2(st - m_new).astype(BF16)
        acc_sc[r] = jnp.exp2(m_old - m_new) * acc_sc[r] + _dot(vt_ref[key_block], p)
        m_sc[r:r + 1, :] = m_new

    n_q = 2 * qblocks
    first = i * qblocks
    for r in range(n_q):
        sa_sc[r] = scores(r, 0)

    def visible_blocks(start, count):
        lead = 2
        chains = [(j, r) for j in range(count) for r in range(n_q)]
        for c in range(len(chains) + lead):
            if c < len(chains):
                j, r = chains[c]
                (sb_sc if j % 2 == 0 else sa_sc)[r] = scores(r, start + j + 1)
            if c >= lead:
                j, r = chains[c - lead]
                update(r, start + j, (sa_sc if j % 2 == 0 else sb_sc)[r])

    def tile_pair(t, carry):
        visible_blocks(2 * t * qblocks, 2 * qblocks)
        return carry

    lax.fori_loop(0, i // 2, tile_pair, 0)
    pl.when(i % 2 == 1)(lambda: visible_blocks((i - 1) * qblocks, qblocks))

    work = [(kl, r) for kl in range(qblocks) for r in range(n_q) if r % qblocks >= kl]
    ahead = 4
    pending = {}
    for n, (kl, r) in enumerate(work):
        for kl2, r2 in work[n:n + 1 + ahead]:
            if kl2 > 0 and (kl2, r2) not in pending:
                pending[(kl2, r2)] = scores(r2, first + kl2)
        st = sa_sc[r] if kl == 0 else pending.pop((kl, r))
        update(r, first + kl, st + bias_ref[...] if r % qblocks == kl else st)

    lam = _lam_value(lam_ref)
    for ql in range(qblocks):
        a1 = acc_sc[ql]
        a2 = acc_sc[qblocks + ql]
        ot = a1[0:HEAD_W] / a1[HEAD_W:HEAD_W + 1] - lam * (a2[0:HEAD_W] / a2[HEAD_W:HEAD_W + 1])
        ot = ot * lax.rsqrt(jnp.mean(ot * ot, axis=0, keepdims=True) + RMS_EPS) * sub_ref[...]
        o_ref[ql * kb:(ql + 1) * kb, :] = (ot * (1.0 - LAM_INIT)).T.astype(BF16)


def _attn_prompt(qt, k, vt, lam_vecs, sub_col, batch, seq):
    kb = KEY_BLOCK
    tq = min(1024, seq)
    qblocks = tq // kb
    assert qblocks % 2 == 0, "the key-block loop is unrolled by two"
    nq = seq // tq
    pos = np.arange(kb) // CHUNK
    bias = jnp.asarray(np.where(pos[:, None] <= pos[None, :], 0.0, -np.inf), F32)
    fix = lambda b, h, i: (0, 0)
    return pl.pallas_call(
        functools.partial(_attn_prompt_kernel, qblocks=qblocks),
        grid=(batch, A_HEADS, nq),
        in_specs=[pl.BlockSpec((4, A_HEAD_DIM), fix), pl.BlockSpec((HEAD_W, 1), fix),
                  pl.BlockSpec((kb, kb), fix),
                  pl.BlockSpec((qblocks, HEAD_W, kb), lambda b, h, i: (b * nq + i, h, 0)),
                  pl.BlockSpec((seq, HEAD_W), lambda b, h, i: (b, h)),
                  pl.BlockSpec((seq // kb, VT_ROWS, kb), lambda b, h, i: (b, h, 0))],
        out_specs=pl.BlockSpec((tq, HEAD_W), lambda b, h, i: (b * nq + i, h)),
        out_shape=jax.ShapeDtypeStruct(k.shape, BF16),
        scratch_shapes=[pltpu.VMEM((2 * qblocks, HEAD_W, kb), BF16), pltpu.VMEM((2 * qblocks, kb), F32),
                        pltpu.VMEM((2 * qblocks, VT_ROWS, kb), F32),
                        pltpu.VMEM((2 * qblocks, kb, kb), F32), pltpu.VMEM((2 * qblocks, kb, kb), F32)],
        compiler_params=pltpu.CompilerParams(
            dimension_semantics=("arbitrary", "arbitrary", "arbitrary"), vmem_limit_bytes=VMEM_LIMIT),
        name="attn_prompt",
    )(lam_vecs, sub_col, bias, qt, k, vt)


def _attn_sample_kernel(lam_ref, sub_ref, q_ref, kc_ref, vc_ref, kn_ref, vn_ref, o_ref, *, t):
    lam = _lam_value(lam_ref)
    for h in range(A_HEADS):
        cs = slice(h * HEAD_W, (h + 1) * HEAD_W)
        qbd = _split_maps(q_ref[:, cs])
        keys = [kc_ref[0, :, h, :].astype(BF16), kc_ref[0, :, A_HEADS + h, :].astype(BF16), kn_ref[:, cs]]
        vals = [vc_ref[0, :, h, :].astype(BF16), vc_ref[0, :, A_HEADS + h, :].astype(BF16), vn_ref[:, cs]]
        scores = [_dot_nt(qbd, k) for k in keys]
        m = functools.reduce(jnp.maximum, [jnp.max(sc, axis=-1, keepdims=True) for sc in scores])
        probs = [jnp.exp(sc - m) for sc in scores]
        l = sum(jnp.sum(p, axis=-1, keepdims=True) for p in probs)
        o_num = sum(_dot(p.astype(BF16), v) for p, v in zip(probs, vals))
        o_ref[:, cs] = _attn_finish(o_num, l, lam, sub_ref[...], t)


def _attn_sample(q, cache_k, cache_v, k_new, v_new, lam_vecs, sub_w, batch, t):
    fix = lambda b: (0, 0)
    row = lambda b: (b, 0)
    cache = pl.BlockSpec((1,) + cache_k.shape[1:], lambda b: (b, 0, 0, 0))
    return pl.pallas_call(
        functools.partial(_attn_sample_kernel, t=t),
        grid=(batch,),
        in_specs=[pl.BlockSpec((4, A_HEAD_DIM), fix), pl.BlockSpec((1, HEAD_W), fix),
                  pl.BlockSpec((t, SEG_W), row),
                  cache, cache, pl.BlockSpec((t, SEG_W), row), pl.BlockSpec((t, SEG_W), row)],
        out_specs=pl.BlockSpec((t, SEG_W), row),
        out_shape=jax.ShapeDtypeStruct(q.shape, BF16),
        compiler_params=pltpu.CompilerParams(dimension_semantics=("arbitrary",),
                                             vmem_limit_bytes=VMEM_LIMIT),
        name="attn_sample",
    )(lam_vecs, sub_w, q, cache_k, cache_v, k_new, v_new)


def _hgrn_consts(length):
    t = np.arange(length)[:, None]
    j = np.arange(length)[None, :]
    sums = [j <= t]
    masks = [j == t]
    blk = length
    while blk >= 2:
        half = blk // 2
        mid_t = (t // blk) * blk + half
        mid_j = (j // blk) * blk + half
        sums.append(np.where(t >= mid_t, (j >= mid_t) & (j <= t), (j > t) & (j < mid_t)))
        masks.append((t // blk == j // blk) & (t >= mid_t) & (j < mid_j))
        blk = half
    sums = np.tile(np.concatenate(sums, axis=0).astype(np.float32), (1, 2))
    masks = np.concatenate(masks, axis=0).astype(np.float32)
    return jnp.asarray(sums, BF16), jnp.asarray(masks, F32)


def _hgrn_kernel(*refs, length, chunks, has_state_in):
    if has_state_in:
        (sums_ref, masks_ref, gn_ref, q_ref, logf_ref, k_ref, v_ref, g_ref, s0_ref,
         r_ref, sout_ref, st_sc) = refs
    else:
        (sums_ref, masks_ref, gn_ref, q_ref, logf_ref, k_ref, v_ref, g_ref,
         r_ref, sout_ref, st_sc) = refs
    step = pl.program_id(1)
    levels = int(math.log2(length))
    L = length
    group = math.gcd(chunks, 8)

    @pl.when(step == 0)
    def _():
        for h in range(R_HEADS):
            if has_state_in:
                st_sc[h] = s0_ref[0, h].T
            else:
                st_sc[h] = jnp.zeros(st_sc.shape[1:], F32)

    heads = [slice(h * HEAD_W, (h + 1) * HEAD_W) for h in range(R_HEADS)]

    def chunk_group(c, carry):
        rows = [pl.ds(pl.multiple_of((c * group + g) * L, L), L) for g in range(group)]
        expo = []
        for g in range(group):
            logf = logf_ref[rows[g], :]
            hi = logf.astype(BF16)
            lo = (logf - hi.astype(F32)).astype(BF16)
            expo.append(_dot(sums_ref[...], jnp.concatenate([hi, lo], axis=0)))
        streams = [(g, h) for g in range(group) for h in range(R_HEADS)]
        intra, q_dec, upd = {}, {}, {}
        for g, h in streams:
            cs = heads[h]
            q = q_ref[rows[g], cs]
            k = k_ref[rows[g], cs]
            v = v_ref[rows[g], cs]
            a = masks_ref[0:L, :] * _dot_nt(q, k)
            for lv in range(levels):
                x = jnp.exp2(expo[g][(1 + lv) * L:(2 + lv) * L, cs]).astype(BF16)
                a = a + masks_ref[(1 + lv) * L:(2 + lv) * L, :] * _dot_nt(q * x, k * x)
            intra[g, h] = _dot(a.astype(BF16), v)
            b = expo[g][0:L, cs]
            q_dec[g, h] = q * jnp.exp2(b).astype(BF16)
            k_dec = k * jnp.exp2(b[L - 1:L, :] - b).astype(BF16)
            upd[g, h] = _dot_tn(v, k_dec)
        for g, h in streams:
            cs = heads[h]
            st = st_sc[h]
            o = intra[g, h] + _dot_nt(q_dec[g, h], st.astype(BF16))
            st_sc[h] = st * jnp.exp2(expo[g][L - 1:L, cs]) + upd[g, h]
            gate = g_ref[rows[g], cs]
            r_ref[rows[g], cs] = (_rms(o, gn_ref[...]) * (gate * _sigmoid(gate))).astype(BF16)
        return carry

    lax.fori_loop(0, chunks // group, chunk_group, 0)

    @pl.when(step == pl.num_programs(1) - 1)
    def _():
        for h in range(R_HEADS):
            sout_ref[0, h] = st_sc[h].T


def _hgrn(q, logf, k, v, g, gn_w, state_in, batch, seq, length, chunks):
    sums, masks = _hgrn_consts(length)
    tm = length * chunks
    steps = seq // tm
    fix = lambda b, s: (0, 0)
    row = lambda b, s: (b * steps + s, 0)
    tok = pl.BlockSpec((tm, SEG_W), row)
    in_specs = [pl.BlockSpec(sums.shape, fix), pl.BlockSpec(masks.shape, fix), pl.BlockSpec((1, HEAD_W), fix),
                tok, tok, tok, tok, tok]
    args = [sums, masks, gn_w, q, logf, k, v, g]
    state_spec = pl.BlockSpec((1, R_HEADS, HEAD_W, HEAD_W), lambda b, s: (b, 0, 0, 0))
    if state_in is not None:
        in_specs.append(state_spec)
        args.append(state_in)
    return pl.pallas_call(
        functools.partial(_hgrn_kernel, length=length, chunks=chunks, has_state_in=state_in is not None),
        grid=(batch, steps),
        in_specs=in_specs,
        out_specs=[tok, state_spec],
        out_shape=[jax.ShapeDtypeStruct(q.shape, BF16),
                   jax.ShapeDtypeStruct((batch, R_HEADS, HEAD_W, HEAD_W), F32)],
        scratch_shapes=[pltpu.VMEM((R_HEADS, HEAD_W, HEAD_W), F32)],
        compiler_params=pltpu.CompilerParams(dimension_semantics=("arbitrary", "arbitrary"),
                                             vmem_limit_bytes=VMEM_LIMIT),
        name="hgrn",
    )(*args)


def _merge_kernel(x_ref, a_ref, r_ref, wo_ref, n2_ref, wr_hi_ref, wr_lo_ref, tri_ref, cnt0_ref,
                  ymid_ref, h2_ref, route_ref, cnt_ref, cnt_sc):
    step = pl.program_id(0)

    @pl.when(step == 0)
    def _():
        cnt_sc[...] = cnt0_ref[...]

    sub = tri_ref.shape[0]
    for s in range(x_ref.shape[0] // sub):
        _merge_rows(slice(s * sub, (s + 1) * sub), x_ref, a_ref, r_ref, wo_ref, n2_ref, wr_hi_ref, wr_lo_ref,
                    tri_ref, ymid_ref, h2_ref, route_ref, cnt_sc)
    cnt_ref[...] = cnt_sc[...]


def _merge_rows(rows, x_ref, a_ref, r_ref, wo_ref, n2_ref, wr_hi_ref, wr_lo_ref, tri_ref,
                ymid_ref, h2_ref, route_ref, cnt_sc):
    half = a_ref.shape[1]
    y = x_ref[rows, :] + _dot(a_ref[rows, :], wo_ref[0:half, :]) + _dot(r_ref[rows, :], wo_ref[half:, :])
    ymid_ref[rows, :] = y
    hn = _rms(y, n2_ref[...])
    h2_ref[rows, :] = hn
    hi = hn.astype(BF16)
    lo = (hn - hi.astype(F32)).astype(BF16)
    logits = _dot(hi, wr_hi_ref[...]) + _dot(lo, wr_hi_ref[...]) + _dot(hi, wr_lo_ref[...])
    lane = lax.broadcasted_iota(jnp.int32, logits.shape, 1)
    neg = -jnp.inf
    big = jnp.int32(ROUTE_W)

    def top1(mask):
        val = jnp.max(jnp.where(mask, logits, neg), axis=-1, keepdims=True)
        idx = jnp.min(jnp.where(mask & (logits == val), lane, big), axis=-1, keepdims=True)
        return val, idx

    is_group = lane < N_GROUPS
    g_max, g_sel = top1(is_group)
    p_group = 1.0 / jnp.sum(jnp.where(is_group, jnp.exp(logits - g_max), 0.0), axis=-1, keepdims=True)
    first = N_GROUPS + g_sel * EXPERTS_PER_GROUP
    in_group = (lane >= first) & (lane < first + EXPERTS_PER_GROUP)
    v1, i1 = top1(in_group)
    v2, i2 = top1(in_group & (lane != i1))
    e2 = jnp.exp(v2 - v1)
    gate1 = p_group * (1.0 / (1.0 + e2))
    gate2 = p_group * (e2 / (1.0 + e2))
    hot1 = lane == i1
    hot2 = lane == i2
    hot = jnp.where(hot1 | hot2, 1.0, 0.0)
    before = cnt_sc[...] + _dot(tri_ref[...], hot.astype(BF16))
    rank1 = jnp.sum(jnp.where(hot1, before, 0.0), axis=-1, keepdims=True)
    rank2 = jnp.sum(jnp.where(hot2, before, 0.0), axis=-1, keepdims=True)
    cnt_sc[...] = cnt_sc[...] + jnp.sum(hot, axis=0, keepdims=True)
    col = lax.broadcasted_iota(jnp.int32, (hot.shape[0], route_ref.shape[1]), 1)
    cols = [(i1 - N_GROUPS).astype(F32), (i2 - N_GROUPS).astype(F32), gate1, gate2, rank1, rank2]
    route = jnp.zeros(col.shape, F32)
    for c, val in enumerate(cols):
        route = jnp.where(col == c, val, route)
    route_ref[rows, :] = route


def _merge(x, a, r, wo_bf, n2, wr_hi, wr_lo, counts_in, tm):
    n = x.shape[0]
    row = lambda i: (i, 0)
    fix = lambda i: (0, 0)
    sub = min(256, tm)
    tri = jnp.asarray(np.tril(np.ones((sub, sub), np.float32), -1), BF16)
    return pl.pallas_call(
        _merge_kernel,
        grid=(n // tm,),
        in_specs=[pl.BlockSpec((tm, D_MODEL), row), pl.BlockSpec((tm, SEG_W), row),
                  pl.BlockSpec((tm, SEG_W), row), pl.BlockSpec((D_MODEL, D_MODEL), fix),
                  pl.BlockSpec((1, D_MODEL), fix), pl.BlockSpec((D_MODEL, ROUTE_W), fix),
                  pl.BlockSpec((D_MODEL, ROUTE_W), fix), pl.BlockSpec((sub, sub), fix),
                  pl.BlockSpec((1, ROUTE_W), fix)],
        out_specs=[pl.BlockSpec((tm, D_MODEL), row), pl.BlockSpec((tm, D_MODEL), row),
                   pl.BlockSpec((tm, 8), row), pl.BlockSpec((1, ROUTE_W), fix)],
        out_shape=[jax.ShapeDtypeStruct((n, D_MODEL), F32), jax.ShapeDtypeStruct((n, D_MODEL), F32),
                   jax.ShapeDtypeStruct((n, 8), F32), jax.ShapeDtypeStruct((1, ROUTE_W), F32)],
        scratch_shapes=[pltpu.VMEM((1, ROUTE_W), F32)],
        compiler_params=pltpu.CompilerParams(dimension_semantics=("arbitrary",),
                                             vmem_limit_bytes=VMEM_LIMIT),
        name="merge",
    )(x, a, r, wo_bf, n2, wr_hi, wr_lo, tri, counts_in)


def _dispatch_kernel(pend_ref, pcnt_ref, n_used_ref, dest_hbm, hp_ref, hs_ref, xs_hbm,
                     idx_sm, zero_sc, sem_idx, sem_out, *, tm, steps_p, steps_s, first_spare, n_blocks):
    i = pl.program_id(0)
    slot = i % 2
    n_steps = steps_p + steps_s

    def idx_copy(step, s):
        return pltpu.make_async_copy(dest_hbm.at[step], idx_sm.at[s], sem_idx.at[s])

    @pl.when(i == 0)
    def _():
        idx_copy(0, 0).start()
        zero_sc[...] = jnp.zeros(zero_sc.shape, F32)

        def fill(start):
            start = pl.multiple_of(start, EXPERT_ROWS)
            return pltpu.make_async_copy(zero_sc, xs_hbm.at[pl.ds(start, EXPERT_ROWS)], sem_out)

        for action in ("start", "wait"):
            for e in range(N_EXPERTS):
                pl.when(pcnt_ref[e] > 0)(
                    lambda e=e: getattr(fill(pend_ref[e] - EXPERT_ROWS), action)())
            for blk in range(first_spare, n_blocks):
                pl.when(blk >= n_used_ref[0])(
                    lambda blk=blk: getattr(fill(blk * EXPERT_ROWS), action)())

    idx_copy(i, slot).wait()

    @pl.when(i + 1 < n_steps)
    def _():
        idx_copy(i + 1, 1 - slot).start()

    def scatter(h_ref, idx_slot):
        for r in range(2 * tm):
            pltpu.make_async_copy(h_ref.at[pl.ds(r % tm, 1)], xs_hbm.at[pl.ds(idx_sm[idx_slot, r], 1)],
                                  sem_out).start(priority=r % 2)
        for _ in range(2):
            pltpu.make_async_copy(h_ref, xs_hbm.at[pl.ds(0, tm)], sem_out).wait()

    for parity in range(2):
        pl.when((i < steps_p) & (slot == parity))(lambda p=parity: scatter(hp_ref, p))
        pl.when((i >= steps_p) & (slot == parity))(lambda p=parity: scatter(hs_ref, p))


def _dispatch(h2_p, h2_s, dest_tbl, pends, pcounts, n_used, n_rows, tm):
    steps_p = h2_p.shape[0] // tm
    steps_s = h2_s.shape[0] // tm
    n_blocks = n_rows // EXPERT_ROWS
    first_spare = (h2_p.shape[0] + h2_s.shape[0]) * TOP_K // EXPERT_ROWS
    grid_spec = pltpu.PrefetchScalarGridSpec(
        num_scalar_prefetch=3, grid=(steps_p + steps_s,),
        in_specs=[pl.BlockSpec(memory_space=pl.ANY),
                  pl.BlockSpec((tm, D_MODEL), lambda i, pe, pc, nu: (jnp.minimum(i, steps_p - 1), 0)),
                  pl.BlockSpec((tm, D_MODEL), lambda i, pe, pc, nu: (jnp.maximum(i - steps_p, 0), 0))],
        out_specs=pl.BlockSpec(memory_space=pl.ANY),
        scratch_shapes=[pltpu.SMEM((2, 2 * tm), jnp.int32), pltpu.VMEM((EXPERT_ROWS, D_MODEL), F32),
                        pltpu.SemaphoreType.DMA((2,)), pltpu.SemaphoreType.DMA])
    return pl.pallas_call(
        functools.partial(_dispatch_kernel, tm=tm, steps_p=steps_p, steps_s=steps_s,
                          first_spare=first_spare, n_blocks=n_blocks),
        grid_spec=grid_spec,
        out_shape=jax.ShapeDtypeStruct((n_rows, D_MODEL), F32),
        compiler_params=pltpu.CompilerParams(dimension_semantics=("arbitrary",),
                                             vmem_limit_bytes=VMEM_LIMIT),
        name="dispatch",
    )(pends, pcounts, n_used, dest_tbl, h2_p, h2_s)


def _experts_kernel(blk_exp_ref, n_used_ref, x_ref, wg_ref, wu_ref, wd_ref, y_ref, wg_sc, wu_sc, wd_sc):
    i = pl.program_id(0)

    @pl.when(i < n_used_ref[0])
    def _():
        prev = blk_exp_ref[jnp.maximum(i - 1, 0)]

        @pl.when((i == 0) | (prev != blk_exp_ref[i]))
        def _():
            wg_sc[...] = wg_ref[0].astype(BF16)
            wu_sc[...] = wu_ref[0].astype(BF16)
            wd_sc[...] = wd_ref[0].astype(BF16)

        xb = x_ref[...].astype(BF16)
        gate = _dot(xb, wg_sc[...])
        up = _dot(xb, wu_sc[...])
        act = (gate * _sigmoid(gate) * up).astype(BF16)
        y_ref[...] = _dot(act, wd_sc[...])

    @pl.when(i >= n_used_ref[0])
    def _():
        y_ref[...] = jnp.zeros(y_ref.shape, F32)


def _experts(xs, blk_exp, n_used, wg, wu, wd):
    n_blocks = int(blk_exp.shape[0])
    wmap = lambda i, be, nu: (be[i], 0, 0)
    xmap = lambda i, be, nu: (jnp.minimum(i, nu[0] - 1), 0)
    grid_spec = pltpu.PrefetchScalarGridSpec(
        num_scalar_prefetch=2,
        grid=(n_blocks,),
        in_specs=[pl.BlockSpec((EXPERT_ROWS, D_MODEL), xmap),
                  pl.BlockSpec((1, D_MODEL, D_EXPERT), wmap),
                  pl.BlockSpec((1, D_MODEL, D_EXPERT), wmap),
                  pl.BlockSpec((1, D_EXPERT, D_MODEL), wmap)],
        out_specs=pl.BlockSpec((EXPERT_ROWS, D_MODEL), lambda i, be, nu: (i, 0)),
        scratch_shapes=[pltpu.VMEM((D_MODEL, D_EXPERT), BF16), pltpu.VMEM((D_MODEL, D_EXPERT), BF16),
                        pltpu.VMEM((D_EXPERT, D_MODEL), BF16)],
    )
    return pl.pallas_call(
        _experts_kernel,
        grid_spec=grid_spec,
        out_shape=jax.ShapeDtypeStruct(xs.shape, F32),
        compiler_params=pltpu.CompilerParams(dimension_semantics=("arbitrary",),
                                             vmem_limit_bytes=VMEM_LIMIT),
        name="experts",
    )(blk_exp, n_used, xs, wg, wu, wd)


def _expert_layout(counts_row, n_tok):
    counts = counts_row[0, N_GROUPS:N_GROUPS + N_EXPERTS].astype(jnp.int32)
    pcounts = (counts + EXPERT_ROWS - 1) // EXPERT_ROWS * EXPERT_ROWS
    pends = jnp.cumsum(pcounts).astype(jnp.int32)
    pstarts = pends - pcounts
    n_blocks = -(-(n_tok * TOP_K) // EXPERT_ROWS) + N_EXPERTS
    blk_start = jnp.arange(n_blocks, dtype=jnp.int32) * EXPERT_ROWS
    blk_exp = jnp.minimum(jnp.sum(blk_start[:, None] >= pends[None, :], axis=1), N_EXPERTS - 1).astype(jnp.int32)
    n_used = (pends[-1:] // EXPERT_ROWS).astype(jnp.int32)
    return pstarts, pends, pcounts, blk_exp, n_used, n_blocks * EXPERT_ROWS


def _dest_table(route, pstarts, tm):
    experts = route[:, 0:TOP_K].astype(jnp.int32)
    ranks = route[:, 2 * TOP_K:3 * TOP_K].astype(jnp.int32)
    dest = pstarts[experts] + ranks
    return dest.reshape(-1, tm, TOP_K).swapaxes(1, 2).reshape(-1, TOP_K * tm)


def _final_kernel(dest_hbm, ys_hbm, y_ref, route_ref, w_ref, o_ref, idx_sm, gbuf, sem_idx, sem_g,
                  *, tm, n_steps):
    i = pl.program_id(0)

    def idx_copy(step, slot):
        return pltpu.make_async_copy(dest_hbm.at[step], idx_sm.at[slot], sem_idx.at[slot])

    def gather(slot):
        for r in range(2 * tm):
            pltpu.make_async_copy(ys_hbm.at[pl.ds(idx_sm[slot, r], 1)], gbuf.at[slot, pl.ds(r, 1)],
                                  sem_g.at[slot]).start(priority=r % 2)

    def step(cur):
        nxt = 1 - cur
        if cur == 0:
            @pl.when(i == 0)
            def _():
                idx_copy(0, 0).start()
                idx_copy(0, 0).wait()
                gather(0)
                if n_steps > 1:
                    idx_copy(1, 1).start()

        if n_steps > 1:
            @pl.when(i + 1 < n_steps)
            def _():
                idx_copy(i + 1, nxt).wait()
                gather(nxt)

            if n_steps > 2:
                @pl.when(i + 2 < n_steps)
                def _():
                    idx_copy(i + 2, cur).start()

        pltpu.make_async_copy(ys_hbm.at[pl.ds(0, 2 * tm)], gbuf.at[cur], sem_g.at[cur]).wait()
        route = route_ref[...]
        rows = gbuf[cur]
        y = y_ref[...] + route[:, 2:3] * rows[0:tm] + route[:, 3:4] * rows[tm:]
        o_ref[...] = _rms(y, w_ref[...])

    pl.when(i % 2 == 0)(lambda: step(0))
    pl.when(i % 2 == 1)(lambda: step(1))


def _final(ymid, route, dest_tbl, ys, w, tm):
    n = ymid.shape[0]
    n_steps = n // tm
    return pl.pallas_call(
        functools.partial(_final_kernel, tm=tm, n_steps=n_steps),
        grid=(n_steps,),
        in_specs=[pl.BlockSpec(memory_space=pl.ANY), pl.BlockSpec(memory_space=pl.ANY),
                  pl.BlockSpec((tm, D_MODEL), lambda i: (i, 0)), pl.BlockSpec((tm, 8), lambda i: (i, 0)),
                  pl.BlockSpec((1, D_MODEL), lambda i: (0, 0))],
        out_specs=pl.BlockSpec((tm, D_MODEL), lambda i: (i, 0)),
        out_shape=jax.ShapeDtypeStruct((n, D_MODEL), F32),
        scratch_shapes=[pltpu.SMEM((2, 2 * tm), jnp.int32), pltpu.VMEM((2, 2 * tm, D_MODEL), F32),
                        pltpu.SemaphoreType.DMA((2,)), pltpu.SemaphoreType.DMA((2,))],
        compiler_params=pltpu.CompilerParams(dimension_semantics=("arbitrary",),
                                             vmem_limit_bytes=VMEM_LIMIT),
        name="final",
    )(dest_tbl, ys, ymid, route, w)


def kernel(x_prompt, x_sample, cache_k, cache_v, state_rec, w_in, lam_q1, lam_k1, lam_q2, lam_k2,
           subln_w, lb_param, gnorm_w, w_out, norm1_w, norm2_w, w_group, w_router,
           w_e_gate, w_e_up, w_e_down, final_w):
    assert w_in.shape[0] == 1 and lb_param.shape[0] == 2, "single-layer model"
    bp, sp, _ = x_prompt.shape
    bs, ts, _ = x_sample.shape
    past = cache_k.shape[2]
    assert sp % CHUNK == 0 and past % CHUNK == 0 and ts <= CHUNK and ts & (ts - 1) == 0
    n_p, n_s = bp * sp, bs * ts
    n_all = n_p + n_s
    tm = 512 if (n_p % 512 == 0 and n_s % 512 == 0) else math.gcd(n_p, n_s)

    w_in_bf = w_in[0].astype(BF16)
    w_out_bf = w_out[0].astype(BF16)
    n1 = norm1_w[0].reshape(1, D_MODEL)
    n2 = norm2_w[0].reshape(1, D_MODEL)
    lam_vecs = jnp.stack([lam_q1[0], lam_k1[0], lam_q2[0], lam_k2[0]]).astype(F32)
    sub_w = subln_w[0].reshape(1, HEAD_W)
    gn_w = gnorm_w[0].reshape(1, HEAD_W)
    w_route = jnp.zeros((D_MODEL, ROUTE_W), F32)
    w_route = w_route.at[:, 0:N_GROUPS].set(w_group[0]).at[:, N_GROUPS:N_GROUPS + N_EXPERTS].set(w_router[0])
    wr_hi = w_route.astype(BF16)
    wr_lo = (w_route - wr_hi.astype(F32)).astype(BF16)

    xp = x_prompt.reshape(n_p, D_MODEL)
    xs = x_sample.reshape(n_s, D_MODEL)

    kf, vf, kb, qr, logf, kr, ir, gr, qt, vt = _inproj(xp, n1, w_in_bf, lb_param, tm, True)
    a_p = _attn_prompt(qt, kb, vt, lam_vecs, sub_w.reshape(HEAD_W, 1), bp, sp)
    chunks = 8 if sp % (8 * CHUNK) == 0 else 1
    r_p, state_p = _hgrn(qr, logf, kr, ir, gr, gn_w, None, bp, sp, CHUNK, chunks)
    ymid_p, h2_p, route_p, counts_p = _merge(xp, a_p, r_p, w_out_bf, n2, wr_hi, wr_lo,
                                             jnp.zeros((1, ROUTE_W), F32), tm)
    k_prompt = kf.reshape(1, bp, sp, A_HEADS, HEAD_W)
    v_prompt = vf.reshape(1, bp, sp, A_HEADS, HEAD_W)

    kf, vf, kb, qr, logf, kr, ir, gr, q, vb = _inproj(xs, n1, w_in_bf, lb_param, tm, False)
    pair_rows = (bs, past // 2, 2 * A_HEADS, HEAD_W)
    a_s = _attn_sample(q, cache_k.reshape(pair_rows), cache_v.reshape(pair_rows), kb, vb,
                       lam_vecs, sub_w, bs, ts)
    r_s, state_s = _hgrn(qr, logf, kr, ir, gr, gn_w, state_rec[0], bs, ts, ts, 1)
    ymid_s, h2_s, route_s, counts = _merge(xs, a_s, r_s, w_out_bf, n2, wr_hi, wr_lo, counts_p, tm)
    k_sample = kf.reshape(1, bs, ts, A_HEADS, HEAD_W)
    v_sample = vf.reshape(1, bs, ts, A_HEADS, HEAD_W)

    pstarts, pends, pcounts, blk_exp, n_used, n_rows = _expert_layout(counts, n_all)
    td = min(256, math.gcd(n_p, n_s))
    dest_p = _dest_table(route_p, pstarts, td)
    dest_s = _dest_table(route_s, pstarts, td)
    xs_rows = _dispatch(h2_p, h2_s, jnp.concatenate([dest_p, dest_s], axis=0), pends, pcounts, n_used,
                        n_rows, td)
    ys_rows = _experts(xs_rows, blk_exp, n_used, w_e_gate[0], w_e_up[0], w_e_down[0])

    fw = final_w.reshape(1, D_MODEL)
    y_prompt = _final(ymid_p, route_p, dest_p, ys_rows, fw, td).reshape(bp, sp, D_MODEL)
    y_sample = _final(ymid_s, route_s, dest_s, ys_rows, fw, td).reshape(bs, ts, D_MODEL)
    return (y_prompt, y_sample, k_prompt, v_prompt, state_p[None], k_sample, v_sample, state_s[None])
```

```python
import functools
import math

import numpy as np
import jax
import jax.numpy as jnp
from jax import lax
from jax.experimental import pallas as pl
from jax.experimental.pallas import tpu as pltpu

F32 = jnp.float32
BF16 = jnp.bfloat16

D_MODEL = 1024
RMS_EPS = 1e-6
CHUNK = 64
A_HEADS = 4
A_HEAD_DIM = 64
HEAD_W = 2 * A_HEAD_DIM
KEY_BLOCK = 256
VT_ROWS = HEAD_W + 16
LOG2_E = math.log2(math.e)
R_HEADS = 4
SEG_W = 512
N_SEG = 7
N_GROUPS = 4
EXPERTS_PER_GROUP = 8
N_EXPERTS = N_GROUPS * EXPERTS_PER_GROUP
TOP_K = 2
D_EXPERT = 512
EXPERT_ROWS = 256
ROUTE_W = 128
LAM_INIT = 0.8 - 0.6 * math.exp(-0.3 * 0)
VMEM_LIMIT = 56 * 1024 * 1024


def _sigmoid(x):
    return 1.0 / (1.0 + jnp.exp(-x))


def _dot(a, b):
    return jnp.dot(a, b, preferred_element_type=F32)


def _dot_nt(a, b):
    return lax.dot_general(a, b, (((1,), (1,)), ((), ())), preferred_element_type=F32)


def _dot_tn(a, b):
    return lax.dot_general(a, b, (((0,), (0,)), ((), ())), preferred_element_type=F32)


def _rms(x, w):
    return x * lax.rsqrt(jnp.mean(x * x, axis=-1, keepdims=True) + RMS_EPS) * w


def _inproj_kernel(x_ref, n1_ref, w_ref, lbp_ref, kf_ref, vf_ref, kb_ref, qr_ref, logf_ref, kr_ref,
                   ir_ref, gr_ref, qa_ref, va_ref, *, transposed_qv):
    p = lbp_ref[...]
    e = jnp.exp(p - jnp.max(p, axis=0, keepdims=True))
    lb = e[0:1] / jnp.sum(e, axis=0, keepdims=True)
    ones = jnp.ones((VT_ROWS - HEAD_W, KEY_BLOCK), BF16)

    sub = min(KEY_BLOCK, x_ref.shape[0])
    for s in range(x_ref.shape[0] // sub):
        rows = slice(s * sub, (s + 1) * sub)
        h = _rms(x_ref[rows, :], n1_ref[...]).astype(BF16)

        def seg(i):
            return _dot(h, w_ref[:, i * SEG_W:(i + 1) * SEG_W])

        q = seg(0) * (A_HEAD_DIM ** -0.5)
        k = seg(1)
        for hd in range(A_HEADS):
            kf_ref[rows, hd, :] = k[:, hd * HEAD_W:(hd + 1) * HEAD_W]
        kb_ref[rows, :] = k.astype(BF16)
        v = seg(2)
        for hd in range(A_HEADS):
            vf_ref[rows, hd, :] = v[:, hd * HEAD_W:(hd + 1) * HEAD_W]
        if transposed_qv:
            qa_ref[s] = (q * LOG2_E).T.astype(BF16)
            vt = v.T.astype(BF16)
            for hd in range(A_HEADS):
                va_ref[s, hd * VT_ROWS:hd * VT_ROWS + HEAD_W, :] = vt[hd * HEAD_W:(hd + 1) * HEAD_W, :]
                va_ref[s, hd * VT_ROWS + HEAD_W:(hd + 1) * VT_ROWS, :] = ones
        else:
            qa_ref[rows, :] = q.astype(BF16)
            va_ref[rows, :] = v.astype(BF16)
        qr = seg(3)
        qr_ref[rows, :] = (qr * _sigmoid(qr)).astype(BF16)
        f = lb + (1.0 - lb) * _sigmoid(seg(4))
        logf_ref[rows, :] = jnp.log(f) * LOG2_E
        kr_ref[rows, :] = (1.0 - f).astype(BF16)
        ir_ref[rows, :] = seg(5).astype(BF16)
        gr_ref[rows, :] = seg(6)


def _inproj(x, n1, w_bf, lbp, tm, transposed_qv):
    n = x.shape[0]
    row = lambda i: (i, 0)
    fix = lambda i: (0, 0)
    out = lambda dt: jax.ShapeDtypeStruct((n, SEG_W), dt)
    ospec = pl.BlockSpec((tm, SEG_W), row)
    hspec = pl.BlockSpec((tm, A_HEADS, HEAD_W), lambda i: (i, 0, 0))
    hout = jax.ShapeDtypeStruct((n, A_HEADS, HEAD_W), F32)
    slabs = tm // KEY_BLOCK
    tspec = lambda rows: pl.BlockSpec((slabs, rows, KEY_BLOCK), lambda i: (i, 0, 0))
    tout = lambda rows: jax.ShapeDtypeStruct((n // KEY_BLOCK, rows, KEY_BLOCK), BF16)
    if transposed_qv:
        qv_specs = [tspec(SEG_W), tspec(A_HEADS * VT_ROWS)]
        qv_shapes = [tout(SEG_W), tout(A_HEADS * VT_ROWS)]
    else:
        qv_specs = [ospec, ospec]
        qv_shapes = [out(BF16), out(BF16)]
    return pl.pallas_call(
        functools.partial(_inproj_kernel, transposed_qv=transposed_qv),
        grid=(n // tm,),
        in_specs=[pl.BlockSpec((tm, D_MODEL), row), pl.BlockSpec((1, D_MODEL), fix),
                  pl.BlockSpec((D_MODEL, N_SEG * SEG_W), fix), pl.BlockSpec(lbp.shape, fix)],
        out_specs=[hspec, hspec] + [ospec] * 6 + qv_specs,
        out_shape=[hout, hout, out(BF16), out(BF16), out(F32), out(BF16), out(BF16), out(F32)] + qv_shapes,
        compiler_params=pltpu.CompilerParams(dimension_semantics=("arbitrary",),
                                             vmem_limit_bytes=VMEM_LIMIT),
        name="inproj",
    )(x, n1, w_bf, lbp)


def _lam_value(lam_ref):
    l = lam_ref[...]
    s1 = jnp.sum(l[0:1] * l[1:2], axis=-1, keepdims=True)
    s2 = jnp.sum(l[2:3] * l[3:4], axis=-1, keepdims=True)
    return jnp.exp(s1) - jnp.exp(s2) + LAM_INIT


def _split_maps(q):
    lane = lax.broadcasted_iota(jnp.int32, q.shape, 1)
    zero = jnp.zeros_like(q)
    return jnp.concatenate([jnp.where(lane < A_HEAD_DIM, q, zero),
                            jnp.where(lane >= A_HEAD_DIM, q, zero)], axis=0)


def _attn_finish(o_num, l, lam, sub_w, tq):
    o = o_num[0:tq] / l[0:tq] - lam * (o_num[tq:] / l[tq:])
    return (_rms(o, sub_w) * (1.0 - LAM_INIT)).astype(BF16)


def _attn_prompt_kernel(lam_ref, sub_ref, bias_ref, qt_ref, k_ref, vt_ref, o_ref,
                        qbd_sc, m_sc, acc_sc, sa_sc, sb_sc, *, qblocks):
    i = pl.program_id(2)
    kb = KEY_BLOCK

    sub = lax.broadcasted_iota(jnp.int32, (HEAD_W, kb), 0)
    for r in range(2 * qblocks):
        qt = qt_ref[r % qblocks]
        keep = (sub < A_HEAD_DIM) if r < qblocks else (sub >= A_HEAD_DIM)
        qbd_sc[r] = jnp.where(keep, qt, jnp.zeros_like(qt))
    m_sc[...] = jnp.full(m_sc.shape, -jnp.inf, F32)
    acc_sc[...] = jnp.zeros(acc_sc.shape, F32)

    def scores(r, key_block):
        return _dot(k_ref[pl.ds(pl.multiple_of(key_block * kb, kb), kb), :], qbd_sc[r])

    def update(r, key_block, st):
        m_old = m_sc[r:r + 1, :]
        m_new = jnp.maximum(m_old, jnp.max(st, axis=0, keepdims=True))
        p = jnp.exp2(st - m_new).astype(BF16)
        acc_sc[r] = jnp.exp2(m_old - m_new) * acc_sc[r] + _dot(vt_ref[key_block], p)
        m_sc[r:r + 1, :] = m_new

    n_q = 2 * qblocks
    first = i * qblocks
    for r in range(n_q):
        sa_sc[r] = scores(r, 0)

    def visible_blocks(start, count):
        lead = 2
        chains = [(j, r) for j in range(count) for r in range(n_q)]
        for c in range(len(chains) + lead):
            if c < len(chains):
                j, r = chains[c]
                (sb_sc if j % 2 == 0 else sa_sc)[r] = scores(r, start + j + 1)
            if c >= lead:
                j, r = chains[c - lead]
                update(r, start + j, (sa_sc if j % 2 == 0 else sb_sc)[r])

    def tile_pair(t, carry):
        visible_blocks(2 * t * qblocks, 2 * qblocks)
        return carry

    lax.fori_loop(0, i // 2, tile_pair, 0)
    pl.when(i % 2 == 1)(lambda: visible_blocks((i - 1) * qblocks, qblocks))

    work = [(kl, r) for kl in range(qblocks) for r in range(n_q) if r % qblocks >= kl]
    ahead = 4
    pending = {}
    for n, (kl, r) in enumerate(work):
        for kl2, r2 in work[n:n + 1 + ahead]:
            if kl2 > 0 and (kl2, r2) not in pending:
                pending[(kl2, r2)] = scores(r2, first + kl2)
        st = sa_sc[r] if kl == 0 else pending.pop((kl, r))
        update(r, first + kl, st + bias_ref[...] if r % qblocks == kl else st)

    lam = _lam_value(lam_ref)
    for ql in range(qblocks):
        a1 = acc_sc[ql]
        a2 = acc_sc[qblocks + ql]
        ot = a1[0:HEAD_W] / a1[HEAD_W:HEAD_W + 1] - lam * (a2[0:HEAD_W] / a2[HEAD_W:HEAD_W + 1])
        ot = ot * lax.rsqrt(jnp.mean(ot * ot, axis=0, keepdims=True) + RMS_EPS) * sub_ref[...]
        o_ref[ql * kb:(ql + 1) * kb, :] = (ot * (1.0 - LAM_INIT)).T.astype(BF16)


def _attn_prompt(qt, k, vt, lam_vecs, sub_col, batch, seq):
    kb = KEY_BLOCK
    tq = min(1024, seq)
    qblocks = tq // kb
    assert qblocks % 2 == 0, "the key-block loop is unrolled by two"
    nq = seq // tq
    pos = np.arange(kb) // CHUNK
    bias = jnp.asarray(np.where(pos[:, None] <= pos[None, :], 0.0, -np.inf), F32)
    fix = lambda b, h, i: (0, 0)
    return pl.pallas_call(
        functools.partial(_attn_prompt_kernel, qblocks=qblocks),
        grid=(batch, A_HEADS, nq),
        in_specs=[pl.BlockSpec((4, A_HEAD_DIM), fix), pl.BlockSpec((HEAD_W, 1), fix),
                  pl.BlockSpec((kb, kb), fix),
                  pl.BlockSpec((qblocks, HEAD_W, kb), lambda b, h, i: (b * nq + i, h, 0)),
                  pl.BlockSpec((seq, HEAD_W), lambda b, h, i: (b, h)),
                  pl.BlockSpec((seq // kb, VT_ROWS, kb), lambda b, h, i: (b, h, 0))],
        out_specs=pl.BlockSpec((tq, HEAD_W), lambda b, h, i: (b * nq + i, h)),
        out_shape=jax.ShapeDtypeStruct(k.shape, BF16),
        scratch_shapes=[pltpu.VMEM((2 * qblocks, HEAD_W, kb), BF16), pltpu.VMEM((2 * qblocks, kb), F32),
                        pltpu.VMEM((2 * qblocks, VT_ROWS, kb), F32),
                        pltpu.VMEM((2 * qblocks, kb, kb), F32), pltpu.VMEM((2 * qblocks, kb, kb), F32)],
        compiler_params=pltpu.CompilerParams(
            dimension_semantics=("arbitrary", "arbitrary", "arbitrary"), vmem_limit_bytes=VMEM_LIMIT),
        name="attn_prompt",
    )(lam_vecs, sub_col, bias, qt, k, vt)


def _attn_sample_kernel(lam_ref, sub_ref, bias_c_ref, bias_n_ref, q_ref, kc_ref, vc_ref, kn_ref, vn_ref,
                        o_ref, *, t):
    lam = _lam_value(lam_ref)
    heads = [slice(h * HEAD_W, (h + 1) * HEAD_W) for h in range(A_HEADS)]
    qbd = jnp.concatenate([_split_maps(q_ref[:, cs]) for cs in heads], axis=0)
    keys = [kc_ref[0].astype(BF16), jnp.concatenate([kn_ref[:, cs] for cs in heads], axis=0)]
    vals = [vc_ref[0].astype(BF16), jnp.concatenate([vn_ref[:, cs] for cs in heads], axis=0)]
    scores = [_dot_nt(qbd, k) + b[...] for k, b in zip(keys, (bias_c_ref, bias_n_ref))]
    m = functools.reduce(jnp.maximum, [jnp.max(sc, axis=-1, keepdims=True) for sc in scores])
    probs = [jnp.exp(sc - m) for sc in scores]
    l = sum(jnp.sum(p, axis=-1, keepdims=True) for p in probs)
    o_num = sum(_dot(p.astype(BF16), v) for p, v in zip(probs, vals))
    for h, cs in enumerate(heads):
        rows = slice(h * 2 * t, (h + 1) * 2 * t)
        o_ref[:, cs] = _attn_finish(o_num[rows], l[rows], lam, sub_ref[...], t)


def _attn_sample(q, cache_k, cache_v, k_new, v_new, lam_vecs, sub_w, batch, t):
    fix = lambda b: (0, 0)
    row = lambda b: (b, 0)
    cache = pl.BlockSpec((1,) + cache_k.shape[1:], lambda b: (b, 0, 0))
    q_head = np.arange(A_HEADS * 2 * t) // (2 * t)
    off = lambda key_head: jnp.asarray(np.where(q_head[:, None] == key_head[None, :], 0.0, -np.inf), F32)
    bias_c = off(np.arange(cache_k.shape[1]) % A_HEADS)
    bias_n = off(np.arange(A_HEADS * t) // t)
    return pl.pallas_call(
        functools.partial(_attn_sample_kernel, t=t),
        grid=(batch,),
        in_specs=[pl.BlockSpec((4, A_HEAD_DIM), fix), pl.BlockSpec((1, HEAD_W), fix),
                  pl.BlockSpec(bias_c.shape, fix), pl.BlockSpec(bias_n.shape, fix),
                  pl.BlockSpec((t, SEG_W), row),
                  cache, cache, pl.BlockSpec((t, SEG_W), row), pl.BlockSpec((t, SEG_W), row)],
        out_specs=pl.BlockSpec((t, SEG_W), row),
        out_shape=jax.ShapeDtypeStruct(q.shape, BF16),
        compiler_params=pltpu.CompilerParams(dimension_semantics=("arbitrary",),
                                             vmem_limit_bytes=VMEM_LIMIT),
        name="attn_sample",
    )(lam_vecs, sub_w, bias_c, bias_n, q, cache_k, cache_v, k_new, v_new)


def _hgrn_consts(length):
    t = np.arange(length)[:, None]
    j = np.arange(length)[None, :]
    sums = [j <= t]
    masks = [j == t]
    blk = length
    while blk >= 2:
        half = blk // 2
        mid_t = (t // blk) * blk + half
        mid_j = (j // blk) * blk + half
        sums.append(np.where(t >= mid_t, (j >= mid_t) & (j <= t), (j > t) & (j < mid_t)))
        masks.append((t // blk == j // blk) & (t >= mid_t) & (j < mid_j))
        blk = half
    sums = np.tile(np.concatenate(sums, axis=0).astype(np.float32), (1, 2))
    masks = np.concatenate(masks, axis=0).astype(np.float32)
    return jnp.asarray(sums, BF16), jnp.asarray(masks, F32)


def _hgrn_kernel(*refs, length, chunks, has_state_in):
    if has_state_in:
        (sums_ref, masks_ref, gn_ref, q_ref, logf_ref, k_ref, v_ref, g_ref, s0_ref,
         r_ref, sout_ref, st_sc) = refs
    else:
        (sums_ref, masks_ref, gn_ref, q_ref, logf_ref, k_ref, v_ref, g_ref,
         r_ref, sout_ref, st_sc) = refs
    step = pl.program_id(1)
    levels = int(math.log2(length))
    L = length
    group = math.gcd(chunks, 8)

    @pl.when(step == 0)
    def _():
        for h in range(R_HEADS):
            if has_state_in:
                st_sc[h] = s0_ref[0, h].T
            else:
                st_sc[h] = jnp.zeros(st_sc.shape[1:], F32)

    heads = [slice(h * HEAD_W, (h + 1) * HEAD_W) for h in range(R_HEADS)]

    def chunk_group(c, carry):
        rows = [pl.ds(pl.multiple_of((c * group + g) * L, L), L) for g in range(group)]
        expo = []
        for g in range(group):
            logf = logf_ref[rows[g], :]
            hi = logf.astype(BF16)
            lo = (logf - hi.astype(F32)).astype(BF16)
            expo.append(_dot(sums_ref[...], jnp.concatenate([hi, lo], axis=0)))
        streams = [(g, h) for g in range(group) for h in range(R_HEADS)]
        intra, q_dec, upd = {}, {}, {}
        for g, h in streams:
            cs = heads[h]
            q = q_ref[rows[g], cs]
            k = k_ref[rows[g], cs]
            v = v_ref[rows[g], cs]
            a = masks_ref[0:L, :] * _dot_nt(q, k)
            for lv in range(levels):
                x = jnp.exp2(expo[g][(1 + lv) * L:(2 + lv) * L, cs]).astype(BF16)
                a = a + masks_ref[(1 + lv) * L:(2 + lv) * L, :] * _dot_nt(q * x, k * x)
            intra[g, h] = _dot(a.astype(BF16), v)
            b = expo[g][0:L, cs]
            q_dec[g, h] = q * jnp.exp2(b).astype(BF16)
            k_dec = k * jnp.exp2(b[L - 1:L, :] - b).astype(BF16)
            upd[g, h] = _dot_tn(v, k_dec)
        for g, h in streams:
            cs = heads[h]
            st = st_sc[h]
            o = intra[g, h] + _dot_nt(q_dec[g, h], st.astype(BF16))
            st_sc[h] = st * jnp.exp2(expo[g][L - 1:L, cs]) + upd[g, h]
            gate = g_ref[rows[g], cs]
            r_ref[rows[g], cs] = (_rms(o, gn_ref[...]) * (gate * _sigmoid(gate))).astype(BF16)
        return carry

    lax.fori_loop(0, chunks // group, chunk_group, 0)

    @pl.when(step == pl.num_programs(1) - 1)
    def _():
        for h in range(R_HEADS):
            sout_ref[0, h] = st_sc[h].T


def _hgrn(q, logf, k, v, g, gn_w, state_in, batch, seq, length, chunks):
    sums, masks = _hgrn_consts(length)
    tm = length * chunks
    steps = seq // tm
    fix = lambda b, s: (0, 0)
    row = lambda b, s: (b * steps + s, 0)
    tok = pl.BlockSpec((tm, SEG_W), row)
    in_specs = [pl.BlockSpec(sums.shape, fix), pl.BlockSpec(masks.shape, fix), pl.BlockSpec((1, HEAD_W), fix),
                tok, tok, tok, tok, tok]
    args = [sums, masks, gn_w, q, logf, k, v, g]
    state_spec = pl.BlockSpec((1, R_HEADS, HEAD_W, HEAD_W), lambda b, s: (b, 0, 0, 0))
    if state_in is not None:
        in_specs.append(state_spec)
        args.append(state_in)
    return pl.pallas_call(
        functools.partial(_hgrn_kernel, length=length, chunks=chunks, has_state_in=state_in is not None),
        grid=(batch, steps),
        in_specs=in_specs,
        out_specs=[tok, state_spec],
        out_shape=[jax.ShapeDtypeStruct(q.shape, BF16),
                   jax.ShapeDtypeStruct((batch, R_HEADS, HEAD_W, HEAD_W), F32)],
        scratch_shapes=[pltpu.VMEM((R_HEADS, HEAD_W, HEAD_W), F32)],
        compiler_params=pltpu.CompilerParams(dimension_semantics=("arbitrary", "arbitrary"),
                                             vmem_limit_bytes=VMEM_LIMIT),
        name="hgrn",
    )(*args)


def _merge_kernel(x_ref, a_ref, r_ref, wo_ref, n2_ref, wr_hi_ref, wr_lo_ref, tri_ref, cnt0_ref,
                  ymid_ref, h2_ref, route_ref, cnt_ref, cnt_sc):
    step = pl.program_id(0)

    @pl.when(step == 0)
    def _():
        cnt_sc[...] = cnt0_ref[...]

    sub = tri_ref.shape[0]
    for s in range(x_ref.shape[0] // sub):
        _merge_rows(slice(s * sub, (s + 1) * sub), x_ref, a_ref, r_ref, wo_ref, n2_ref, wr_hi_ref, wr_lo_ref,
                    tri_ref, ymid_ref, h2_ref, route_ref, cnt_sc)
    cnt_ref[...] = cnt_sc[...]


def _merge_rows(rows, x_ref, a_ref, r_ref, wo_ref, n2_ref, wr_hi_ref, wr_lo_ref, tri_ref,
                ymid_ref, h2_ref, route_ref, cnt_sc):
    half = a_ref.shape[1]
    y = x_ref[rows, :] + _dot(a_ref[rows, :], wo_ref[0:half, :]) + _dot(r_ref[rows, :], wo_ref[half:, :])
    ymid_ref[rows, :] = y
    hn = _rms(y, n2_ref[...])
    h2_ref[rows, :] = hn
    hi = hn.astype(BF16)
    lo = (hn - hi.astype(F32)).astype(BF16)
    logits = _dot(hi, wr_hi_ref[...]) + _dot(lo, wr_hi_ref[...]) + _dot(hi, wr_lo_ref[...])
    lane = lax.broadcasted_iota(jnp.int32, logits.shape, 1)
    neg = -jnp.inf
    big = jnp.int32(ROUTE_W)

    def top1(mask):
        val = jnp.max(jnp.where(mask, logits, neg), axis=-1, keepdims=True)
        idx = jnp.min(jnp.where(mask & (logits == val), lane, big), axis=-1, keepdims=True)
        return val, idx

    is_group = lane < N_GROUPS
    g_max, g_sel = top1(is_group)
    p_group = 1.0 / jnp.sum(jnp.where(is_group, jnp.exp(logits - g_max), 0.0), axis=-1, keepdims=True)
    first = N_GROUPS + g_sel * EXPERTS_PER_GROUP
    in_group = (lane >= first) & (lane < first + EXPERTS_PER_GROUP)
    v1, i1 = top1(in_group)
    v2, i2 = top1(in_group & (lane != i1))
    e2 = jnp.exp(v2 - v1)
    gate1 = p_group * (1.0 / (1.0 + e2))
    gate2 = p_group * (e2 / (1.0 + e2))
    hot1 = lane == i1
    hot2 = lane == i2
    hot = jnp.where(hot1 | hot2, 1.0, 0.0)
    before = cnt_sc[...] + _dot(tri_ref[...], hot.astype(BF16))
    rank1 = jnp.sum(jnp.where(hot1, before, 0.0), axis=-1, keepdims=True)
    rank2 = jnp.sum(jnp.where(hot2, before, 0.0), axis=-1, keepdims=True)
    cnt_sc[...] = cnt_sc[...] + jnp.sum(hot, axis=0, keepdims=True)
    col = lax.broadcasted_iota(jnp.int32, (hot.shape[0], route_ref.shape[1]), 1)
    cols = [(i1 - N_GROUPS).astype(F32), (i2 - N_GROUPS).astype(F32), gate1, gate2, rank1, rank2]
    route = jnp.zeros(col.shape, F32)
    for c, val in enumerate(cols):
        route = jnp.where(col == c, val, route)
    route_ref[rows, :] = route


def _merge(x, a, r, wo_bf, n2, wr_hi, wr_lo, counts_in, tm):
    n = x.shape[0]
    row = lambda i: (i, 0)
    fix = lambda i: (0, 0)
    sub = min(256, tm)
    tri = jnp.asarray(np.tril(np.ones((sub, sub), np.float32), -1), BF16)
    return pl.pallas_call(
        _merge_kernel,
        grid=(n // tm,),
        in_specs=[pl.BlockSpec((tm, D_MODEL), row), pl.BlockSpec((tm, SEG_W), row),
                  pl.BlockSpec((tm, SEG_W), row), pl.BlockSpec((D_MODEL, D_MODEL), fix),
                  pl.BlockSpec((1, D_MODEL), fix), pl.BlockSpec((D_MODEL, ROUTE_W), fix),
                  pl.BlockSpec((D_MODEL, ROUTE_W), fix), pl.BlockSpec((sub, sub), fix),
                  pl.BlockSpec((1, ROUTE_W), fix)],
        out_specs=[pl.BlockSpec((tm, D_MODEL), row), pl.BlockSpec((tm, D_MODEL), row),
                   pl.BlockSpec((tm, 8), row), pl.BlockSpec((1, ROUTE_W), fix)],
        out_shape=[jax.ShapeDtypeStruct((n, D_MODEL), F32), jax.ShapeDtypeStruct((n, D_MODEL), F32),
                   jax.ShapeDtypeStruct((n, 8), F32), jax.ShapeDtypeStruct((1, ROUTE_W), F32)],
        scratch_shapes=[pltpu.VMEM((1, ROUTE_W), F32)],
        compiler_params=pltpu.CompilerParams(dimension_semantics=("arbitrary",),
                                             vmem_limit_bytes=VMEM_LIMIT),
        name="merge",
    )(x, a, r, wo_bf, n2, wr_hi, wr_lo, tri, counts_in)


def _dispatch_kernel(pend_ref, pcnt_ref, n_used_ref, dest_hbm, hp_ref, hs_ref, xs_hbm,
                     idx_sm, zero_sc, sem_idx, sem_out, *, tm, steps_p, steps_s, first_spare, n_blocks):
    i = pl.program_id(0)
    slot = i % 2
    n_steps = steps_p + steps_s

    def idx_copy(step, s):
        return pltpu.make_async_copy(dest_hbm.at[step], idx_sm.at[s], sem_idx.at[s])

    @pl.when(i == 0)
    def _():
        idx_copy(0, 0).start()
        zero_sc[...] = jnp.zeros(zero_sc.shape, F32)

        def fill(start):
            start = pl.multiple_of(start, EXPERT_ROWS)
            return pltpu.make_async_copy(zero_sc, xs_hbm.at[pl.ds(start, EXPERT_ROWS)], sem_out)

        for action in ("start", "wait"):
            for e in range(N_EXPERTS):
                pl.when(pcnt_ref[e] > 0)(
                    lambda e=e: getattr(fill(pend_ref[e] - EXPERT_ROWS), action)())
            for blk in range(first_spare, n_blocks):
                pl.when(blk >= n_used_ref[0])(
                    lambda blk=blk: getattr(fill(blk * EXPERT_ROWS), action)())

    idx_copy(i, slot).wait()

    @pl.when(i + 1 < n_steps)
    def _():
        idx_copy(i + 1, 1 - slot).start()

    def scatter(h_ref, idx_slot):
        for r in range(2 * tm):
            pltpu.make_async_copy(h_ref.at[pl.ds(r % tm, 1)], xs_hbm.at[pl.ds(idx_sm[idx_slot, r], 1)],
                                  sem_out).start(priority=r % 2)
        for _ in range(2):
            pltpu.make_async_copy(h_ref, xs_hbm.at[pl.ds(0, tm)], sem_out).wait()

    for parity in range(2):
        pl.when((i < steps_p) & (slot == parity))(lambda p=parity: scatter(hp_ref, p))
        pl.when((i >= steps_p) & (slot == parity))(lambda p=parity: scatter(hs_ref, p))


def _dispatch(h2_p, h2_s, dest_tbl, pends, pcounts, n_used, n_rows, tm):
    steps_p = h2_p.shape[0] // tm
    steps_s = h2_s.shape[0] // tm
    n_blocks = n_rows // EXPERT_ROWS
    first_spare = (h2_p.shape[0] + h2_s.shape[0]) * TOP_K // EXPERT_ROWS
    grid_spec = pltpu.PrefetchScalarGridSpec(
        num_scalar_prefetch=3, grid=(steps_p + steps_s,),
        in_specs=[pl.BlockSpec(memory_space=pl.ANY),
                  pl.BlockSpec((tm, D_MODEL), lambda i, pe, pc, nu: (jnp.minimum(i, steps_p - 1), 0)),
                  pl.BlockSpec((tm, D_MODEL), lambda i, pe, pc, nu: (jnp.maximum(i - steps_p, 0), 0))],
        out_specs=pl.BlockSpec(memory_space=pl.ANY),
        scratch_shapes=[pltpu.SMEM((2, 2 * tm), jnp.int32), pltpu.VMEM((EXPERT_ROWS, D_MODEL), F32),
                        pltpu.SemaphoreType.DMA((2,)), pltpu.SemaphoreType.DMA])
    return pl.pallas_call(
        functools.partial(_dispatch_kernel, tm=tm, steps_p=steps_p, steps_s=steps_s,
                          first_spare=first_spare, n_blocks=n_blocks),
        grid_spec=grid_spec,
        out_shape=jax.ShapeDtypeStruct((n_rows, D_MODEL), F32),
        compiler_params=pltpu.CompilerParams(dimension_semantics=("arbitrary",),
                                             vmem_limit_bytes=VMEM_LIMIT),
        name="dispatch",
    )(pends, pcounts, n_used, dest_tbl, h2_p, h2_s)


def _experts_kernel(blk_exp_ref, n_used_ref, x_ref, wg_ref, wu_ref, wd_ref, y_ref, wg_sc, wu_sc, wd_sc):
    i = pl.program_id(0)

    @pl.when(i < n_used_ref[0])
    def _():
        prev = blk_exp_ref[jnp.maximum(i - 1, 0)]

        @pl.when((i == 0) | (prev != blk_exp_ref[i]))
        def _():
            wg_sc[...] = wg_ref[0].astype(BF16)
            wu_sc[...] = wu_ref[0].astype(BF16)
            wd_sc[...] = wd_ref[0].astype(BF16)

        xb = x_ref[...].astype(BF16)
        gate = _dot(xb, wg_sc[...])
        up = _dot(xb, wu_sc[...])
        act = (gate * _sigmoid(gate) * up).astype(BF16)
        y_ref[...] = _dot(act, wd_sc[...])

    @pl.when(i >= n_used_ref[0])
    def _():
        y_ref[...] = jnp.zeros(y_ref.shape, F32)


def _experts(xs, blk_exp, n_used, wg, wu, wd):
    n_blocks = int(blk_exp.shape[0])
    wmap = lambda i, be, nu: (be[i], 0, 0)
    xmap = lambda i, be, nu: (jnp.minimum(i, nu[0] - 1), 0)
    grid_spec = pltpu.PrefetchScalarGridSpec(
        num_scalar_prefetch=2,
        grid=(n_blocks,),
        in_specs=[pl.BlockSpec((EXPERT_ROWS, D_MODEL), xmap),
                  pl.BlockSpec((1, D_MODEL, D_EXPERT), wmap),
                  pl.BlockSpec((1, D_MODEL, D_EXPERT), wmap),
                  pl.BlockSpec((1, D_EXPERT, D_MODEL), wmap)],
        out_specs=pl.BlockSpec((EXPERT_ROWS, D_MODEL), lambda i, be, nu: (i, 0)),
        scratch_shapes=[pltpu.VMEM((D_MODEL, D_EXPERT), BF16), pltpu.VMEM((D_MODEL, D_EXPERT), BF16),
                        pltpu.VMEM((D_EXPERT, D_MODEL), BF16)],
    )
    return pl.pallas_call(
        _experts_kernel,
        grid_spec=grid_spec,
        out_shape=jax.ShapeDtypeStruct(xs.shape, F32),
        compiler_params=pltpu.CompilerParams(dimension_semantics=("arbitrary",),
                                             vmem_limit_bytes=VMEM_LIMIT),
        name="experts",
    )(blk_exp, n_used, xs, wg, wu, wd)


def _expert_layout(counts_row, n_tok):
    counts = counts_row[0, N_GROUPS:N_GROUPS + N_EXPERTS].astype(jnp.int32)
    pcounts = (counts + EXPERT_ROWS - 1) // EXPERT_ROWS * EXPERT_ROWS
    pends = jnp.cumsum(pcounts).astype(jnp.int32)
    pstarts = pends - pcounts
    n_blocks = -(-(n_tok * TOP_K) // EXPERT_ROWS) + N_EXPERTS
    blk_start = jnp.arange(n_blocks, dtype=jnp.int32) * EXPERT_ROWS
    blk_exp = jnp.minimum(jnp.sum(blk_start[:, None] >= pends[None, :], axis=1), N_EXPERTS - 1).astype(jnp.int32)
    n_used = (pends[-1:] // EXPERT_ROWS).astype(jnp.int32)
    return pstarts, pends, pcounts, blk_exp, n_used, n_blocks * EXPERT_ROWS


def _dest_kernel(route_ref, start_ref, o_ref, *, td):
    route = route_ref[...]
    tb = route.shape[0]
    lane = lax.broadcasted_iota(jnp.int32, (tb, ROUTE_W), 1)
    diagonal = lax.broadcasted_iota(jnp.int32, (td, td), 0) == lax.broadcasted_iota(jnp.int32, (td, td), 1)
    for k in range(TOP_K):
        expert = route[:, k:k + 1].astype(jnp.int32)
        start = jnp.sum(jnp.where(lane == expert, start_ref[...], 0.0), axis=-1, keepdims=True)
        dest = start + route[:, 2 * TOP_K + k:2 * TOP_K + k + 1]
        for j in range(tb // td):
            as_row = jnp.sum(jnp.where(diagonal, dest[j * td:(j + 1) * td, :], 0.0), axis=0, keepdims=True)
            o_ref[j:j + 1, k * td:(k + 1) * td] = as_row.astype(jnp.int32)


def _dest_table(route, pstarts, td):
    n = route.shape[0]
    tb = math.gcd(n, 8 * td)
    starts = jnp.zeros((1, ROUTE_W), F32).at[0, 0:N_EXPERTS].set(pstarts.astype(F32))
    return pl.pallas_call(
        functools.partial(_dest_kernel, td=td),
        grid=(n // tb,),
        in_specs=[pl.BlockSpec((tb, 8), lambda i: (i, 0)), pl.BlockSpec((1, ROUTE_W), lambda i: (0, 0))],
        out_specs=pl.BlockSpec((tb // td, TOP_K * td), lambda i: (i, 0)),
        out_shape=jax.ShapeDtypeStruct((n // td, TOP_K * td), jnp.int32),
        compiler_params=pltpu.CompilerParams(dimension_semantics=("arbitrary",)),
        name="dest_table",
    )(route, starts)


def _final_kernel(dest_hbm, ys_hbm, y_ref, route_ref, w_ref, o_ref, idx_sm, gbuf, sem_idx, sem_g,
                  *, tm, n_steps):
    i = pl.program_id(0)

    def idx_copy(step, slot):
        return pltpu.make_async_copy(dest_hbm.at[step], idx_sm.at[slot], sem_idx.at[slot])

    def gather(slot):
        for r in range(2 * tm):
            pltpu.make_async_copy(ys_hbm.at[pl.ds(idx_sm[slot, r], 1)], gbuf.at[slot, pl.ds(r, 1)],
                                  sem_g.at[slot]).start(priority=r % 2)

    def step(cur):
        nxt = 1 - cur
        if cur == 0:
            @pl.when(i == 0)
            def _():
                idx_copy(0, 0).start()
                idx_copy(0, 0).wait()
                gather(0)
                if n_steps > 1:
                    idx_copy(1, 1).start()

        if n_steps > 1:
            @pl.when(i + 1 < n_steps)
            def _():
                idx_copy(i + 1, nxt).wait()
                gather(nxt)

            if n_steps > 2:
                @pl.when(i + 2 < n_steps)
                def _():
                    idx_copy(i + 2, cur).start()

        pltpu.make_async_copy(ys_hbm.at[pl.ds(0, 2 * tm)], gbuf.at[cur], sem_g.at[cur]).wait()
        route = route_ref[...]
        rows = gbuf[cur]
        y = y_ref[...] + route[:, 2:3] * rows[0:tm] + route[:, 3:4] * rows[tm:]
        o_ref[...] = _rms(y, w_ref[...])

    pl.when(i % 2 == 0)(lambda: step(0))
    pl.when(i % 2 == 1)(lambda: step(1))


def _final(ymid, route, dest_tbl, ys, w, tm):
    n = ymid.shape[0]
    n_steps = n // tm
    return pl.pallas_call(
        functools.partial(_final_kernel, tm=tm, n_steps=n_steps),
        grid=(n_steps,),
        in_specs=[pl.BlockSpec(memory_space=pl.ANY), pl.BlockSpec(memory_space=pl.ANY),
                  pl.BlockSpec((tm, D_MODEL), lambda i: (i, 0)), pl.BlockSpec((tm, 8), lambda i: (i, 0)),
                  pl.BlockSpec((1, D_MODEL), lambda i: (0, 0))],
        out_specs=pl.BlockSpec((tm, D_MODEL), lambda i: (i, 0)),
        out_shape=jax.ShapeDtypeStruct((n, D_MODEL), F32),
        scratch_shapes=[pltpu.SMEM((2, 2 * tm), jnp.int32), pltpu.VMEM((2, 2 * tm, D_MODEL), F32),
                        pltpu.SemaphoreType.DMA((2,)), pltpu.SemaphoreType.DMA((2,))],
        compiler_params=pltpu.CompilerParams(dimension_semantics=("arbitrary",),
                                             vmem_limit_bytes=VMEM_LIMIT),
        name="final",
    )(dest_tbl, ys, ymid, route, w)


def kernel(x_prompt, x_sample, cache_k, cache_v, state_rec, w_in, lam_q1, lam_k1, lam_q2, lam_k2,
           subln_w, lb_param, gnorm_w, w_out, norm1_w, norm2_w, w_group, w_router,
           w_e_gate, w_e_up, w_e_down, final_w):
    assert w_in.shape[0] == 1 and lb_param.shape[0] == 2, "single-layer model"
    bp, sp, _ = x_prompt.shape
    bs, ts, _ = x_sample.shape
    past = cache_k.shape[2]
    assert sp % CHUNK == 0 and past % CHUNK == 0 and ts <= CHUNK and ts & (ts - 1) == 0
    n_p, n_s = bp * sp, bs * ts
    n_all = n_p + n_s
    tm = 512 if (n_p % 512 == 0 and n_s % 512 == 0) else math.gcd(n_p, n_s)

    w_in_bf = w_in[0].astype(BF16)
    w_out_bf = w_out[0].astype(BF16)
    n1 = norm1_w[0].reshape(1, D_MODEL)
    n2 = norm2_w[0].reshape(1, D_MODEL)
    lam_vecs = jnp.stack([lam_q1[0], lam_k1[0], lam_q2[0], lam_k2[0]]).astype(F32)
    sub_w = subln_w[0].reshape(1, HEAD_W)
    gn_w = gnorm_w[0].reshape(1, HEAD_W)
    w_route = jnp.zeros((D_MODEL, ROUTE_W), F32)
    w_route = w_route.at[:, 0:N_GROUPS].set(w_group[0]).at[:, N_GROUPS:N_GROUPS + N_EXPERTS].set(w_router[0])
    wr_hi = w_route.astype(BF16)
    wr_lo = (w_route - wr_hi.astype(F32)).astype(BF16)

    xp = x_prompt.reshape(n_p, D_MODEL)
    xs = x_sample.reshape(n_s, D_MODEL)

    kf, vf, kb, qr, logf, kr, ir, gr, qt, vt = _inproj(xp, n1, w_in_bf, lb_param, tm, True)
    a_p = _attn_prompt(qt, kb, vt, lam_vecs, sub_w.reshape(HEAD_W, 1), bp, sp)
    chunks = 8 if sp % (8 * CHUNK) == 0 else 1
    r_p, state_p = _hgrn(qr, logf, kr, ir, gr, gn_w, None, bp, sp, CHUNK, chunks)
    ymid_p, h2_p, route_p, counts_p = _merge(xp, a_p, r_p, w_out_bf, n2, wr_hi, wr_lo,
                                             jnp.zeros((1, ROUTE_W), F32), tm)
    k_prompt = kf.reshape(1, bp, sp, A_HEADS, HEAD_W)
    v_prompt = vf.reshape(1, bp, sp, A_HEADS, HEAD_W)

    kf, vf, kb, qr, logf, kr, ir, gr, q, vb = _inproj(xs, n1, w_in_bf, lb_param, tm, False)
    cache_rows = (bs, past * A_HEADS, HEAD_W)
    a_s = _attn_sample(q, cache_k.reshape(cache_rows), cache_v.reshape(cache_rows), kb, vb,
                       lam_vecs, sub_w, bs, ts)
    r_s, state_s = _hgrn(qr, logf, kr, ir, gr, gn_w, state_rec[0], bs, ts, ts, 1)
    ymid_s, h2_s, route_s, counts = _merge(xs, a_s, r_s, w_out_bf, n2, wr_hi, wr_lo, counts_p, tm)
    k_sample = kf.reshape(1, bs, ts, A_HEADS, HEAD_W)
    v_sample = vf.reshape(1, bs, ts, A_HEADS, HEAD_W)

    pstarts, pends, pcounts, blk_exp, n_used, n_rows = _expert_layout(counts, n_all)
    td = min(256, math.gcd(n_p, n_s))
    dest_p = _dest_table(route_p, pstarts, td)
    dest_s = _dest_table(route_s, pstarts, td)
    xs_rows = _dispatch(h2_p, h2_s, jnp.concatenate([dest_p, dest_s], axis=0), pends, pcounts, n_used,
                        n_rows, td)
    ys_rows = _experts(xs_rows, blk_exp, n_used, w_e_gate[0], w_e_up[0], w_e_down[0])

    fw = final_w.reshape(1, D_MODEL)
    y_prompt = _final(ymid_p, route_p, dest_p, ys_rows, fw, td).reshape(bp, sp, D_MODEL)
    y_sample = _final(ymid_s, route_s, dest_s, ys_rows, fw, td).reshape(bs, ts, D_MODEL)
    return (y_prompt, y_sample, k_prompt, v_prompt, state_p[None], k_sample, v_sample, state_s[None])
```

```python
import functools
import math

import numpy as np
import jax
import jax.numpy as jnp
from jax import lax
from jax.experimental import pallas as pl
from jax.experimental.pallas import tpu as pltpu

F32 = jnp.float32
BF16 = jnp.bfloat16

D_MODEL = 1024
RMS_EPS = 1e-6
CHUNK = 64
A_HEADS = 4
A_HEAD_DIM = 64
HEAD_W = 2 * A_HEAD_DIM
KEY_BLOCK = 256
VT_ROWS = HEAD_W + 16
LOG2_E = math.log2(math.e)
R_HEADS = 4
SEG_W = 512
N_SEG = 7
N_GROUPS = 4
EXPERTS_PER_GROUP = 8
N_EXPERTS = N_GROUPS * EXPERTS_PER_GROUP
TOP_K = 2
D_EXPERT = 512
EXPERT_ROWS = 256
ROUTE_W = 128
LAM_INIT = 0.8 - 0.6 * math.exp(-0.3 * 0)
VMEM_LIMIT = 56 * 1024 * 1024


def _sigmoid(x):
    return 1.0 / (1.0 + jnp.exp(-x))


def _dot(a, b):
    return jnp.dot(a, b, preferred_element_type=F32)


def _dot_nt(a, b):
    return lax.dot_general(a, b, (((1,), (1,)), ((), ())), preferred_element_type=F32)


def _dot_tn(a, b):
    return lax.dot_general(a, b, (((0,), (0,)), ((), ())), preferred_element_type=F32)


def _rms(x, w):
    return x * lax.rsqrt(jnp.mean(x * x, axis=-1, keepdims=True) + RMS_EPS) * w


def _inproj_kernel(x_ref, n1_ref, w_ref, lbp_ref, kf_ref, vf_ref, kb_ref, qr_ref, logf_ref, kr_ref,
                   ir_ref, gr_ref, qa_ref, va_ref, h_sc, *, transposed_qv):
    p = lbp_ref[...]
    e = jnp.exp(p - jnp.max(p, axis=0, keepdims=True))
    lb = e[0:1] / jnp.sum(e, axis=0, keepdims=True)
    ones = jnp.ones((VT_ROWS - HEAD_W, KEY_BLOCK), BF16)

    sub = min(KEY_BLOCK, x_ref.shape[0])
    for s in range(x_ref.shape[0] // sub):
        rows = slice(s * sub, (s + 1) * sub)
        h_sc[rows, :] = _rms(x_ref[rows, :], n1_ref[...]).astype(BF16)

        def seg(i, rows=rows):
            return _dot(h_sc[rows, :], w_ref[:, i * SEG_W:(i + 1) * SEG_W])

        q = seg(0) * (A_HEAD_DIM ** -0.5)
        k = seg(1)
        for hd in range(A_HEADS):
            kf_ref[rows, hd, :] = k[:, hd * HEAD_W:(hd + 1) * HEAD_W]
        kb_ref[rows, :] = k.astype(BF16)
        v = seg(2)
        for hd in range(A_HEADS):
            vf_ref[rows, hd, :] = v[:, hd * HEAD_W:(hd + 1) * HEAD_W]
        if transposed_qv:
            qa_ref[s] = (q * LOG2_E).T.astype(BF16)
            vt = v.T.astype(BF16)
            for hd in range(A_HEADS):
                va_ref[s, hd * VT_ROWS:hd * VT_ROWS + HEAD_W, :] = vt[hd * HEAD_W:(hd + 1) * HEAD_W, :]
                va_ref[s, hd * VT_ROWS + HEAD_W:(hd + 1) * VT_ROWS, :] = ones
        else:
            qa_ref[rows, :] = q.astype(BF16)
            va_ref[rows, :] = v.astype(BF16)
        qr = seg(3)
        qr_ref[rows, :] = (qr * _sigmoid(qr)).astype(BF16)
        f = lb + (1.0 - lb) * _sigmoid(seg(4))
        logf_ref[rows, :] = jnp.log(f) * LOG2_E
        kr_ref[rows, :] = (1.0 - f).astype(BF16)
        ir_ref[rows, :] = seg(5).astype(BF16)
        gr_ref[rows, :] = seg(6)


def _inproj(x, n1, w_bf, lbp, tm, transposed_qv):
    n = x.shape[0]
    row = lambda i: (i, 0)
    fix = lambda i: (0, 0)
    out = lambda dt: jax.ShapeDtypeStruct((n, SEG_W), dt)
    ospec = pl.BlockSpec((tm, SEG_W), row)
    hspec = pl.BlockSpec((tm, A_HEADS, HEAD_W), lambda i: (i, 0, 0))
    hout = jax.ShapeDtypeStruct((n, A_HEADS, HEAD_W), F32)
    slabs = tm // KEY_BLOCK
    tspec = lambda rows: pl.BlockSpec((slabs, rows, KEY_BLOCK), lambda i: (i, 0, 0))
    tout = lambda rows: jax.ShapeDtypeStruct((n // KEY_BLOCK, rows, KEY_BLOCK), BF16)
    if transposed_qv:
        qv_specs = [tspec(SEG_W), tspec(A_HEADS * VT_ROWS)]
        qv_shapes = [tout(SEG_W), tout(A_HEADS * VT_ROWS)]
    else:
        qv_specs = [ospec, ospec]
        qv_shapes = [out(BF16), out(BF16)]
    return pl.pallas_call(
        functools.partial(_inproj_kernel, transposed_qv=transposed_qv),
        grid=(n // tm,),
        in_specs=[pl.BlockSpec((tm, D_MODEL), row), pl.BlockSpec((1, D_MODEL), fix),
                  pl.BlockSpec((D_MODEL, N_SEG * SEG_W), fix), pl.BlockSpec(lbp.shape, fix)],
        out_specs=[hspec, hspec] + [ospec] * 6 + qv_specs,
        out_shape=[hout, hout, out(BF16), out(BF16), out(F32), out(BF16), out(BF16), out(F32)] + qv_shapes,
        scratch_shapes=[pltpu.VMEM((tm, D_MODEL), BF16)],
        compiler_params=pltpu.CompilerParams(dimension_semantics=("arbitrary",),
                                             vmem_limit_bytes=VMEM_LIMIT),
        name="inproj",
    )(x, n1, w_bf, lbp)


def _lam_value(lam_ref):
    l = lam_ref[...]
    s1 = jnp.sum(l[0:1] * l[1:2], axis=-1, keepdims=True)
    s2 = jnp.sum(l[2:3] * l[3:4], axis=-1, keepdims=True)
    return jnp.exp(s1) - jnp.exp(s2) + LAM_INIT


def _split_maps(q):
    lane = lax.broadcasted_iota(jnp.int32, q.shape, 1)
    zero = jnp.zeros_like(q)
    return jnp.concatenate([jnp.where(lane < A_HEAD_DIM, q, zero),
                            jnp.where(lane >= A_HEAD_DIM, q, zero)], axis=0)


def _attn_finish(o_num, l, lam, sub_w, tq):
    o = o_num[0:tq] / l[0:tq] - lam * (o_num[tq:] / l[tq:])
    return (_rms(o, sub_w) * (1.0 - LAM_INIT)).astype(BF16)


def _attn_prompt_kernel(lam_ref, sub_ref, bias_ref, qt_ref, k_ref, vt_ref, o_ref,
                        qbd_sc, m_sc, acc_sc, sa_sc, sb_sc, *, qblocks):
    i = pl.program_id(2)
    kb = KEY_BLOCK

    sub = lax.broadcasted_iota(jnp.int32, (HEAD_W, kb), 0)
    for r in range(2 * qblocks):
        qt = qt_ref[r % qblocks]
        keep = (sub < A_HEAD_DIM) if r < qblocks else (sub >= A_HEAD_DIM)
        qbd_sc[r] = jnp.where(keep, qt, jnp.zeros_like(qt))
    m_sc[...] = jnp.full(m_sc.shape, -jnp.inf, F32)
    acc_sc[...] = jnp.zeros(acc_sc.shape, F32)

    def scores(r, key_block):
        return _dot(k_ref[pl.ds(pl.multiple_of(key_block * kb, kb), kb), :], qbd_sc[r])

    def update(r, key_block, st):
        m_old = m_sc[r:r + 1, :]
        m_new = jnp.maximum(m_old, jnp.max(st, axis=0, keepdims=True))
        p = jnp.exp2(st - m_new).astype(BF16)
        acc_sc[r] = jnp.exp2(m_old - m_new) * acc_sc[r] + _dot(vt_ref[key_block], p)
        m_sc[r:r + 1, :] = m_new

    n_q = 2 * qblocks
    first = i * qblocks
    for r in range(n_q):
        sa_sc[r] = scores(r, 0)

    def visible_blocks(start, count):
        lead = 2
        chains = [(j, r) for j in range(count) for r in range(n_q)]
        for c in range(len(chains) + lead):
            if c < len(chains):
                j, r = chains[c]
                (sb_sc if j % 2 == 0 else sa_sc)[r] = scores(r, start + j + 1)
            if c >= lead:
                j, r = chains[c - lead]
                update(r, start + j, (sa_sc if j % 2 == 0 else sb_sc)[r])

    def tile_pair(t, carry):
        visible_blocks(2 * t * qblocks, 2 * qblocks)
        return carry

    lax.fori_loop(0, i // 2, tile_pair, 0)
    pl.when(i % 2 == 1)(lambda: visible_blocks((i - 1) * qblocks, qblocks))

    work = [(kl, r) for kl in range(qblocks) for r in range(n_q) if r % qblocks >= kl]
    ahead = 4
    pending = {}
    for n, (kl, r) in enumerate(work):
        for kl2, r2 in work[n:n + 1 + ahead]:
            if kl2 > 0 and (kl2, r2) not in pending:
                pending[(kl2, r2)] = scores(r2, first + kl2)
        st = sa_sc[r] if kl == 0 else pending.pop((kl, r))
        update(r, first + kl, st + bias_ref[...] if r % qblocks == kl else st)

    lam = _lam_value(lam_ref)
    for ql in range(qblocks):
        a1 = acc_sc[ql]
        a2 = acc_sc[qblocks + ql]
        ot = a1[0:HEAD_W] / a1[HEAD_W:HEAD_W + 1] - lam * (a2[0:HEAD_W] / a2[HEAD_W:HEAD_W + 1])
        ot = ot * lax.rsqrt(jnp.mean(ot * ot, axis=0, keepdims=True) + RMS_EPS) * sub_ref[...]
        o_ref[ql * kb:(ql + 1) * kb, :] = (ot * (1.0 - LAM_INIT)).T.astype(BF16)


def _attn_prompt(qt, k, vt, lam_vecs, sub_col, batch, seq):
    kb = KEY_BLOCK
    tq = min(1024, seq)
    qblocks = tq // kb
    assert qblocks % 2 == 0, "the key-block loop is unrolled by two"
    nq = seq // tq
    pos = np.arange(kb) // CHUNK
    bias = jnp.asarray(np.where(pos[:, None] <= pos[None, :], 0.0, -np.inf), F32)
    fix = lambda b, h, i: (0, 0)
    return pl.pallas_call(
        functools.partial(_attn_prompt_kernel, qblocks=qblocks),
        grid=(batch, A_HEADS, nq),
        in_specs=[pl.BlockSpec((4, A_HEAD_DIM), fix), pl.BlockSpec((HEAD_W, 1), fix),
                  pl.BlockSpec((kb, kb), fix),
                  pl.BlockSpec((qblocks, HEAD_W, kb), lambda b, h, i: (b * nq + i, h, 0)),
                  pl.BlockSpec((seq, HEAD_W), lambda b, h, i: (b, h)),
                  pl.BlockSpec((seq // kb, VT_ROWS, kb), lambda b, h, i: (b, h, 0))],
        out_specs=pl.BlockSpec((tq, HEAD_W), lambda b, h, i: (b * nq + i, h)),
        out_shape=jax.ShapeDtypeStruct(k.shape, BF16),
        scratch_shapes=[pltpu.VMEM((2 * qblocks, HEAD_W, kb), BF16), pltpu.VMEM((2 * qblocks, kb), F32),
                        pltpu.VMEM((2 * qblocks, VT_ROWS, kb), F32),
                        pltpu.VMEM((2 * qblocks, kb, kb), F32), pltpu.VMEM((2 * qblocks, kb, kb), F32)],
        compiler_params=pltpu.CompilerParams(
            dimension_semantics=("arbitrary", "arbitrary", "arbitrary"), vmem_limit_bytes=VMEM_LIMIT),
        name="attn_prompt",
    )(lam_vecs, sub_col, bias, qt, k, vt)


def _attn_sample_kernel(lam_ref, sub_ref, bias_c_ref, bias_n_ref, q_ref, kc_ref, vc_ref, kn_ref, vn_ref,
                        o_ref, *, t):
    lam = _lam_value(lam_ref)
    heads = [slice(h * HEAD_W, (h + 1) * HEAD_W) for h in range(A_HEADS)]
    qbd = jnp.concatenate([_split_maps(q_ref[:, cs]) for cs in heads], axis=0)
    keys = [kc_ref[0].astype(BF16), jnp.concatenate([kn_ref[:, cs] for cs in heads], axis=0)]
    vals = [vc_ref[0].astype(BF16), jnp.concatenate([vn_ref[:, cs] for cs in heads], axis=0)]
    scores = [_dot_nt(qbd, k) + b[...] for k, b in zip(keys, (bias_c_ref, bias_n_ref))]
    m = functools.reduce(jnp.maximum, [jnp.max(sc, axis=-1, keepdims=True) for sc in scores])
    probs = [jnp.exp(sc - m) for sc in scores]
    l = sum(jnp.sum(p, axis=-1, keepdims=True) for p in probs)
    o_num = sum(_dot(p.astype(BF16), v) for p, v in zip(probs, vals))
    for h, cs in enumerate(heads):
        rows = slice(h * 2 * t, (h + 1) * 2 * t)
        o_ref[:, cs] = _attn_finish(o_num[rows], l[rows], lam, sub_ref[...], t)


def _attn_sample(q, cache_k, cache_v, k_new, v_new, lam_vecs, sub_w, batch, t):
    fix = lambda b: (0, 0)
    row = lambda b: (b, 0)
    cache = pl.BlockSpec((1,) + cache_k.shape[1:], lambda b: (b, 0, 0))
    q_head = np.arange(A_HEADS * 2 * t) // (2 * t)
    off = lambda key_head: jnp.asarray(np.where(q_head[:, None] == key_head[None, :], 0.0, -np.inf), F32)
    bias_c = off(np.arange(cache_k.shape[1]) % A_HEADS)
    bias_n = off(np.arange(A_HEADS * t) // t)
    return pl.pallas_call(
        functools.partial(_attn_sample_kernel, t=t),
        grid=(batch,),
        in_specs=[pl.BlockSpec((4, A_HEAD_DIM), fix), pl.BlockSpec((1, HEAD_W), fix),
                  pl.BlockSpec(bias_c.shape, fix), pl.BlockSpec(bias_n.shape, fix),
                  pl.BlockSpec((t, SEG_W), row),
                  cache, cache, pl.BlockSpec((t, SEG_W), row), pl.BlockSpec((t, SEG_W), row)],
        out_specs=pl.BlockSpec((t, SEG_W), row),
        out_shape=jax.ShapeDtypeStruct(q.shape, BF16),
        compiler_params=pltpu.CompilerParams(dimension_semantics=("arbitrary",),
                                             vmem_limit_bytes=VMEM_LIMIT),
        name="attn_sample",
    )(lam_vecs, sub_w, bias_c, bias_n, q, cache_k, cache_v, k_new, v_new)


def _hgrn_consts(length):
    t = np.arange(length)[:, None]
    j = np.arange(length)[None, :]
    sums = [j <= t]
    masks = [j == t]
    blk = length
    while blk >= 2:
        half = blk // 2
        mid_t = (t // blk) * blk + half
        mid_j = (j // blk) * blk + half
        sums.append(np.where(t >= mid_t, (j >= mid_t) & (j <= t), (j > t) & (j < mid_t)))
        masks.append((t // blk == j // blk) & (t >= mid_t) & (j < mid_j))
        blk = half
    sums = np.tile(np.concatenate(sums, axis=0).astype(np.float32), (1, 2))
    masks = np.concatenate(masks, axis=0).astype(np.float32)
    return jnp.asarray(sums, BF16), jnp.asarray(masks, F32)


def _hgrn_kernel(*refs, length, chunks, has_state_in):
    if has_state_in:
        (sums_ref, masks_ref, gn_ref, q_ref, logf_ref, k_ref, v_ref, g_ref, s0_ref,
         r_ref, sout_ref, st_sc) = refs
    else:
        (sums_ref, masks_ref, gn_ref, q_ref, logf_ref, k_ref, v_ref, g_ref,
         r_ref, sout_ref, st_sc) = refs
    step = pl.program_id(1)
    levels = int(math.log2(length))
    L = length
    group = math.gcd(chunks, 8)

    @pl.when(step == 0)
    def _():
        for h in range(R_HEADS):
            if has_state_in:
                st_sc[h] = s0_ref[0, h].T
            else:
                st_sc[h] = jnp.zeros(st_sc.shape[1:], F32)

    heads = [slice(h * HEAD_W, (h + 1) * HEAD_W) for h in range(R_HEADS)]

    def chunk_group(c, carry):
        rows = [pl.ds(pl.multiple_of((c * group + g) * L, L), L) for g in range(group)]
        expo = []
        for g in range(group):
            logf = logf_ref[rows[g], :]
            hi = logf.astype(BF16)
            lo = (logf - hi.astype(F32)).astype(BF16)
            expo.append(_dot(sums_ref[...], jnp.concatenate([hi, lo], axis=0)))
        streams = [(g, h) for g in range(group) for h in range(R_HEADS)]
        intra, q_dec, upd = {}, {}, {}
        for g, h in streams:
            cs = heads[h]
            q = q_ref[rows[g], cs]
            k = k_ref[rows[g], cs]
            v = v_ref[rows[g], cs]
            a = masks_ref[0:L, :] * _dot_nt(q, k)
            for lv in range(levels):
                x = jnp.exp2(expo[g][(1 + lv) * L:(2 + lv) * L, cs]).astype(BF16)
                a = a + masks_ref[(1 + lv) * L:(2 + lv) * L, :] * _dot_nt(q * x, k * x)
            intra[g, h] = _dot(a.astype(BF16), v)
            b = expo[g][0:L, cs]
            q_dec[g, h] = q * jnp.exp2(b).astype(BF16)
            k_dec = k * jnp.exp2(b[L - 1:L, :] - b).astype(BF16)
            upd[g, h] = _dot_tn(v, k_dec)
        for g, h in streams:
            cs = heads[h]
            st = st_sc[h]
            o = intra[g, h] + _dot_nt(q_dec[g, h], st.astype(BF16))
            st_sc[h] = st * jnp.exp2(expo[g][L - 1:L, cs]) + upd[g, h]
            gate = g_ref[rows[g], cs]
            r_ref[rows[g], cs] = (_rms(o, gn_ref[...]) * (gate * _sigmoid(gate))).astype(BF16)
        return carry

    lax.fori_loop(0, chunks // group, chunk_group, 0)

    @pl.when(step == pl.num_programs(1) - 1)
    def _():
        for h in range(R_HEADS):
            sout_ref[0, h] = st_sc[h].T


def _hgrn(q, logf, k, v, g, gn_w, state_in, batch, seq, length, chunks):
    sums, masks = _hgrn_consts(length)
    tm = length * chunks
    steps = seq // tm
    fix = lambda b, s: (0, 0)
    row = lambda b, s: (b * steps + s, 0)
    tok = pl.BlockSpec((tm, SEG_W), row)
    in_specs = [pl.BlockSpec(sums.shape, fix), pl.BlockSpec(masks.shape, fix), pl.BlockSpec((1, HEAD_W), fix),
                tok, tok, tok, tok, tok]
    args = [sums, masks, gn_w, q, logf, k, v, g]
    state_spec = pl.BlockSpec((1, R_HEADS, HEAD_W, HEAD_W), lambda b, s: (b, 0, 0, 0))
    if state_in is not None:
        in_specs.append(state_spec)
        args.append(state_in)
    return pl.pallas_call(
        functools.partial(_hgrn_kernel, length=length, chunks=chunks, has_state_in=state_in is not None),
        grid=(batch, steps),
        in_specs=in_specs,
        out_specs=[tok, state_spec],
        out_shape=[jax.ShapeDtypeStruct(q.shape, BF16),
                   jax.ShapeDtypeStruct((batch, R_HEADS, HEAD_W, HEAD_W), F32)],
        scratch_shapes=[pltpu.VMEM((R_HEADS, HEAD_W, HEAD_W), F32)],
        compiler_params=pltpu.CompilerParams(dimension_semantics=("arbitrary", "arbitrary"),
                                             vmem_limit_bytes=VMEM_LIMIT),
        name="hgrn",
    )(*args)


def _merge_kernel(x_ref, a_ref, r_ref, wo_ref, n2_ref, wr_ref, tri_ref, cnt0_ref,
                  ymid_ref, h2_ref, route_ref, cnt_ref, cnt_sc):
    step = pl.program_id(0)

    @pl.when(step == 0)
    def _():
        cnt_sc[...] = cnt0_ref[...]

    sub = tri_ref.shape[0]
    for s in range(x_ref.shape[0] // sub):
        _merge_rows(slice(s * sub, (s + 1) * sub), x_ref, a_ref, r_ref, wo_ref, n2_ref, wr_ref,
                    tri_ref, ymid_ref, h2_ref, route_ref, cnt_sc)
    cnt_ref[...] = cnt_sc[...]


def _merge_rows(rows, x_ref, a_ref, r_ref, wo_ref, n2_ref, wr_ref, tri_ref,
                ymid_ref, h2_ref, route_ref, cnt_sc):
    half = a_ref.shape[1]
    y = x_ref[rows, :] + _dot(a_ref[rows, :], wo_ref[0:half, :]) + _dot(r_ref[rows, :], wo_ref[half:, :])
    ymid_ref[rows, :] = y
    hn = _rms(y, n2_ref[...])
    h2_ref[rows, :] = hn
    hi = hn.astype(BF16)
    lo = (hn - hi.astype(F32)).astype(BF16)
    logits = (_dot(hi, wr_ref[:, :ROUTE_W]) + _dot(lo, wr_ref[:, :ROUTE_W])) + _dot(hi, wr_ref[:, ROUTE_W:])
    lane = lax.broadcasted_iota(jnp.int32, logits.shape, 1)
    neg = -jnp.inf
    big = jnp.int32(ROUTE_W)

    def top1(mask):
        val = jnp.max(jnp.where(mask, logits, neg), axis=-1, keepdims=True)
        idx = jnp.min(jnp.where(mask & (logits == val), lane, big), axis=-1, keepdims=True)
        return val, idx

    is_group = lane < N_GROUPS
    g_max, g_sel = top1(is_group)
    p_group = 1.0 / jnp.sum(jnp.where(is_group, jnp.exp(logits - g_max), 0.0), axis=-1, keepdims=True)
    first = N_GROUPS + g_sel * EXPERTS_PER_GROUP
    in_group = (lane >= first) & (lane < first + EXPERTS_PER_GROUP)
    v1, i1 = top1(in_group)
    v2, i2 = top1(in_group & (lane != i1))
    e2 = jnp.exp(v2 - v1)
    gate1 = p_group * (1.0 / (1.0 + e2))
    gate2 = p_group * (e2 / (1.0 + e2))
    hot1 = lane == i1
    hot2 = lane == i2
    hot = jnp.where(hot1 | hot2, 1.0, 0.0)
    before = cnt_sc[...] + _dot(tri_ref[...], hot.astype(BF16))
    rank1 = jnp.sum(jnp.where(hot1, before, 0.0), axis=-1, keepdims=True)
    rank2 = jnp.sum(jnp.where(hot2, before, 0.0), axis=-1, keepdims=True)
    cnt_sc[...] = cnt_sc[...] + jnp.sum(hot, axis=0, keepdims=True)
    col = lax.broadcasted_iota(jnp.int32, (hot.shape[0], route_ref.shape[1]), 1)
    cols = [(i1 - N_GROUPS).astype(F32), (i2 - N_GROUPS).astype(F32), gate1, gate2, rank1, rank2]
    route = jnp.zeros(col.shape, F32)
    for c, val in enumerate(cols):
        route = jnp.where(col == c, val, route)
    route_ref[rows, :] = route


def _merge(x, a, r, wo_bf, n2, wr, counts_in, tm):
    n = x.shape[0]
    row = lambda i: (i, 0)
    fix = lambda i: (0, 0)
    sub = min(256, tm)
    tri = jnp.asarray(np.tril(np.ones((sub, sub), np.float32), -1), BF16)
    return pl.pallas_call(
        _merge_kernel,
        grid=(n // tm,),
        in_specs=[pl.BlockSpec((tm, D_MODEL), row), pl.BlockSpec((tm, SEG_W), row),
                  pl.BlockSpec((tm, SEG_W), row), pl.BlockSpec((D_MODEL, D_MODEL), fix),
                  pl.BlockSpec((1, D_MODEL), fix), pl.BlockSpec((D_MODEL, 2 * ROUTE_W), fix),
                  pl.BlockSpec((sub, sub), fix),
                  pl.BlockSpec((1, ROUTE_W), fix)],
        out_specs=[pl.BlockSpec((tm, D_MODEL), row), pl.BlockSpec((tm, D_MODEL), row),
                   pl.BlockSpec((tm, 8), row), pl.BlockSpec((1, ROUTE_W), fix)],
        out_shape=[jax.ShapeDtypeStruct((n, D_MODEL), F32), jax.ShapeDtypeStruct((n, D_MODEL), F32),
                   jax.ShapeDtypeStruct((n, 8), F32), jax.ShapeDtypeStruct((1, ROUTE_W), F32)],
        scratch_shapes=[pltpu.VMEM((1, ROUTE_W), F32)],
        compiler_params=pltpu.CompilerParams(dimension_semantics=("arbitrary",),
                                             vmem_limit_bytes=VMEM_LIMIT),
        name="merge",
    )(x, a, r, wo_bf, n2, wr, tri, counts_in)


def _dispatch_kernel(pend_ref, pcnt_ref, n_used_ref, dest_hbm, hp_ref, hs_ref, xs_hbm,
                     idx_sm, zero_sc, sem_idx, sem_out, *, tm, steps_p, steps_s, first_spare, n_blocks):
    i = pl.program_id(0)
    slot = i % 2
    n_steps = steps_p + steps_s

    def idx_copy(step, s):
        return pltpu.make_async_copy(dest_hbm.at[step], idx_sm.at[s], sem_idx.at[s])

    @pl.when(i == 0)
    def _():
        idx_copy(0, 0).start()
        zero_sc[...] = jnp.zeros(zero_sc.shape, F32)

        def fill(start):
            start = pl.multiple_of(start, EXPERT_ROWS)
            return pltpu.make_async_copy(zero_sc, xs_hbm.at[pl.ds(start, EXPERT_ROWS)], sem_out)

        for action in ("start", "wait"):
            for e in range(N_EXPERTS):
                pl.when(pcnt_ref[e] > 0)(
                    lambda e=e: getattr(fill(pend_ref[e] - EXPERT_ROWS), action)())
            for blk in range(first_spare, n_blocks):
                pl.when(blk >= n_used_ref[0])(
                    lambda blk=blk: getattr(fill(blk * EXPERT_ROWS), action)())

    idx_copy(i, slot).wait()

    @pl.when(i + 1 < n_steps)
    def _():
        idx_copy(i + 1, 1 - slot).start()

    def scatter(h_ref, idx_slot):
        for r in range(2 * tm):
            pltpu.make_async_copy(h_ref.at[pl.ds(r % tm, 1)], xs_hbm.at[pl.ds(idx_sm[idx_slot, r], 1)],
                                  sem_out).start(priority=r % 2)
        for _ in range(2):
            pltpu.make_async_copy(h_ref, xs_hbm.at[pl.ds(0, tm)], sem_out).wait()

    for parity in range(2):
        pl.when((i < steps_p) & (slot == parity))(lambda p=parity: scatter(hp_ref, p))
        pl.when((i >= steps_p) & (slot == parity))(lambda p=parity: scatter(hs_ref, p))


def _dispatch(h2_p, h2_s, dest_tbl, pends, pcounts, n_used, n_rows, tm):
    steps_p = h2_p.shape[0] // tm
    steps_s = h2_s.shape[0] // tm
    n_blocks = n_rows // EXPERT_ROWS
    first_spare = (h2_p.shape[0] + h2_s.shape[0]) * TOP_K // EXPERT_ROWS
    grid_spec = pltpu.PrefetchScalarGridSpec(
        num_scalar_prefetch=3, grid=(steps_p + steps_s,),
        in_specs=[pl.BlockSpec(memory_space=pl.ANY),
                  pl.BlockSpec((tm, D_MODEL), lambda i, pe, pc, nu: (jnp.minimum(i, steps_p - 1), 0)),
                  pl.BlockSpec((tm, D_MODEL), lambda i, pe, pc, nu: (jnp.maximum(i - steps_p, 0), 0))],
        out_specs=pl.BlockSpec(memory_space=pl.ANY),
        scratch_shapes=[pltpu.SMEM((2, 2 * tm), jnp.int32), pltpu.VMEM((EXPERT_ROWS, D_MODEL), F32),
                        pltpu.SemaphoreType.DMA((2,)), pltpu.SemaphoreType.DMA])
    return pl.pallas_call(
        functools.partial(_dispatch_kernel, tm=tm, steps_p=steps_p, steps_s=steps_s,
                          first_spare=first_spare, n_blocks=n_blocks),
        grid_spec=grid_spec,
        out_shape=jax.ShapeDtypeStruct((n_rows, D_MODEL), F32),
        compiler_params=pltpu.CompilerParams(dimension_semantics=("arbitrary",),
                                             vmem_limit_bytes=VMEM_LIMIT),
        name="dispatch",
    )(pends, pcounts, n_used, dest_tbl, h2_p, h2_s)


def _experts_kernel(blk_exp_ref, n_used_ref, x_ref, wg_ref, wu_ref, wd_ref, y_ref, wg_sc, wu_sc, wd_sc):
    i = pl.program_id(0)

    @pl.when(i < n_used_ref[0])
    def _():
        prev = blk_exp_ref[jnp.maximum(i - 1, 0)]

        @pl.when((i == 0) | (prev != blk_exp_ref[i]))
        def _():
            wg_sc[...] = wg_ref[0].astype(BF16)
            wu_sc[...] = wu_ref[0].astype(BF16)
            wd_sc[...] = wd_ref[0].astype(BF16)

        xb = x_ref[...].astype(BF16)
        gate = _dot(xb, wg_sc[...])
        up = _dot(xb, wu_sc[...])
        act = (gate * _sigmoid(gate) * up).astype(BF16)
        y_ref[...] = _dot(act, wd_sc[...])

    @pl.when(i >= n_used_ref[0])
    def _():
        y_ref[...] = jnp.zeros(y_ref.shape, F32)


def _experts(xs, blk_exp, n_used, wg, wu, wd):
    n_blocks = int(blk_exp.shape[0])
    wmap = lambda i, be, nu: (be[i], 0, 0)
    xmap = lambda i, be, nu: (jnp.minimum(i, nu[0] - 1), 0)
    grid_spec = pltpu.PrefetchScalarGridSpec(
        num_scalar_prefetch=2,
        grid=(n_blocks,),
        in_specs=[pl.BlockSpec((EXPERT_ROWS, D_MODEL), xmap),
                  pl.BlockSpec((1, D_MODEL, D_EXPERT), wmap),
                  pl.BlockSpec((1, D_MODEL, D_EXPERT), wmap),
                  pl.BlockSpec((1, D_EXPERT, D_MODEL), wmap)],
        out_specs=pl.BlockSpec((EXPERT_ROWS, D_MODEL), lambda i, be, nu: (i, 0)),
        scratch_shapes=[pltpu.VMEM((D_MODEL, D_EXPERT), BF16), pltpu.VMEM((D_MODEL, D_EXPERT), BF16),
                        pltpu.VMEM((D_EXPERT, D_MODEL), BF16)],
    )
    return pl.pallas_call(
        _experts_kernel,
        grid_spec=grid_spec,
        out_shape=jax.ShapeDtypeStruct(xs.shape, F32),
        compiler_params=pltpu.CompilerParams(dimension_semantics=("arbitrary",),
                                             vmem_limit_bytes=VMEM_LIMIT),
        name="experts",
    )(blk_exp, n_used, xs, wg, wu, wd)


def _expert_layout(counts_row, n_tok):
    counts = counts_row[0, N_GROUPS:N_GROUPS + N_EXPERTS].astype(jnp.int32)
    pcounts = (counts + EXPERT_ROWS - 1) // EXPERT_ROWS * EXPERT_ROWS
    pends = jnp.cumsum(pcounts).astype(jnp.int32)
    pstarts = pends - pcounts
    n_blocks = -(-(n_tok * TOP_K) // EXPERT_ROWS) + N_EXPERTS
    blk_start = jnp.arange(n_blocks, dtype=jnp.int32) * EXPERT_ROWS
    blk_exp = jnp.minimum(jnp.sum(blk_start[:, None] >= pends[None, :], axis=1), N_EXPERTS - 1).astype(jnp.int32)
    n_used = (pends[-1:] // EXPERT_ROWS).astype(jnp.int32)
    return pstarts, pends, pcounts, blk_exp, n_used, n_blocks * EXPERT_ROWS


def _dest_kernel(route_ref, start_ref, o_ref, *, td):
    route = route_ref[...]
    tb = route.shape[0]
    lane = lax.broadcasted_iota(jnp.int32, (tb, ROUTE_W), 1)
    diagonal = lax.broadcasted_iota(jnp.int32, (td, td), 0) == lax.broadcasted_iota(jnp.int32, (td, td), 1)
    for k in range(TOP_K):
        expert = route[:, k:k + 1].astype(jnp.int32)
        start = jnp.sum(jnp.where(lane == expert, start_ref[...], 0.0), axis=-1, keepdims=True)
        dest = start + route[:, 2 * TOP_K + k:2 * TOP_K + k + 1]
        for j in range(tb // td):
            as_row = jnp.sum(jnp.where(diagonal, dest[j * td:(j + 1) * td, :], 0.0), axis=0, keepdims=True)
            o_ref[j:j + 1, k * td:(k + 1) * td] = as_row.astype(jnp.int32)


def _dest_table(route, pstarts, td):
    n = route.shape[0]
    tb = math.gcd(n, 8 * td)
    starts = jnp.zeros((1, ROUTE_W), F32).at[0, 0:N_EXPERTS].set(pstarts.astype(F32))
    return pl.pallas_call(
        functools.partial(_dest_kernel, td=td),
        grid=(n // tb,),
        in_specs=[pl.BlockSpec((tb, 8), lambda i: (i, 0)), pl.BlockSpec((1, ROUTE_W), lambda i: (0, 0))],
        out_specs=pl.BlockSpec((tb // td, TOP_K * td), lambda i: (i, 0)),
        out_shape=jax.ShapeDtypeStruct((n // td, TOP_K * td), jnp.int32),
        compiler_params=pltpu.CompilerParams(dimension_semantics=("arbitrary",)),
        name="dest_table",
    )(route, starts)


def _final_kernel(dest_hbm, ys_hbm, y_ref, route_ref, w_ref, o_ref, idx_sm, gbuf, sem_idx, sem_g,
                  *, tm, n_steps):
    i = pl.program_id(0)

    def idx_copy(step, slot):
        return pltpu.make_async_copy(dest_hbm.at[step], idx_sm.at[slot], sem_idx.at[slot])

    def gather(slot):
        for r in range(2 * tm):
            pltpu.make_async_copy(ys_hbm.at[pl.ds(idx_sm[slot, r], 1)], gbuf.at[slot, pl.ds(r, 1)],
                                  sem_g.at[slot]).start(priority=r % 2)

    def step(cur):
        nxt = 1 - cur
        if cur == 0:
            @pl.when(i == 0)
            def _():
                idx_copy(0, 0).start()
                idx_copy(0, 0).wait()
                gather(0)
                if n_steps > 1:
                    idx_copy(1, 1).start()

        if n_steps > 1:
            @pl.when(i + 1 < n_steps)
            def _():
                idx_copy(i + 1, nxt).wait()
                gather(nxt)

            if n_steps > 2:
                @pl.when(i + 2 < n_steps)
                def _():
                    idx_copy(i + 2, cur).start()

        pltpu.make_async_copy(ys_hbm.at[pl.ds(0, 2 * tm)], gbuf.at[cur], sem_g.at[cur]).wait()
        route = route_ref[...]
        rows = gbuf[cur]
        y = y_ref[...] + route[:, 2:3] * rows[0:tm] + route[:, 3:4] * rows[tm:]
        o_ref[...] = _rms(y, w_ref[...])

    pl.when(i % 2 == 0)(lambda: step(0))
    pl.when(i % 2 == 1)(lambda: step(1))


def _final(ymid, route, dest_tbl, ys, w, tm):
    n = ymid.shape[0]
    n_steps = n // tm
    return pl.pallas_call(
        functools.partial(_final_kernel, tm=tm, n_steps=n_steps),
        grid=(n_steps,),
        in_specs=[pl.BlockSpec(memory_space=pl.ANY), pl.BlockSpec(memory_space=pl.ANY),
                  pl.BlockSpec((tm, D_MODEL), lambda i: (i, 0)), pl.BlockSpec((tm, 8), lambda i: (i, 0)),
                  pl.BlockSpec((1, D_MODEL), lambda i: (0, 0))],
        out_specs=pl.BlockSpec((tm, D_MODEL), lambda i: (i, 0)),
        out_shape=jax.ShapeDtypeStruct((n, D_MODEL), F32),
        scratch_shapes=[pltpu.SMEM((2, 2 * tm), jnp.int32), pltpu.VMEM((2, 2 * tm, D_MODEL), F32),
                        pltpu.SemaphoreType.DMA((2,)), pltpu.SemaphoreType.DMA((2,))],
        compiler_params=pltpu.CompilerParams(dimension_semantics=("arbitrary",),
                                             vmem_limit_bytes=VMEM_LIMIT),
        name="final",
    )(dest_tbl, ys, ymid, route, w)


def kernel(x_prompt, x_sample, cache_k, cache_v, state_rec, w_in, lam_q1, lam_k1, lam_q2, lam_k2,
           subln_w, lb_param, gnorm_w, w_out, norm1_w, norm2_w, w_group, w_router,
           w_e_gate, w_e_up, w_e_down, final_w):
    assert w_in.shape[0] == 1 and lb_param.shape[0] == 2, "single-layer model"
    bp, sp, _ = x_prompt.shape
    bs, ts, _ = x_sample.shape
    past = cache_k.shape[2]
    assert sp % CHUNK == 0 and past % CHUNK == 0 and ts <= CHUNK and ts & (ts - 1) == 0
    n_p, n_s = bp * sp, bs * ts
    n_all = n_p + n_s
    tm = 512 if (n_p % 512 == 0 and n_s % 512 == 0) else math.gcd(n_p, n_s)

    w_in_bf = w_in[0].astype(BF16)
    w_out_bf = w_out[0].astype(BF16)
    n1 = norm1_w[0].reshape(1, D_MODEL)
    n2 = norm2_w[0].reshape(1, D_MODEL)
    lam_vecs = jnp.stack([lam_q1[0], lam_k1[0], lam_q2[0], lam_k2[0]]).astype(F32)
    sub_w = subln_w[0].reshape(1, HEAD_W)
    gn_w = gnorm_w[0].reshape(1, HEAD_W)
    w_route = jnp.zeros((D_MODEL, ROUTE_W), F32)
    w_route = w_route.at[:, 0:N_GROUPS].set(w_group[0]).at[:, N_GROUPS:N_GROUPS + N_EXPERTS].set(w_router[0])
    wr_hi = w_route.astype(BF16)
    wr = jnp.concatenate([wr_hi, (w_route - wr_hi.astype(F32)).astype(BF16)], axis=1)

    xp = x_prompt.reshape(n_p, D_MODEL)
    xs = x_sample.reshape(n_s, D_MODEL)

    kf, vf, kb, qr, logf, kr, ir, gr, qt, vt = _inproj(xp, n1, w_in_bf, lb_param, tm, True)
    a_p = _attn_prompt(qt, kb, vt, lam_vecs, sub_w.reshape(HEAD_W, 1), bp, sp)
    chunks = 8 if sp % (8 * CHUNK) == 0 else 1
    r_p, state_p = _hgrn(qr, logf, kr, ir, gr, gn_w, None, bp, sp, CHUNK, chunks)
    ymid_p, h2_p, route_p, counts_p = _merge(xp, a_p, r_p, w_out_bf, n2, wr,
                                             jnp.zeros((1, ROUTE_W), F32), tm)
    k_prompt = kf.reshape(1, bp, sp, A_HEADS, HEAD_W)
    v_prompt = vf.reshape(1, bp, sp, A_HEADS, HEAD_W)

    kf, vf, kb, qr, logf, kr, ir, gr, q, vb = _inproj(xs, n1, w_in_bf, lb_param, tm, False)
    cache_rows = (bs, past * A_HEADS, HEAD_W)
    a_s = _attn_sample(q, cache_k.reshape(cache_rows), cache_v.reshape(cache_rows), kb, vb,
                       lam_vecs, sub_w, bs, ts)
    r_s, state_s = _hgrn(qr, logf, kr, ir, gr, gn_w, state_rec[0], bs, ts, ts, 1)
    ymid_s, h2_s, route_s, counts = _merge(xs, a_s, r_s, w_out_bf, n2, wr, counts_p, tm)
    k_sample = kf.reshape(1, bs, ts, A_HEADS, HEAD_W)
    v_sample = vf.reshape(1, bs, ts, A_HEADS, HEAD_W)

    pstarts, pends, pcounts, blk_exp, n_used, n_rows = _expert_layout(counts, n_all)
    td = min(512, math.gcd(n_p, n_s))
    dest_p = _dest_table(route_p, pstarts, td)
    dest_s = _dest_table(route_s, pstarts, td)
    xs_rows = _dispatch(h2_p, h2_s, jnp.concatenate([dest_p, dest_s], axis=0), pends, pcounts, n_used,
                        n_rows, td)
    ys_rows = _experts(xs_rows, blk_exp, n_used, w_e_gate[0], w_e_up[0], w_e_down[0])

    fw = final_w.reshape(1, D_MODEL)
    y_prompt = _final(ymid_p, route_p, dest_p, ys_rows, fw, td).reshape(bp, sp, D_MODEL)
    y_sample = _final(ymid_s, route_s, dest_s, ys_rows, fw, td).reshape(bs, ts, D_MODEL)
    return (y_prompt, y_sample, k_prompt, v_prompt, state_p[None], k_sample, v_sample, state_s[None])
```

```python
import functools
import math

import numpy as np
import jax
import jax.numpy as jnp
from jax import lax
from jax.experimental import pallas as pl
from jax.experimental.pallas import tpu as pltpu

F32 = jnp.float32
BF16 = jnp.bfloat16

D_MODEL = 1024
RMS_EPS = 1e-6
CHUNK = 64
A_HEADS = 4
A_HEAD_DIM = 64
HEAD_W = 2 * A_HEAD_DIM
KEY_BLOCK = 256
VT_ROWS = HEAD_W + 16
LOG2_E = math.log2(math.e)
R_HEADS = 4
SEG_W = 512
N_SEG = 7
N_GROUPS = 4
EXPERTS_PER_GROUP = 8
N_EXPERTS = N_GROUPS * EXPERTS_PER_GROUP
TOP_K = 2
D_EXPERT = 512
EXPERT_ROWS = 256
ROUTE_W = 128
LAM_INIT = 0.8 - 0.6 * math.exp(-0.3 * 0)
VMEM_LIMIT = 56 * 1024 * 1024


def _sigmoid(x):
    return 1.0 / (1.0 + jnp.exp(-x))


def _dot(a, b):
    return jnp.dot(a, b, preferred_element_type=F32)


def _dot_nt(a, b):
    return lax.dot_general(a, b, (((1,), (1,)), ((), ())), preferred_element_type=F32)


def _dot_tn(a, b):
    return lax.dot_general(a, b, (((0,), (0,)), ((), ())), preferred_element_type=F32)


def _rms(x, w):
    return x * lax.rsqrt(jnp.mean(x * x, axis=-1, keepdims=True) + RMS_EPS) * w


def _inproj_kernel(x_ref, n1_ref, w_ref, lbp_ref, kf_ref, vf_ref, kb_ref, qr_ref, logf_ref, kr_ref,
                   ir_ref, gr_ref, qa_ref, va_ref, h_sc, *, transposed_qv):
    p = lbp_ref[...]
    e = jnp.exp(p - jnp.max(p, axis=0, keepdims=True))
    lb = e[0:1] / jnp.sum(e, axis=0, keepdims=True)
    ones = jnp.ones((VT_ROWS - HEAD_W, KEY_BLOCK), BF16)

    sub = min(KEY_BLOCK, x_ref.shape[0])
    for s in range(x_ref.shape[0] // sub):
        rows = slice(s * sub, (s + 1) * sub)
        h_sc[rows, :] = _rms(x_ref[rows, :], n1_ref[...]).astype(BF16)

        def seg(i, rows=rows):
            return _dot(h_sc[rows, :], w_ref[:, i * SEG_W:(i + 1) * SEG_W])

        q = seg(0) * (A_HEAD_DIM ** -0.5)
        k = seg(1)
        for hd in range(A_HEADS):
            kf_ref[rows, hd, :] = k[:, hd * HEAD_W:(hd + 1) * HEAD_W]
        kb_ref[rows, :] = k.astype(BF16)
        v = seg(2)
        for hd in range(A_HEADS):
            vf_ref[rows, hd, :] = v[:, hd * HEAD_W:(hd + 1) * HEAD_W]
        if transposed_qv:
            qa_ref[s] = (q * LOG2_E).T.astype(BF16)
            vt = v.T.astype(BF16)
            for hd in range(A_HEADS):
                va_ref[s, hd * VT_ROWS:hd * VT_ROWS + HEAD_W, :] = vt[hd * HEAD_W:(hd + 1) * HEAD_W, :]
                va_ref[s, hd * VT_ROWS + HEAD_W:(hd + 1) * VT_ROWS, :] = ones
        else:
            qa_ref[rows, :] = q.astype(BF16)
            va_ref[rows, :] = v.astype(BF16)
        qr = seg(3)
        qr_ref[rows, :] = (qr * _sigmoid(qr)).astype(BF16)
        f = lb + (1.0 - lb) * _sigmoid(seg(4))
        logf_ref[rows, :] = jnp.log(f) * LOG2_E
        kr_ref[rows, :] = (1.0 - f).astype(BF16)
        ir_ref[rows, :] = seg(5).astype(BF16)
        gr_ref[rows, :] = seg(6)


def _inproj(x, n1, w_bf, lbp, tm, transposed_qv):
    n = x.shape[0]
    row = lambda i: (i, 0)
    fix = lambda i: (0, 0)
    out = lambda dt: jax.ShapeDtypeStruct((n, SEG_W), dt)
    ospec = pl.BlockSpec((tm, SEG_W), row)
    hspec = pl.BlockSpec((tm, A_HEADS, HEAD_W), lambda i: (i, 0, 0))
    hout = jax.ShapeDtypeStruct((n, A_HEADS, HEAD_W), F32)
    slabs = tm // KEY_BLOCK
    tspec = lambda rows: pl.BlockSpec((slabs, rows, KEY_BLOCK), lambda i: (i, 0, 0))
    tout = lambda rows: jax.ShapeDtypeStruct((n // KEY_BLOCK, rows, KEY_BLOCK), BF16)
    if transposed_qv:
        qv_specs = [tspec(SEG_W), tspec(A_HEADS * VT_ROWS)]
        qv_shapes = [tout(SEG_W), tout(A_HEADS * VT_ROWS)]
    else:
        qv_specs = [ospec, ospec]
        qv_shapes = [out(BF16), out(BF16)]
    return pl.pallas_call(
        functools.partial(_inproj_kernel, transposed_qv=transposed_qv),
        grid=(n // tm,),
        in_specs=[pl.BlockSpec((tm, D_MODEL), row), pl.BlockSpec((1, D_MODEL), fix),
                  pl.BlockSpec((D_MODEL, N_SEG * SEG_W), fix), pl.BlockSpec(lbp.shape, fix)],
        out_specs=[hspec, hspec] + [ospec] * 6 + qv_specs,
        out_shape=[hout, hout, out(BF16), out(BF16), out(F32), out(BF16), out(BF16), out(F32)] + qv_shapes,
        scratch_shapes=[pltpu.VMEM((tm, D_MODEL), BF16)],
        compiler_params=pltpu.CompilerParams(dimension_semantics=("arbitrary",),
                                             vmem_limit_bytes=VMEM_LIMIT),
        name="inproj",
    )(x, n1, w_bf, lbp)


def _lam_value(lam_ref):
    l = lam_ref[...]
    s1 = jnp.sum(l[0:1] * l[1:2], axis=-1, keepdims=True)
    s2 = jnp.sum(l[2:3] * l[3:4], axis=-1, keepdims=True)
    return jnp.exp(s1) - jnp.exp(s2) + LAM_INIT


def _split_maps(q):
    lane = lax.broadcasted_iota(jnp.int32, q.shape, 1)
    zero = jnp.zeros_like(q)
    return jnp.concatenate([jnp.where(lane < A_HEAD_DIM, q, zero),
                            jnp.where(lane >= A_HEAD_DIM, q, zero)], axis=0)


def _attn_finish(o_num, l, lam, sub_w, tq):
    o = o_num[0:tq] / l[0:tq] - lam * (o_num[tq:] / l[tq:])
    return (_rms(o, sub_w) * (1.0 - LAM_INIT)).astype(BF16)


def _attn_prompt_kernel(lam_ref, sub_ref, bias_ref, qt_ref, k_ref, vt_ref, o_ref,
                        qbd_sc, m_sc, acc_sc, sa_sc, sb_sc, *, qblocks):
    i = pl.program_id(2)
    kb = KEY_BLOCK

    sub = lax.broadcasted_iota(jnp.int32, (HEAD_W, kb), 0)
    for r in range(2 * qblocks):
        qt = qt_ref[r % qblocks]
        keep = (sub < A_HEAD_DIM) if r < qblocks else (sub >= A_HEAD_DIM)
        qbd_sc[r] = jnp.where(keep, qt, jnp.zeros_like(qt))
    m_sc[...] = jnp.full(m_sc.shape, -jnp.inf, F32)
    acc_sc[...] = jnp.zeros(acc_sc.shape, F32)

    def scores(r, key_block):
        return _dot(k_ref[pl.ds(pl.multiple_of(key_block * kb, kb), kb), :], qbd_sc[r])

    def update(r, key_block, st):
        m_old = m_sc[r:r + 1, :]
        m_new = jnp.maximum(m_old, jnp.max(st, axis=0, keepdims=True))
        p = jnp.exp2(st - m_new).astype(BF16)
        acc_sc[r] = jnp.exp2(m_old - m_new) * acc_sc[r] + _dot(vt_ref[key_block], p)
        m_sc[r:r + 1, :] = m_new

    n_q = 2 * qblocks
    first = i * qblocks
    for r in range(n_q):
        sa_sc[r] = scores(r, 0)

    def visible_blocks(start, count):
        lead = 2
        chains = [(j, r) for j in range(count) for r in range(n_q)]
        for c in range(len(chains) + lead):
            if c < len(chains):
                j, r = chains[c]
                (sb_sc if j % 2 == 0 else sa_sc)[r] = scores(r, start + j + 1)
            if c >= lead:
                j, r = chains[c - lead]
                update(r, start + j, (sa_sc if j % 2 == 0 else sb_sc)[r])

    def tile_pair(t, carry):
        visible_blocks(2 * t * qblocks, 2 * qblocks)
        return carry

    lax.fori_loop(0, i // 2, tile_pair, 0)
    pl.when(i % 2 == 1)(lambda: visible_blocks((i - 1) * qblocks, qblocks))

    work = [(kl, r) for kl in range(qblocks) for r in range(n_q) if r % qblocks >= kl]
    buffer_of = lambda kl: sa_sc if kl % 2 == 0 else sb_sc
    ahead = 4
    issued = set()
    for n, (kl, r) in enumerate(work):
        for kl2, r2 in work[n:n + 1 + ahead]:
            if kl2 > 0 and (kl2, r2) not in issued:
                assert (kl2 - 2, r2) not in work[n:], "score buffer reused before it was read"
                buffer_of(kl2)[r2] = scores(r2, first + kl2)
                issued.add((kl2, r2))
        st = buffer_of(kl)[r]
        update(r, first + kl, st + bias_ref[...] if r % qblocks == kl else st)

    lam = _lam_value(lam_ref)
    for ql in range(qblocks):
        a1 = acc_sc[ql]
        a2 = acc_sc[qblocks + ql]
        ot = a1[0:HEAD_W] / a1[HEAD_W:HEAD_W + 1] - lam * (a2[0:HEAD_W] / a2[HEAD_W:HEAD_W + 1])
        ot = ot * lax.rsqrt(jnp.mean(ot * ot, axis=0, keepdims=True) + RMS_EPS) * sub_ref[...]
        o_ref[ql * kb:(ql + 1) * kb, :] = (ot * (1.0 - LAM_INIT)).T.astype(BF16)


def _attn_prompt(qt, k, vt, lam_vecs, sub_col, batch, seq):
    kb = KEY_BLOCK
    tq = min(1024, seq)
    qblocks = tq // kb
    assert qblocks % 2 == 0, "the key-block loop is unrolled by two"
    nq = seq // tq
    pos = np.arange(kb) // CHUNK
    bias = jnp.asarray(np.where(pos[:, None] <= pos[None, :], 0.0, -np.inf), F32)
    fix = lambda b, h, i: (0, 0)
    return pl.pallas_call(
        functools.partial(_attn_prompt_kernel, qblocks=qblocks),
        grid=(batch, A_HEADS, nq),
        in_specs=[pl.BlockSpec((4, A_HEAD_DIM), fix), pl.BlockSpec((HEAD_W, 1), fix),
                  pl.BlockSpec((kb, kb), fix),
                  pl.BlockSpec((qblocks, HEAD_W, kb), lambda b, h, i: (b * nq + i, h, 0)),
                  pl.BlockSpec((seq, HEAD_W), lambda b, h, i: (b, h)),
                  pl.BlockSpec((seq // kb, VT_ROWS, kb), lambda b, h, i: (b, h, 0))],
        out_specs=pl.BlockSpec((tq, HEAD_W), lambda b, h, i: (b * nq + i, h)),
        out_shape=jax.ShapeDtypeStruct(k.shape, BF16),
        scratch_shapes=[pltpu.VMEM((2 * qblocks, HEAD_W, kb), BF16), pltpu.VMEM((2 * qblocks, kb), F32),
                        pltpu.VMEM((2 * qblocks, VT_ROWS, kb), F32),
                        pltpu.VMEM((2 * qblocks, kb, kb), F32), pltpu.VMEM((2 * qblocks, kb, kb), F32)],
        compiler_params=pltpu.CompilerParams(
            dimension_semantics=("arbitrary", "arbitrary", "arbitrary"), vmem_limit_bytes=VMEM_LIMIT),
        name="attn_prompt",
    )(lam_vecs, sub_col, bias, qt, k, vt)


def _attn_sample_kernel(lam_ref, sub_ref, bias_c_ref, bias_n_ref, q_ref, kc_ref, vc_ref, kn_ref, vn_ref,
                        o_ref, *, t):
    lam = _lam_value(lam_ref)
    heads = [slice(h * HEAD_W, (h + 1) * HEAD_W) for h in range(A_HEADS)]
    qbd = jnp.concatenate([_split_maps(q_ref[:, cs]) for cs in heads], axis=0)
    keys = [kc_ref[0].astype(BF16), jnp.concatenate([kn_ref[:, cs] for cs in heads], axis=0)]
    vals = [vc_ref[0].astype(BF16), jnp.concatenate([vn_ref[:, cs] for cs in heads], axis=0)]
    scores = [_dot_nt(qbd, k) + b[...] for k, b in zip(keys, (bias_c_ref, bias_n_ref))]
    m = functools.reduce(jnp.maximum, [jnp.max(sc, axis=-1, keepdims=True) for sc in scores])
    probs = [jnp.exp(sc - m) for sc in scores]
    l = sum(jnp.sum(p, axis=-1, keepdims=True) for p in probs)
    o_num = sum(_dot(p.astype(BF16), v) for p, v in zip(probs, vals))
    for h, cs in enumerate(heads):
        rows = slice(h * 2 * t, (h + 1) * 2 * t)
        o_ref[:, cs] = _attn_finish(o_num[rows], l[rows], lam, sub_ref[...], t)


def _attn_sample(q, cache_k, cache_v, k_new, v_new, lam_vecs, sub_w, batch, t):
    fix = lambda b: (0, 0)
    row = lambda b: (b, 0)
    cache = pl.BlockSpec((1,) + cache_k.shape[1:], lambda b: (b, 0, 0))
    q_head = np.arange(A_HEADS * 2 * t) // (2 * t)
    off = lambda key_head: jnp.asarray(np.where(q_head[:, None] == key_head[None, :], 0.0, -np.inf), F32)
    bias_c = off(np.arange(cache_k.shape[1]) % A_HEADS)
    bias_n = off(np.arange(A_HEADS * t) // t)
    return pl.pallas_call(
        functools.partial(_attn_sample_kernel, t=t),
        grid=(batch,),
        in_specs=[pl.BlockSpec((4, A_HEAD_DIM), fix), pl.BlockSpec((1, HEAD_W), fix),
                  pl.BlockSpec(bias_c.shape, fix), pl.BlockSpec(bias_n.shape, fix),
                  pl.BlockSpec((t, SEG_W), row),
                  cache, cache, pl.BlockSpec((t, SEG_W), row), pl.BlockSpec((t, SEG_W), row)],
        out_specs=pl.BlockSpec((t, SEG_W), row),
        out_shape=jax.ShapeDtypeStruct(q.shape, BF16),
        compiler_params=pltpu.CompilerParams(dimension_semantics=("arbitrary",),
                                             vmem_limit_bytes=VMEM_LIMIT),
        name="attn_sample",
    )(lam_vecs, sub_w, bias_c, bias_n, q, cache_k, cache_v, k_new, v_new)


def _hgrn_consts(length):
    t = np.arange(length)[:, None]
    j = np.arange(length)[None, :]
    sums = [j <= t]
    masks = [j == t]
    blk = length
    while blk >= 2:
        half = blk // 2
        mid_t = (t // blk) * blk + half
        mid_j = (j // blk) * blk + half
        sums.append(np.where(t >= mid_t, (j >= mid_t) & (j <= t), (j > t) & (j < mid_t)))
        masks.append((t // blk == j // blk) & (t >= mid_t) & (j < mid_j))
        blk = half
    sums = np.tile(np.concatenate(sums, axis=0).astype(np.float32), (1, 2))
    masks = np.concatenate(masks, axis=0).astype(np.float32)
    return jnp.asarray(sums, BF16), jnp.asarray(masks, F32)


def _hgrn_kernel(*refs, length, chunks, has_state_in):
    if has_state_in:
        (sums_ref, masks_ref, gn_ref, q_ref, logf_ref, k_ref, v_ref, g_ref, s0_ref,
         r_ref, sout_ref, st_sc) = refs
    else:
        (sums_ref, masks_ref, gn_ref, q_ref, logf_ref, k_ref, v_ref, g_ref,
         r_ref, sout_ref, st_sc) = refs
    step = pl.program_id(1)
    levels = int(math.log2(length))
    L = length
    group = math.gcd(chunks, 8)

    @pl.when(step == 0)
    def _():
        for h in range(R_HEADS):
            if has_state_in:
                st_sc[h] = s0_ref[0, h].T
            else:
                st_sc[h] = jnp.zeros(st_sc.shape[1:], F32)

    heads = [slice(h * HEAD_W, (h + 1) * HEAD_W) for h in range(R_HEADS)]

    def chunk_group(c, carry):
        rows = [pl.ds(pl.multiple_of((c * group + g) * L, L), L) for g in range(group)]
        expo = []
        for g in range(group):
            logf = logf_ref[rows[g], :]
            hi = logf.astype(BF16)
            lo = (logf - hi.astype(F32)).astype(BF16)
            expo.append(_dot(sums_ref[...], jnp.concatenate([hi, lo], axis=0)))
        streams = [(g, h) for g in range(group) for h in range(R_HEADS)]
        intra, q_dec, upd = {}, {}, {}
        for g, h in streams:
            cs = heads[h]
            q = q_ref[rows[g], cs]
            k = k_ref[rows[g], cs]
            v = v_ref[rows[g], cs]
            a = masks_ref[0:L, :] * _dot_nt(q, k)
            for lv in range(levels):
                x = jnp.exp2(expo[g][(1 + lv) * L:(2 + lv) * L, cs]).astype(BF16)
                a = a + masks_ref[(1 + lv) * L:(2 + lv) * L, :] * _dot_nt(q * x, k * x)
            intra[g, h] = _dot(a.astype(BF16), v)
            b = expo[g][0:L, cs]
            q_dec[g, h] = q * jnp.exp2(b).astype(BF16)
            k_dec = k * jnp.exp2(b[L - 1:L, :] - b).astype(BF16)
            upd[g, h] = _dot_tn(v, k_dec)
        for g, h in streams:
            cs = heads[h]
            st = st_sc[h]
            o = intra[g, h] + _dot_nt(q_dec[g, h], st.astype(BF16))
            st_sc[h] = st * jnp.exp2(expo[g][L - 1:L, cs]) + upd[g, h]
            gate = g_ref[rows[g], cs]
            r_ref[rows[g], cs] = (_rms(o, gn_ref[...]) * (gate * _sigmoid(gate))).astype(BF16)
        return carry

    lax.fori_loop(0, chunks // group, chunk_group, 0)

    @pl.when(step == pl.num_programs(1) - 1)
    def _():
        for h in range(R_HEADS):
            sout_ref[0, h] = st_sc[h].T


def _hgrn(q, logf, k, v, g, gn_w, state_in, batch, seq, length, chunks):
    sums, masks = _hgrn_consts(length)
    tm = length * chunks
    steps = seq // tm
    fix = lambda b, s: (0, 0)
    row = lambda b, s: (b * steps + s, 0)
    tok = pl.BlockSpec((tm, SEG_W), row)
    in_specs = [pl.BlockSpec(sums.shape, fix), pl.BlockSpec(masks.shape, fix), pl.BlockSpec((1, HEAD_W), fix),
                tok, tok, tok, tok, tok]
    args = [sums, masks, gn_w, q, logf, k, v, g]
    state_spec = pl.BlockSpec((1, R_HEADS, HEAD_W, HEAD_W), lambda b, s: (b, 0, 0, 0))
    if state_in is not None:
        in_specs.append(state_spec)
        args.append(state_in)
    return pl.pallas_call(
        functools.partial(_hgrn_kernel, length=length, chunks=chunks, has_state_in=state_in is not None),
        grid=(batch, steps),
        in_specs=in_specs,
        out_specs=[tok, state_spec],
        out_shape=[jax.ShapeDtypeStruct(q.shape, BF16),
                   jax.ShapeDtypeStruct((batch, R_HEADS, HEAD_W, HEAD_W), F32)],
        scratch_shapes=[pltpu.VMEM((R_HEADS, HEAD_W, HEAD_W), F32)],
        compiler_params=pltpu.CompilerParams(dimension_semantics=("arbitrary", "arbitrary"),
                                             vmem_limit_bytes=VMEM_LIMIT),
        name="hgrn",
    )(*args)


def _merge_kernel(x_ref, a_ref, r_ref, wo_ref, n2_ref, wr_ref, tri_ref, cnt0_ref,
                  ymid_ref, h2_ref, route_ref, cnt_ref, cnt_sc):
    step = pl.program_id(0)

    @pl.when(step == 0)
    def _():
        cnt_sc[...] = cnt0_ref[...]

    sub = tri_ref.shape[0]
    for s in range(x_ref.shape[0] // sub):
        _merge_rows(slice(s * sub, (s + 1) * sub), x_ref, a_ref, r_ref, wo_ref, n2_ref, wr_ref,
                    tri_ref, ymid_ref, h2_ref, route_ref, cnt_sc)
    cnt_ref[...] = cnt_sc[...]


def _merge_rows(rows, x_ref, a_ref, r_ref, wo_ref, n2_ref, wr_ref, tri_ref,
                ymid_ref, h2_ref, route_ref, cnt_sc):
    half = a_ref.shape[1]
    y = x_ref[rows, :] + _dot(a_ref[rows, :], wo_ref[0:half, :]) + _dot(r_ref[rows, :], wo_ref[half:, :])
    ymid_ref[rows, :] = y
    hn = _rms(y, n2_ref[...])
    h2_ref[rows, :] = hn
    hi = hn.astype(BF16)
    lo = (hn - hi.astype(F32)).astype(BF16)
    logits = (_dot(hi, wr_ref[:, :ROUTE_W]) + _dot(lo, wr_ref[:, :ROUTE_W])) + _dot(hi, wr_ref[:, ROUTE_W:])
    lane = lax.broadcasted_iota(jnp.int32, logits.shape, 1)
    neg = -jnp.inf
    big = jnp.int32(ROUTE_W)

    def top1(mask):
        val = jnp.max(jnp.where(mask, logits, neg), axis=-1, keepdims=True)
        idx = jnp.min(jnp.where(mask & (logits == val), lane, big), axis=-1, keepdims=True)
        return val, idx

    is_group = lane < N_GROUPS
    g_max, g_sel = top1(is_group)
    p_group = 1.0 / jnp.sum(jnp.where(is_group, jnp.exp(logits - g_max), 0.0), axis=-1, keepdims=True)
    first = N_GROUPS + g_sel * EXPERTS_PER_GROUP
    in_group = (lane >= first) & (lane < first + EXPERTS_PER_GROUP)
    v1, i1 = top1(in_group)
    v2, i2 = top1(in_group & (lane != i1))
    e2 = jnp.exp(v2 - v1)
    gate1 = p_group * (1.0 / (1.0 + e2))
    gate2 = p_group * (e2 / (1.0 + e2))
    hot1 = lane == i1
    hot2 = lane == i2
    hot = jnp.where(hot1 | hot2, 1.0, 0.0)
    before = cnt_sc[...] + _dot(tri_ref[...], hot.astype(BF16))
    rank1 = jnp.sum(jnp.where(hot1, before, 0.0), axis=-1, keepdims=True)
    rank2 = jnp.sum(jnp.where(hot2, before, 0.0), axis=-1, keepdims=True)
    cnt_sc[...] = cnt_sc[...] + jnp.sum(hot, axis=0, keepdims=True)
    col = lax.broadcasted_iota(jnp.int32, (hot.shape[0], route_ref.shape[1]), 1)
    cols = [(i1 - N_GROUPS).astype(F32), (i2 - N_GROUPS).astype(F32), gate1, gate2, rank1, rank2]
    route = jnp.zeros(col.shape, F32)
    for c, val in enumerate(cols):
        route = jnp.where(col == c, val, route)
    route_ref[rows, :] = route


def _merge(x, a, r, wo_bf, n2, wr, counts_in, tm):
    n = x.shape[0]
    row = lambda i: (i, 0)
    fix = lambda i: (0, 0)
    sub = min(256, tm)
    tri = jnp.asarray(np.tril(np.ones((sub, sub), np.float32), -1), BF16)
    return pl.pallas_call(
        _merge_kernel,
        grid=(n // tm,),
        in_specs=[pl.BlockSpec((tm, D_MODEL), row), pl.BlockSpec((tm, SEG_W), row),
                  pl.BlockSpec((tm, SEG_W), row), pl.BlockSpec((D_MODEL, D_MODEL), fix),
                  pl.BlockSpec((1, D_MODEL), fix), pl.BlockSpec((D_MODEL, 2 * ROUTE_W), fix),
                  pl.BlockSpec((sub, sub), fix),
                  pl.BlockSpec((1, ROUTE_W), fix)],
        out_specs=[pl.BlockSpec((tm, D_MODEL), row), pl.BlockSpec((tm, D_MODEL), row),
                   pl.BlockSpec((tm, 8), row), pl.BlockSpec((1, ROUTE_W), fix)],
        out_shape=[jax.ShapeDtypeStruct((n, D_MODEL), F32), jax.ShapeDtypeStruct((n, D_MODEL), F32),
                   jax.ShapeDtypeStruct((n, 8), F32), jax.ShapeDtypeStruct((1, ROUTE_W), F32)],
        scratch_shapes=[pltpu.VMEM((1, ROUTE_W), F32)],
        compiler_params=pltpu.CompilerParams(dimension_semantics=("arbitrary",),
                                             vmem_limit_bytes=VMEM_LIMIT),
        name="merge",
    )(x, a, r, wo_bf, n2, wr, tri, counts_in)


def _dispatch_kernel(pend_ref, pcnt_ref, n_used_ref, dest_hbm, hp_ref, hs_ref, xs_hbm,
                     idx_sm, zero_sc, sem_idx, sem_out, *, tm, steps_p, steps_s, first_spare, n_blocks):
    i = pl.program_id(0)
    slot = i % 2
    n_steps = steps_p + steps_s

    def idx_copy(step, s):
        return pltpu.make_async_copy(dest_hbm.at[step], idx_sm.at[s], sem_idx.at[s])

    @pl.when(i == 0)
    def _():
        idx_copy(0, 0).start()
        zero_sc[...] = jnp.zeros(zero_sc.shape, F32)

        def fill(start):
            start = pl.multiple_of(start, EXPERT_ROWS)
            return pltpu.make_async_copy(zero_sc, xs_hbm.at[pl.ds(start, EXPERT_ROWS)], sem_out)

        for action in ("start", "wait"):
            for e in range(N_EXPERTS):
                pl.when(pcnt_ref[e] > 0)(
                    lambda e=e: getattr(fill(pend_ref[e] - EXPERT_ROWS), action)())
            for blk in range(first_spare, n_blocks):
                pl.when(blk >= n_used_ref[0])(
                    lambda blk=blk: getattr(fill(blk * EXPERT_ROWS), action)())

    idx_copy(i, slot).wait()

    @pl.when(i + 1 < n_steps)
    def _():
        idx_copy(i + 1, 1 - slot).start()

    def scatter(h_ref, idx_slot):
        for r in range(2 * tm):
            pltpu.make_async_copy(h_ref.at[pl.ds(r % tm, 1)], xs_hbm.at[pl.ds(idx_sm[idx_slot, r], 1)],
                                  sem_out).start(priority=r % 2)
        for _ in range(2):
            pltpu.make_async_copy(h_ref, xs_hbm.at[pl.ds(0, tm)], sem_out).wait()

    for parity in range(2):
        pl.when((i < steps_p) & (slot == parity))(lambda p=parity: scatter(hp_ref, p))
        pl.when((i >= steps_p) & (slot == parity))(lambda p=parity: scatter(hs_ref, p))


def _dispatch(h2_p, h2_s, dest_tbl, pends, pcounts, n_used, n_rows, tm):
    steps_p = h2_p.shape[0] // tm
    steps_s = h2_s.shape[0] // tm
    n_blocks = n_rows // EXPERT_ROWS
    first_spare = (h2_p.shape[0] + h2_s.shape[0]) * TOP_K // EXPERT_ROWS
    grid_spec = pltpu.PrefetchScalarGridSpec(
        num_scalar_prefetch=3, grid=(steps_p + steps_s,),
        in_specs=[pl.BlockSpec(memory_space=pl.ANY),
                  pl.BlockSpec((tm, D_MODEL), lambda i, pe, pc, nu: (jnp.minimum(i, steps_p - 1), 0)),
                  pl.BlockSpec((tm, D_MODEL), lambda i, pe, pc, nu: (jnp.maximum(i - steps_p, 0), 0))],
        out_specs=pl.BlockSpec(memory_space=pl.ANY),
        scratch_shapes=[pltpu.SMEM((2, 2 * tm), jnp.int32), pltpu.VMEM((EXPERT_ROWS, D_MODEL), F32),
                        pltpu.SemaphoreType.DMA((2,)), pltpu.SemaphoreType.DMA])
    return pl.pallas_call(
        functools.partial(_dispatch_kernel, tm=tm, steps_p=steps_p, steps_s=steps_s,
                          first_spare=first_spare, n_blocks=n_blocks),
        grid_spec=grid_spec,
        out_shape=jax.ShapeDtypeStruct((n_rows, D_MODEL), F32),
        compiler_params=pltpu.CompilerParams(dimension_semantics=("arbitrary",),
                                             vmem_limit_bytes=VMEM_LIMIT),
        name="dispatch",
    )(pends, pcounts, n_used, dest_tbl, h2_p, h2_s)


def _experts_kernel(blk_exp_ref, n_used_ref, x_ref, wg_ref, wu_ref, wd_ref, y_ref, wg_sc, wu_sc, wd_sc):
    i = pl.program_id(0)

    @pl.when(i < n_used_ref[0])
    def _():
        prev = blk_exp_ref[jnp.maximum(i - 1, 0)]

        @pl.when((i == 0) | (prev != blk_exp_ref[i]))
        def _():
            wg_sc[...] = wg_ref[0].astype(BF16)
            wu_sc[...] = wu_ref[0].astype(BF16)
            wd_sc[...] = wd_ref[0].astype(BF16)

        xb = x_ref[...].astype(BF16)
        gate = _dot(xb, wg_sc[...])
        up = _dot(xb, wu_sc[...])
        act = (gate * _sigmoid(gate) * up).astype(BF16)
        y_ref[...] = _dot(act, wd_sc[...])

    @pl.when(i >= n_used_ref[0])
    def _():
        y_ref[...] = jnp.zeros(y_ref.shape, F32)


def _experts(xs, blk_exp, n_used, wg, wu, wd):
    n_blocks = int(blk_exp.shape[0])
    wmap = lambda i, be, nu: (be[i], 0, 0)
    xmap = lambda i, be, nu: (jnp.minimum(i, nu[0] - 1), 0)
    grid_spec = pltpu.PrefetchScalarGridSpec(
        num_scalar_prefetch=2,
        grid=(n_blocks,),
        in_specs=[pl.BlockSpec((EXPERT_ROWS, D_MODEL), xmap),
                  pl.BlockSpec((1, D_MODEL, D_EXPERT), wmap),
                  pl.BlockSpec((1, D_MODEL, D_EXPERT), wmap),
                  pl.BlockSpec((1, D_EXPERT, D_MODEL), wmap)],
        out_specs=pl.BlockSpec((EXPERT_ROWS, D_MODEL), lambda i, be, nu: (i, 0)),
        scratch_shapes=[pltpu.VMEM((D_MODEL, D_EXPERT), BF16), pltpu.VMEM((D_MODEL, D_EXPERT), BF16),
                        pltpu.VMEM((D_EXPERT, D_MODEL), BF16)],
    )
    return pl.pallas_call(
        _experts_kernel,
        grid_spec=grid_spec,
        out_shape=jax.ShapeDtypeStruct(xs.shape, F32),
        compiler_params=pltpu.CompilerParams(dimension_semantics=("arbitrary",),
                                             vmem_limit_bytes=VMEM_LIMIT),
        name="experts",
    )(blk_exp, n_used, xs, wg, wu, wd)


def _expert_layout(counts_row, n_tok):
    counts = counts_row[0, N_GROUPS:N_GROUPS + N_EXPERTS].astype(jnp.int32)
    pcounts = (counts + EXPERT_ROWS - 1) // EXPERT_ROWS * EXPERT_ROWS
    pends = jnp.cumsum(pcounts).astype(jnp.int32)
    pstarts = pends - pcounts
    n_blocks = -(-(n_tok * TOP_K) // EXPERT_ROWS) + N_EXPERTS
    blk_start = jnp.arange(n_blocks, dtype=jnp.int32) * EXPERT_ROWS
    blk_exp = jnp.minimum(jnp.sum(blk_start[:, None] >= pends[None, :], axis=1), N_EXPERTS - 1).astype(jnp.int32)
    n_used = (pends[-1:] // EXPERT_ROWS).astype(jnp.int32)
    return pstarts, pends, pcounts, blk_exp, n_used, n_blocks * EXPERT_ROWS


def _dest_kernel(route_ref, start_ref, o_ref, *, td):
    route = route_ref[...]
    tb = route.shape[0]
    lane = lax.broadcasted_iota(jnp.int32, (tb, ROUTE_W), 1)
    diagonal = lax.broadcasted_iota(jnp.int32, (td, td), 0) == lax.broadcasted_iota(jnp.int32, (td, td), 1)
    for k in range(TOP_K):
        expert = route[:, k:k + 1].astype(jnp.int32)
        start = jnp.sum(jnp.where(lane == expert, start_ref[...], 0.0), axis=-1, keepdims=True)
        dest = start + route[:, 2 * TOP_K + k:2 * TOP_K + k + 1]
        for j in range(tb // td):
            as_row = jnp.sum(jnp.where(diagonal, dest[j * td:(j + 1) * td, :], 0.0), axis=0, keepdims=True)
            o_ref[j:j + 1, k * td:(k + 1) * td] = as_row.astype(jnp.int32)


def _dest_table(route, pstarts, td):
    n = route.shape[0]
    tb = math.gcd(n, 8 * td)
    starts = jnp.zeros((1, ROUTE_W), F32).at[0, 0:N_EXPERTS].set(pstarts.astype(F32))
    return pl.pallas_call(
        functools.partial(_dest_kernel, td=td),
        grid=(n // tb,),
        in_specs=[pl.BlockSpec((tb, 8), lambda i: (i, 0)), pl.BlockSpec((1, ROUTE_W), lambda i: (0, 0))],
        out_specs=pl.BlockSpec((tb // td, TOP_K * td), lambda i: (i, 0)),
        out_shape=jax.ShapeDtypeStruct((n // td, TOP_K * td), jnp.int32),
        compiler_params=pltpu.CompilerParams(dimension_semantics=("arbitrary",)),
        name="dest_table",
    )(route, starts)


def _final_kernel(dest_hbm, ys_hbm, y_ref, route_ref, w_ref, o_ref, idx_sm, gbuf, sem_idx, sem_g,
                  *, tm, n_steps):
    i = pl.program_id(0)

    def idx_copy(step, slot):
        return pltpu.make_async_copy(dest_hbm.at[step], idx_sm.at[slot], sem_idx.at[slot])

    def gather(slot):
        for r in range(2 * tm):
            pltpu.make_async_copy(ys_hbm.at[pl.ds(idx_sm[slot, r], 1)], gbuf.at[slot, pl.ds(r, 1)],
                                  sem_g.at[slot]).start(priority=r % 2)

    def step(cur):
        nxt = 1 - cur
        if cur == 0:
            @pl.when(i == 0)
            def _():
                idx_copy(0, 0).start()
                idx_copy(0, 0).wait()
                gather(0)
                if n_steps > 1:
                    idx_copy(1, 1).start()

        if n_steps > 1:
            @pl.when(i + 1 < n_steps)
            def _():
                idx_copy(i + 1, nxt).wait()
                gather(nxt)

            if n_steps > 2:
                @pl.when(i + 2 < n_steps)
                def _():
                    idx_copy(i + 2, cur).start()

        pltpu.make_async_copy(ys_hbm.at[pl.ds(0, 2 * tm)], gbuf.at[cur], sem_g.at[cur]).wait()
        route = route_ref[...]
        rows = gbuf[cur]
        y = y_ref[...] + route[:, 2:3] * rows[0:tm] + route[:, 3:4] * rows[tm:]
        o_ref[...] = _rms(y, w_ref[...])

    pl.when(i % 2 == 0)(lambda: step(0))
    pl.when(i % 2 == 1)(lambda: step(1))


def _final(ymid, route, dest_tbl, ys, w, tm):
    n = ymid.shape[0]
    n_steps = n // tm
    return pl.pallas_call(
        functools.partial(_final_kernel, tm=tm, n_steps=n_steps),
        grid=(n_steps,),
        in_specs=[pl.BlockSpec(memory_space=pl.ANY), pl.BlockSpec(memory_space=pl.ANY),
                  pl.BlockSpec((tm, D_MODEL), lambda i: (i, 0)), pl.BlockSpec((tm, 8), lambda i: (i, 0)),
                  pl.BlockSpec((1, D_MODEL), lambda i: (0, 0))],
        out_specs=pl.BlockSpec((tm, D_MODEL), lambda i: (i, 0)),
        out_shape=jax.ShapeDtypeStruct((n, D_MODEL), F32),
        scratch_shapes=[pltpu.SMEM((2, 2 * tm), jnp.int32), pltpu.VMEM((2, 2 * tm, D_MODEL), F32),
                        pltpu.SemaphoreType.DMA((2,)), pltpu.SemaphoreType.DMA((2,))],
        compiler_params=pltpu.CompilerParams(dimension_semantics=("arbitrary",),
                                             vmem_limit_bytes=VMEM_LIMIT),
        name="final",
    )(dest_tbl, ys, ymid, route, w)


def kernel(x_prompt, x_sample, cache_k, cache_v, state_rec, w_in, lam_q1, lam_k1, lam_q2, lam_k2,
           subln_w, lb_param, gnorm_w, w_out, norm1_w, norm2_w, w_group, w_router,
           w_e_gate, w_e_up, w_e_down, final_w):
    assert w_in.shape[0] == 1 and lb_param.shape[0] == 2, "single-layer model"
    bp, sp, _ = x_prompt.shape
    bs, ts, _ = x_sample.shape
    past = cache_k.shape[2]
    assert sp % CHUNK == 0 and past % CHUNK == 0 and ts <= CHUNK and ts & (ts - 1) == 0
    n_p, n_s = bp * sp, bs * ts
    n_all = n_p + n_s
    tm = 512 if (n_p % 512 == 0 and n_s % 512 == 0) else math.gcd(n_p, n_s)

    w_in_bf = w_in[0].astype(BF16)
    w_out_bf = w_out[0].astype(BF16)
    n1 = norm1_w[0].reshape(1, D_MODEL)
    n2 = norm2_w[0].reshape(1, D_MODEL)
    lam_vecs = jnp.stack([lam_q1[0], lam_k1[0], lam_q2[0], lam_k2[0]]).astype(F32)
    sub_w = subln_w[0].reshape(1, HEAD_W)
    gn_w = gnorm_w[0].reshape(1, HEAD_W)
    w_route = jnp.zeros((D_MODEL, ROUTE_W), F32)
    w_route = w_route.at[:, 0:N_GROUPS].set(w_group[0]).at[:, N_GROUPS:N_GROUPS + N_EXPERTS].set(w_router[0])
    wr_hi = w_route.astype(BF16)
    wr = jnp.concatenate([wr_hi, (w_route - wr_hi.astype(F32)).astype(BF16)], axis=1)

    xp = x_prompt.reshape(n_p, D_MODEL)
    xs = x_sample.reshape(n_s, D_MODEL)

    kf, vf, kb, qr, logf, kr, ir, gr, qt, vt = _inproj(xp, n1, w_in_bf, lb_param, tm, True)
    a_p = _attn_prompt(qt, kb, vt, lam_vecs, sub_w.reshape(HEAD_W, 1), bp, sp)
    chunks = 8 if sp % (8 * CHUNK) == 0 else 1
    r_p, state_p = _hgrn(qr, logf, kr, ir, gr, gn_w, None, bp, sp, CHUNK, chunks)
    ymid_p, h2_p, route_p, counts_p = _merge(xp, a_p, r_p, w_out_bf, n2, wr,
                                             jnp.zeros((1, ROUTE_W), F32), tm)
    k_prompt = kf.reshape(1, bp, sp, A_HEADS, HEAD_W)
    v_prompt = vf.reshape(1, bp, sp, A_HEADS, HEAD_W)

    kf, vf, kb, qr, logf, kr, ir, gr, q, vb = _inproj(xs, n1, w_in_bf, lb_param, tm, False)
    cache_rows = (bs, past * A_HEADS, HEAD_W)
    a_s = _attn_sample(q, cache_k.reshape(cache_rows), cache_v.reshape(cache_rows), kb, vb,
                       lam_vecs, sub_w, bs, ts)
    r_s, state_s = _hgrn(qr, logf, kr, ir, gr, gn_w, state_rec[0], bs, ts, ts, 1)
    ymid_s, h2_s, route_s, counts = _merge(xs, a_s, r_s, w_out_bf, n2, wr, counts_p, tm)
    k_sample = kf.reshape(1, bs, ts, A_HEADS, HEAD_W)
    v_sample = vf.reshape(1, bs, ts, A_HEADS, HEAD_W)

    pstarts, pends, pcounts, blk_exp, n_used, n_rows = _expert_layout(counts, n_all)
    td = min(512, math.gcd(n_p, n_s))
    dest_p = _dest_table(route_p, pstarts, td)
    dest_s = _dest_table(route_s, pstarts, td)
    xs_rows = _dispatch(h2_p, h2_s, jnp.concatenate([dest_p, dest_s], axis=0), pends, pcounts, n_used,
                        n_rows, td)
    ys_rows = _experts(xs_rows, blk_exp, n_used, w_e_gate[0], w_e_up[0], w_e_down[0])

    fw = final_w.reshape(1, D_MODEL)
    y_prompt = _final(ymid_p, route_p, dest_p, ys_rows, fw, td).reshape(bp, sp, D_MODEL)
    y_sample = _final(ymid_s, route_s, dest_s, ys_rows, fw, td).reshape(bs, ts, D_MODEL)
    return (y_prompt, y_sample, k_prompt, v_prompt, state_p[None], k_sample, v_sample, state_s[None])
```

```python
import functools
import math

import numpy as np
import jax
import jax.numpy as jnp
from jax import lax
from jax.experimental import pallas as pl
from jax.experimental.pallas import tpu as pltpu

F32 = jnp.float32
BF16 = jnp.bfloat16

D_MODEL = 1024
RMS_EPS = 1e-6
CHUNK = 64
A_HEADS = 4
A_HEAD_DIM = 64
HEAD_W = 2 * A_HEAD_DIM
KEY_BLOCK = 256
VT_ROWS = HEAD_W + 16
LOG2_E = math.log2(math.e)
R_HEADS = 4
SEG_W = 512
N_SEG = 7
N_GROUPS = 4
EXPERTS_PER_GROUP = 8
N_EXPERTS = N_GROUPS * EXPERTS_PER_GROUP
TOP_K = 2
D_EXPERT = 512
EXPERT_ROWS = 256
ROUTE_W = 128
LAM_INIT = 0.8 - 0.6 * math.exp(-0.3 * 0)
VMEM_LIMIT = 56 * 1024 * 1024


def _sigmoid(x):
    return 1.0 / (1.0 + jnp.exp(-x))


def _dot(a, b):
    return jnp.dot(a, b, preferred_element_type=F32)


def _dot_nt(a, b):
    return lax.dot_general(a, b, (((1,), (1,)), ((), ())), preferred_element_type=F32)


def _dot_tn(a, b):
    return lax.dot_general(a, b, (((0,), (0,)), ((), ())), preferred_element_type=F32)


def _rms(x, w):
    return x * lax.rsqrt(jnp.mean(x * x, axis=-1, keepdims=True) + RMS_EPS) * w


def _inproj_kernel(x_ref, n1_ref, w_ref, lbp_ref, kf_ref, vf_ref, kb_ref, qr_ref, logf_ref, kr_ref,
                   ir_ref, gr_ref, qa_ref, va_ref, h_sc, *, transposed_qv):
    p = lbp_ref[...]
    e = jnp.exp(p - jnp.max(p, axis=0, keepdims=True))
    lb = e[0:1] / jnp.sum(e, axis=0, keepdims=True)
    ones = jnp.ones((VT_ROWS - HEAD_W, KEY_BLOCK), BF16)

    sub = min(KEY_BLOCK, x_ref.shape[0])
    for s in range(x_ref.shape[0] // sub):
        rows = slice(s * sub, (s + 1) * sub)
        h_sc[rows, :] = _rms(x_ref[rows, :], n1_ref[...]).astype(BF16)

        def seg(i, rows=rows):
            return _dot(h_sc[rows, :], w_ref[:, i * SEG_W:(i + 1) * SEG_W])

        q = seg(0) * (A_HEAD_DIM ** -0.5)
        k = seg(1)
        for hd in range(A_HEADS):
            kf_ref[rows, hd, :] = k[:, hd * HEAD_W:(hd + 1) * HEAD_W]
        kb_ref[rows, :] = k.astype(BF16)
        v = seg(2)
        for hd in range(A_HEADS):
            vf_ref[rows, hd, :] = v[:, hd * HEAD_W:(hd + 1) * HEAD_W]
        if transposed_qv:
            qa_ref[s] = (q * LOG2_E).T.astype(BF16)
            vt = v.T.astype(BF16)
            for hd in range(A_HEADS):
                va_ref[s, hd * VT_ROWS:hd * VT_ROWS + HEAD_W, :] = vt[hd * HEAD_W:(hd + 1) * HEAD_W, :]
                va_ref[s, hd * VT_ROWS + HEAD_W:(hd + 1) * VT_ROWS, :] = ones
        else:
            qa_ref[rows, :] = q.astype(BF16)
            va_ref[rows, :] = v.astype(BF16)
        qr = seg(3)
        qr_ref[rows, :] = (qr * _sigmoid(qr)).astype(BF16)
        f = lb + (1.0 - lb) * _sigmoid(seg(4))
        logf_ref[rows, :] = jnp.log(f) * LOG2_E
        kr_ref[rows, :] = (1.0 - f).astype(BF16)
        ir_ref[rows, :] = seg(5).astype(BF16)
        gr_ref[rows, :] = seg(6)


def _inproj(x, n1, w_bf, lbp, tm, transposed_qv):
    n = x.shape[0]
    row = lambda i: (i, 0)
    fix = lambda i: (0, 0)
    out = lambda dt: jax.ShapeDtypeStruct((n, SEG_W), dt)
    ospec = pl.BlockSpec((tm, SEG_W), row)
    hspec = pl.BlockSpec((tm, A_HEADS, HEAD_W), lambda i: (i, 0, 0))
    hout = jax.ShapeDtypeStruct((n, A_HEADS, HEAD_W), F32)
    slabs = tm // KEY_BLOCK
    tspec = lambda rows: pl.BlockSpec((slabs, rows, KEY_BLOCK), lambda i: (i, 0, 0))
    tout = lambda rows: jax.ShapeDtypeStruct((n // KEY_BLOCK, rows, KEY_BLOCK), BF16)
    if transposed_qv:
        qv_specs = [tspec(SEG_W), tspec(A_HEADS * VT_ROWS)]
        qv_shapes = [tout(SEG_W), tout(A_HEADS * VT_ROWS)]
    else:
        qv_specs = [ospec, ospec]
        qv_shapes = [out(BF16), out(BF16)]
    return pl.pallas_call(
        functools.partial(_inproj_kernel, transposed_qv=transposed_qv),
        grid=(n // tm,),
        in_specs=[pl.BlockSpec((tm, D_MODEL), row), pl.BlockSpec((1, D_MODEL), fix),
                  pl.BlockSpec((D_MODEL, N_SEG * SEG_W), fix), pl.BlockSpec(lbp.shape, fix)],
        out_specs=[hspec, hspec] + [ospec] * 6 + qv_specs,
        out_shape=[hout, hout, out(BF16), out(BF16), out(F32), out(BF16), out(BF16), out(F32)] + qv_shapes,
        scratch_shapes=[pltpu.VMEM((tm, D_MODEL), BF16)],
        compiler_params=pltpu.CompilerParams(dimension_semantics=("arbitrary",),
                                             vmem_limit_bytes=VMEM_LIMIT),
        name="inproj",
    )(x, n1, w_bf, lbp)


def _lam_value(lam_ref):
    l = lam_ref[...]
    s1 = jnp.sum(l[0:1] * l[1:2], axis=-1, keepdims=True)
    s2 = jnp.sum(l[2:3] * l[3:4], axis=-1, keepdims=True)
    return jnp.exp(s1) - jnp.exp(s2) + LAM_INIT


def _split_maps(q):
    lane = lax.broadcasted_iota(jnp.int32, q.shape, 1)
    zero = jnp.zeros_like(q)
    return jnp.concatenate([jnp.where(lane < A_HEAD_DIM, q, zero),
                            jnp.where(lane >= A_HEAD_DIM, q, zero)], axis=0)


def _attn_finish(o_num, l, lam, sub_w, tq):
    o = o_num[0:tq] / l[0:tq] - lam * (o_num[tq:] / l[tq:])
    return (_rms(o, sub_w) * (1.0 - LAM_INIT)).astype(BF16)


def _attn_prompt_kernel(lam_ref, sub_ref, bias_ref, qt_ref, k_ref, vt_ref, o_ref,
                        qbd_sc, m_sc, acc_sc, sa_sc, sb_sc, *, qblocks):
    i = pl.program_id(2)
    kb = KEY_BLOCK

    def scores(r, key_block):
        return _dot(k_ref[pl.ds(pl.multiple_of(key_block * kb, kb), kb), :], qbd_sc[r])

    def update(r, key_block, st):
        m_old = m_sc[r:r + 1, :]
        m_new = jnp.maximum(m_old, jnp.max(st, axis=0, keepdims=True))
        p = jnp.exp2(st - m_new).astype(BF16)
        acc_sc[r] = jnp.exp2(m_old - m_new) * acc_sc[r] + _dot(vt_ref[key_block], p)
        m_sc[r:r + 1, :] = m_new

    n_q = 2 * qblocks
    first = i * qblocks
    sub = lax.broadcasted_iota(jnp.int32, (HEAD_W, kb), 0)
    for r in range(n_q):
        qt = qt_ref[r % qblocks]
        keep = (sub < A_HEAD_DIM) if r < qblocks else (sub >= A_HEAD_DIM)
        qbd_sc[r] = jnp.where(keep, qt, jnp.zeros_like(qt))
        sa_sc[r] = scores(r, 0)
    m_sc[...] = jnp.full(m_sc.shape, -jnp.inf, F32)
    acc_sc[...] = jnp.zeros(acc_sc.shape, F32)

    def visible_blocks(start, count):
        lead = 2
        chains = [(j, r) for j in range(count) for r in range(n_q)]
        for c in range(len(chains) + lead):
            if c < len(chains):
                j, r = chains[c]
                (sb_sc if j % 2 == 0 else sa_sc)[r] = scores(r, start + j + 1)
            if c >= lead:
                j, r = chains[c - lead]
                update(r, start + j, (sa_sc if j % 2 == 0 else sb_sc)[r])

    def tile_pair(t, carry):
        visible_blocks(2 * t * qblocks, 2 * qblocks)
        return carry

    lax.fori_loop(0, i // 2, tile_pair, 0)

    def diagonal_tile():
        work = [(kl, r) for kl in range(qblocks) for r in range(n_q) if r % qblocks >= kl]
        buffer_of = lambda kl: sa_sc if kl % 2 == 0 else sb_sc
        ahead = 4
        issued = set()
        for n, (kl, r) in enumerate(work):
            for kl2, r2 in work[n:n + 1 + ahead]:
                if kl2 > 0 and (kl2, r2) not in issued:
                    assert (kl2 - 2, r2) not in work[n:], "score buffer reused before it was read"
                    buffer_of(kl2)[r2] = scores(r2, first + kl2)
                    issued.add((kl2, r2))
            st = buffer_of(kl)[r]
            update(r, first + kl, st + bias_ref[...] if r % qblocks == kl else st)

    @pl.when(i % 2 == 1)
    def _():
        visible_blocks((i - 1) * qblocks, qblocks)
        diagonal_tile()

    pl.when(i % 2 == 0)(diagonal_tile)

    lam = _lam_value(lam_ref)
    for ql in range(qblocks):
        a1 = acc_sc[ql]
        a2 = acc_sc[qblocks + ql]
        ot = a1[0:HEAD_W] / a1[HEAD_W:HEAD_W + 1] - lam * (a2[0:HEAD_W] / a2[HEAD_W:HEAD_W + 1])
        ot = ot * lax.rsqrt(jnp.mean(ot * ot, axis=0, keepdims=True) + RMS_EPS) * sub_ref[...]
        o_ref[ql * kb:(ql + 1) * kb, :] = (ot * (1.0 - LAM_INIT)).T.astype(BF16)


def _attn_prompt(qt, k, vt, lam_vecs, sub_col, batch, seq):
    kb = KEY_BLOCK
    tq = min(1024, seq)
    qblocks = tq // kb
    assert qblocks % 2 == 0, "the key-block loop is unrolled by two"
    nq = seq // tq
    pos = np.arange(kb) // CHUNK
    bias = jnp.asarray(np.where(pos[:, None] <= pos[None, :], 0.0, -np.inf), F32)
    fix = lambda b, h, i: (0, 0)
    return pl.pallas_call(
        functools.partial(_attn_prompt_kernel, qblocks=qblocks),
        grid=(batch, A_HEADS, nq),
        in_specs=[pl.BlockSpec((4, A_HEAD_DIM), fix), pl.BlockSpec((HEAD_W, 1), fix),
                  pl.BlockSpec((kb, kb), fix),
                  pl.BlockSpec((qblocks, HEAD_W, kb), lambda b, h, i: (b * nq + i, h, 0)),
                  pl.BlockSpec((seq, HEAD_W), lambda b, h, i: (b, h)),
                  pl.BlockSpec((seq // kb, VT_ROWS, kb), lambda b, h, i: (b, h, 0))],
        out_specs=pl.BlockSpec((tq, HEAD_W), lambda b, h, i: (b * nq + i, h)),
        out_shape=jax.ShapeDtypeStruct(k.shape, BF16),
        scratch_shapes=[pltpu.VMEM((2 * qblocks, HEAD_W, kb), BF16), pltpu.VMEM((2 * qblocks, kb), F32),
                        pltpu.VMEM((2 * qblocks, VT_ROWS, kb), F32),
                        pltpu.VMEM((2 * qblocks, kb, kb), F32), pltpu.VMEM((2 * qblocks, kb, kb), F32)],
        compiler_params=pltpu.CompilerParams(
            dimension_semantics=("arbitrary", "arbitrary", "arbitrary"), vmem_limit_bytes=VMEM_LIMIT),
        name="attn_prompt",
    )(lam_vecs, sub_col, bias, qt, k, vt)


def _attn_sample_kernel(lam_ref, sub_ref, bias_c_ref, bias_n_ref, q_ref, kc_ref, vc_ref, kn_ref, vn_ref,
                        o_ref, *, t):
    lam = _lam_value(lam_ref)
    heads = [slice(h * HEAD_W, (h + 1) * HEAD_W) for h in range(A_HEADS)]
    qbd = jnp.concatenate([_split_maps(q_ref[:, cs]) for cs in heads], axis=0)
    keys = [kc_ref[0].astype(BF16), jnp.concatenate([kn_ref[:, cs] for cs in heads], axis=0)]
    vals = [vc_ref[0].astype(BF16), jnp.concatenate([vn_ref[:, cs] for cs in heads], axis=0)]
    scores = [_dot_nt(qbd, k) + b[...] for k, b in zip(keys, (bias_c_ref, bias_n_ref))]
    m = functools.reduce(jnp.maximum, [jnp.max(sc, axis=-1, keepdims=True) for sc in scores])
    probs = [jnp.exp(sc - m) for sc in scores]
    l = sum(jnp.sum(p, axis=-1, keepdims=True) for p in probs)
    o_num = sum(_dot(p.astype(BF16), v) for p, v in zip(probs, vals))
    for h, cs in enumerate(heads):
        rows = slice(h * 2 * t, (h + 1) * 2 * t)
        o_ref[:, cs] = _attn_finish(o_num[rows], l[rows], lam, sub_ref[...], t)


def _attn_sample(q, cache_k, cache_v, k_new, v_new, lam_vecs, sub_w, batch, t):
    fix = lambda b: (0, 0)
    row = lambda b: (b, 0)
    cache = pl.BlockSpec((1,) + cache_k.shape[1:], lambda b: (b, 0, 0))
    q_head = np.arange(A_HEADS * 2 * t) // (2 * t)
    off = lambda key_head: jnp.asarray(np.where(q_head[:, None] == key_head[None, :], 0.0, -np.inf), F32)
    bias_c = off(np.arange(cache_k.shape[1]) % A_HEADS)
    bias_n = off(np.arange(A_HEADS * t) // t)
    return pl.pallas_call(
        functools.partial(_attn_sample_kernel, t=t),
        grid=(batch,),
        in_specs=[pl.BlockSpec((4, A_HEAD_DIM), fix), pl.BlockSpec((1, HEAD_W), fix),
                  pl.BlockSpec(bias_c.shape, fix), pl.BlockSpec(bias_n.shape, fix),
                  pl.BlockSpec((t, SEG_W), row),
                  cache, cache, pl.BlockSpec((t, SEG_W), row), pl.BlockSpec((t, SEG_W), row)],
        out_specs=pl.BlockSpec((t, SEG_W), row),
        out_shape=jax.ShapeDtypeStruct(q.shape, BF16),
        compiler_params=pltpu.CompilerParams(dimension_semantics=("arbitrary",),
                                             vmem_limit_bytes=VMEM_LIMIT),
        name="attn_sample",
    )(lam_vecs, sub_w, bias_c, bias_n, q, cache_k, cache_v, k_new, v_new)


def _hgrn_consts(length):
    t = np.arange(length)[:, None]
    j = np.arange(length)[None, :]
    sums = [j <= t]
    masks = [j == t]
    blk = length
    while blk >= 2:
        half = blk // 2
        mid_t = (t // blk) * blk + half
        mid_j = (j // blk) * blk + half
        sums.append(np.where(t >= mid_t, (j >= mid_t) & (j <= t), (j > t) & (j < mid_t)))
        masks.append((t // blk == j // blk) & (t >= mid_t) & (j < mid_j))
        blk = half
    sums = np.tile(np.concatenate(sums, axis=0).astype(np.float32), (1, 2))
    masks = np.concatenate(masks, axis=0).astype(np.float32)
    return jnp.asarray(sums, BF16), jnp.asarray(masks, F32)


def _hgrn_kernel(*refs, length, chunks, group, has_state_in):
    if has_state_in:
        (sums_ref, masks_ref, gn_ref, q_ref, logf_ref, k_ref, v_ref, g_ref, s0_ref,
         r_ref, sout_ref, st_sc) = refs
    else:
        (sums_ref, masks_ref, gn_ref, q_ref, logf_ref, k_ref, v_ref, g_ref,
         r_ref, sout_ref, st_sc) = refs
    step = pl.program_id(1)
    levels = int(math.log2(length))
    L = length

    @pl.when(step == 0)
    def _():
        for h in range(R_HEADS):
            if has_state_in:
                st_sc[h] = s0_ref[0, h].T
            else:
                st_sc[h] = jnp.zeros(st_sc.shape[1:], F32)

    heads = [slice(h * HEAD_W, (h + 1) * HEAD_W) for h in range(R_HEADS)]

    def chunk_group(c, carry):
        rows = [pl.ds(pl.multiple_of((c * group + g) * L, L), L) for g in range(group)]
        expo = []
        for g in range(group):
            logf = logf_ref[rows[g], :]
            hi = logf.astype(BF16)
            lo = (logf - hi.astype(F32)).astype(BF16)
            expo.append(_dot(sums_ref[...], jnp.concatenate([hi, lo], axis=0)))
        streams = [(g, h) for g in range(group) for h in range(R_HEADS)]
        intra, q_dec, upd = {}, {}, {}
        for g, h in streams:
            cs = heads[h]
            q = q_ref[rows[g], cs]
            k = k_ref[rows[g], cs]
            v = v_ref[rows[g], cs]
            a = masks_ref[0:L, :] * _dot_nt(q, k)
            for lv in range(levels):
                x = jnp.exp2(expo[g][(1 + lv) * L:(2 + lv) * L, cs]).astype(BF16)
                a = a + masks_ref[(1 + lv) * L:(2 + lv) * L, :] * _dot_nt(q * x, k * x)
            intra[g, h] = _dot(a.astype(BF16), v)
            b = expo[g][0:L, cs]
            q_dec[g, h] = q * jnp.exp2(b).astype(BF16)
            k_dec = k * jnp.exp2(b[L - 1:L, :] - b).astype(BF16)
            upd[g, h] = _dot_tn(v, k_dec)
        for g, h in streams:
            cs = heads[h]
            st = st_sc[h]
            o = intra[g, h] + _dot_nt(q_dec[g, h], st.astype(BF16))
            st_sc[h] = st * jnp.exp2(expo[g][L - 1:L, cs]) + upd[g, h]
            gate = g_ref[rows[g], cs]
            r_ref[rows[g], cs] = (_rms(o, gn_ref[...]) * (gate * _sigmoid(gate))).astype(BF16)
        return carry

    lax.fori_loop(0, chunks // group, chunk_group, 0)

    @pl.when(step == pl.num_programs(1) - 1)
    def _():
        for h in range(R_HEADS):
            sout_ref[0, h] = st_sc[h].T


def _hgrn(q, logf, k, v, g, gn_w, state_in, batch, seq, length, chunks):
    sums, masks = _hgrn_consts(length)
    group = math.gcd(chunks, 8)
    tm = length * chunks
    steps = seq // tm
    fix = lambda b, s: (0, 0)
    row = lambda b, s: (b * steps + s, 0)
    tok = pl.BlockSpec((tm, SEG_W), row)
    in_specs = [pl.BlockSpec(sums.shape, fix), pl.BlockSpec(masks.shape, fix), pl.BlockSpec((1, HEAD_W), fix),
                tok, tok, tok, tok, tok]
    args = [sums, masks, gn_w, q, logf, k, v, g]
    state_spec = pl.BlockSpec((1, R_HEADS, HEAD_W, HEAD_W), lambda b, s: (b, 0, 0, 0))
    if state_in is not None:
        in_specs.append(state_spec)
        args.append(state_in)
    return pl.pallas_call(
        functools.partial(_hgrn_kernel, length=length, chunks=chunks, group=group,
                          has_state_in=state_in is not None),
        grid=(batch, steps),
        in_specs=in_specs,
        out_specs=[tok, state_spec],
        out_shape=[jax.ShapeDtypeStruct(q.shape, BF16),
                   jax.ShapeDtypeStruct((batch, R_HEADS, HEAD_W, HEAD_W), F32)],
        scratch_shapes=[pltpu.VMEM((R_HEADS, HEAD_W, HEAD_W), F32)],
        compiler_params=pltpu.CompilerParams(dimension_semantics=("arbitrary", "arbitrary"),
                                             vmem_limit_bytes=VMEM_LIMIT),
        name="hgrn",
    )(*args)


def _merge_kernel(x_ref, a_ref, r_ref, wo_ref, n2_ref, wr_ref, tri_ref, cnt0_ref,
                  ymid_ref, h2_ref, route_ref, cnt_ref, cnt_sc):
    step = pl.program_id(0)

    @pl.when(step == 0)
    def _():
        cnt_sc[...] = cnt0_ref[...]

    sub = tri_ref.shape[0]
    blocks = [slice(s * sub, (s + 1) * sub) for s in range(x_ref.shape[0] // sub)]
    half = a_ref.shape[1]
    for rows in blocks:
        ymid_ref[rows, :] = (x_ref[rows, :] + _dot(a_ref[rows, :], wo_ref[0:half, :])
                             + _dot(r_ref[rows, :], wo_ref[half:, :]))
    for rows in blocks:
        _route_rows(rows, n2_ref, wr_ref, tri_ref, ymid_ref, h2_ref, route_ref, cnt_sc)
    cnt_ref[...] = cnt_sc[...]


def _route_rows(rows, n2_ref, wr_ref, tri_ref, ymid_ref, h2_ref, route_ref, cnt_sc):
    hn = _rms(ymid_ref[rows, :], n2_ref[...])
    h2_ref[rows, :] = hn
    hi = hn.astype(BF16)
    lo = (hn - hi.astype(F32)).astype(BF16)
    logits = (_dot(hi, wr_ref[:, :ROUTE_W]) + _dot(lo, wr_ref[:, :ROUTE_W])) + _dot(hi, wr_ref[:, ROUTE_W:])
    lane = lax.broadcasted_iota(jnp.int32, logits.shape, 1)
    neg = -jnp.inf
    big = jnp.int32(ROUTE_W)

    def top1(mask):
        val = jnp.max(jnp.where(mask, logits, neg), axis=-1, keepdims=True)
        idx = jnp.min(jnp.where(mask & (logits == val), lane, big), axis=-1, keepdims=True)
        return val, idx

    is_group = lane < N_GROUPS
    g_max, g_sel = top1(is_group)
    p_group = 1.0 / jnp.sum(jnp.where(is_group, jnp.exp(logits - g_max), 0.0), axis=-1, keepdims=True)
    first = N_GROUPS + g_sel * EXPERTS_PER_GROUP
    in_group = (lane >= first) & (lane < first + EXPERTS_PER_GROUP)
    v1, i1 = top1(in_group)
    v2, i2 = top1(in_group & (lane != i1))
    e2 = jnp.exp(v2 - v1)
    gate1 = p_group * (1.0 / (1.0 + e2))
    gate2 = p_group * (e2 / (1.0 + e2))
    hot1 = lane == i1
    hot2 = lane == i2
    hot = jnp.where(hot1 | hot2, 1.0, 0.0)
    before = cnt_sc[...] + _dot(tri_ref[...], hot.astype(BF16))
    rank1 = jnp.sum(jnp.where(hot1, before, 0.0), axis=-1, keepdims=True)
    rank2 = jnp.sum(jnp.where(hot2, before, 0.0), axis=-1, keepdims=True)
    cnt_sc[...] = cnt_sc[...] + jnp.sum(hot, axis=0, keepdims=True)
    col = lax.broadcasted_iota(jnp.int32, (hot.shape[0], route_ref.shape[1]), 1)
    cols = [(i1 - N_GROUPS).astype(F32), (i2 - N_GROUPS).astype(F32), gate1, gate2, rank1, rank2]
    route = jnp.zeros(col.shape, F32)
    for c, val in enumerate(cols):
        route = jnp.where(col == c, val, route)
    route_ref[rows, :] = route


def _merge(x, a, r, wo_bf, n2, wr, counts_in, tm):
    n = x.shape[0]
    row = lambda i: (i, 0)
    fix = lambda i: (0, 0)
    sub = min(256, tm)
    tri = jnp.asarray(np.tril(np.ones((sub, sub), np.float32), -1), BF16)
    return pl.pallas_call(
        _merge_kernel,
        grid=(n // tm,),
        in_specs=[pl.BlockSpec((tm, D_MODEL), row), pl.BlockSpec((tm, SEG_W), row),
                  pl.BlockSpec((tm, SEG_W), row), pl.BlockSpec((D_MODEL, D_MODEL), fix),
                  pl.BlockSpec((1, D_MODEL), fix), pl.BlockSpec((D_MODEL, 2 * ROUTE_W), fix),
                  pl.BlockSpec((sub, sub), fix),
                  pl.BlockSpec((1, ROUTE_W), fix)],
        out_specs=[pl.BlockSpec((tm, D_MODEL), row), pl.BlockSpec((tm, D_MODEL), row),
                   pl.BlockSpec((tm, 8), row), pl.BlockSpec((1, ROUTE_W), fix)],
        out_shape=[jax.ShapeDtypeStruct((n, D_MODEL), F32), jax.ShapeDtypeStruct((n, D_MODEL), F32),
                   jax.ShapeDtypeStruct((n, 8), F32), jax.ShapeDtypeStruct((1, ROUTE_W), F32)],
        scratch_shapes=[pltpu.VMEM((1, ROUTE_W), F32)],
        compiler_params=pltpu.CompilerParams(dimension_semantics=("arbitrary",),
                                             vmem_limit_bytes=VMEM_LIMIT),
        name="merge",
    )(x, a, r, wo_bf, n2, wr, tri, counts_in)


def _dispatch_kernel(pend_ref, pcnt_ref, n_used_ref, dest_hbm, hp_ref, hs_ref, xs_hbm,
                     idx_sm, zero_sc, sem_idx, sem_out, *, tm, steps_p, steps_s, first_spare, n_blocks):
    i = pl.program_id(0)
    slot = i % 2
    n_steps = steps_p + steps_s

    def idx_copy(step, s):
        return pltpu.make_async_copy(dest_hbm.at[step], idx_sm.at[s], sem_idx.at[s])

    @pl.when(i == 0)
    def _():
        idx_copy(0, 0).start()
        zero_sc[...] = jnp.zeros(zero_sc.shape, F32)

        def fill(start):
            start = pl.multiple_of(start, EXPERT_ROWS)
            return pltpu.make_async_copy(zero_sc, xs_hbm.at[pl.ds(start, EXPERT_ROWS)], sem_out)

        for action in ("start", "wait"):
            for e in range(N_EXPERTS):
                pl.when(pcnt_ref[e] > 0)(
                    lambda e=e: getattr(fill(pend_ref[e] - EXPERT_ROWS), action)())
            for blk in range(first_spare, n_blocks):
                pl.when(blk >= n_used_ref[0])(
                    lambda blk=blk: getattr(fill(blk * EXPERT_ROWS), action)())

    idx_copy(i, slot).wait()

    @pl.when(i + 1 < n_steps)
    def _():
        idx_copy(i + 1, 1 - slot).start()

    def scatter(h_ref, idx_slot):
        for r in range(2 * tm):
            pltpu.make_async_copy(h_ref.at[pl.ds(r % tm, 1)], xs_hbm.at[pl.ds(idx_sm[idx_slot, r], 1)],
                                  sem_out).start(priority=r % 2)
        for _ in range(2):
            pltpu.make_async_copy(h_ref, xs_hbm.at[pl.ds(0, tm)], sem_out).wait()

    for parity in range(2):
        pl.when((i < steps_p) & (slot == parity))(lambda p=parity: scatter(hp_ref, p))
        pl.when((i >= steps_p) & (slot == parity))(lambda p=parity: scatter(hs_ref, p))


def _dispatch(h2_p, h2_s, dest_tbl, pends, pcounts, n_used, n_rows, tm):
    steps_p = h2_p.shape[0] // tm
    steps_s = h2_s.shape[0] // tm
    n_blocks = n_rows // EXPERT_ROWS
    first_spare = (h2_p.shape[0] + h2_s.shape[0]) * TOP_K // EXPERT_ROWS
    grid_spec = pltpu.PrefetchScalarGridSpec(
        num_scalar_prefetch=3, grid=(steps_p + steps_s,),
        in_specs=[pl.BlockSpec(memory_space=pl.ANY),
                  pl.BlockSpec((tm, D_MODEL), lambda i, pe, pc, nu: (jnp.minimum(i, steps_p - 1), 0)),
                  pl.BlockSpec((tm, D_MODEL), lambda i, pe, pc, nu: (jnp.maximum(i - steps_p, 0), 0))],
        out_specs=pl.BlockSpec(memory_space=pl.ANY),
        scratch_shapes=[pltpu.SMEM((2, 2 * tm), jnp.int32), pltpu.VMEM((EXPERT_ROWS, D_MODEL), F32),
                        pltpu.SemaphoreType.DMA((2,)), pltpu.SemaphoreType.DMA])
    return pl.pallas_call(
        functools.partial(_dispatch_kernel, tm=tm, steps_p=steps_p, steps_s=steps_s,
                          first_spare=first_spare, n_blocks=n_blocks),
        grid_spec=grid_spec,
        out_shape=jax.ShapeDtypeStruct((n_rows, D_MODEL), F32),
        compiler_params=pltpu.CompilerParams(dimension_semantics=("arbitrary",),
                                             vmem_limit_bytes=VMEM_LIMIT),
        name="dispatch",
    )(pends, pcounts, n_used, dest_tbl, h2_p, h2_s)


def _experts_kernel(blk_exp_ref, n_used_ref, x_ref, wg_ref, wu_ref, wd_ref, y_ref, wg_sc, wu_sc, wd_sc):
    i = pl.program_id(0)

    @pl.when(i < n_used_ref[0])
    def _():
        prev = blk_exp_ref[jnp.maximum(i - 1, 0)]

        @pl.when((i == 0) | (prev != blk_exp_ref[i]))
        def _():
            wg_sc[...] = wg_ref[0].astype(BF16)
            wu_sc[...] = wu_ref[0].astype(BF16)
            wd_sc[...] = wd_ref[0].astype(BF16)

        xb = x_ref[...].astype(BF16)
        gate = _dot(xb, wg_sc[...])
        up = _dot(xb, wu_sc[...])
        act = (gate * _sigmoid(gate) * up).astype(BF16)
        y_ref[...] = _dot(act, wd_sc[...])

    @pl.when(i >= n_used_ref[0])
    def _():
        y_ref[...] = jnp.zeros(y_ref.shape, F32)


def _experts(xs, blk_exp, n_used, wg, wu, wd):
    n_blocks = int(blk_exp.shape[0])
    wmap = lambda i, be, nu: (be[i], 0, 0)
    xmap = lambda i, be, nu: (jnp.minimum(i, nu[0] - 1), 0)
    grid_spec = pltpu.PrefetchScalarGridSpec(
        num_scalar_prefetch=2,
        grid=(n_blocks,),
        in_specs=[pl.BlockSpec((EXPERT_ROWS, D_MODEL), xmap),
                  pl.BlockSpec((1, D_MODEL, D_EXPERT), wmap),
                  pl.BlockSpec((1, D_MODEL, D_EXPERT), wmap),
                  pl.BlockSpec((1, D_EXPERT, D_MODEL), wmap)],
        out_specs=pl.BlockSpec((EXPERT_ROWS, D_MODEL), lambda i, be, nu: (i, 0)),
        scratch_shapes=[pltpu.VMEM((D_MODEL, D_EXPERT), BF16), pltpu.VMEM((D_MODEL, D_EXPERT), BF16),
                        pltpu.VMEM((D_EXPERT, D_MODEL), BF16)],
    )
    return pl.pallas_call(
        _experts_kernel,
        grid_spec=grid_spec,
        out_shape=jax.ShapeDtypeStruct(xs.shape, F32),
        compiler_params=pltpu.CompilerParams(dimension_semantics=("arbitrary",),
                                             vmem_limit_bytes=VMEM_LIMIT),
        name="experts",
    )(blk_exp, n_used, xs, wg, wu, wd)


def _expert_layout(counts_row, n_tok):
    counts = counts_row[0, N_GROUPS:N_GROUPS + N_EXPERTS].astype(jnp.int32)
    pcounts = (counts + EXPERT_ROWS - 1) // EXPERT_ROWS * EXPERT_ROWS
    pends = jnp.cumsum(pcounts).astype(jnp.int32)
    pstarts = pends - pcounts
    n_blocks = -(-(n_tok * TOP_K) // EXPERT_ROWS) + N_EXPERTS
    blk_start = jnp.arange(n_blocks, dtype=jnp.int32) * EXPERT_ROWS
    blk_exp = jnp.minimum(jnp.sum(blk_start[:, None] >= pends[None, :], axis=1), N_EXPERTS - 1).astype(jnp.int32)
    n_used = (pends[-1:] // EXPERT_ROWS).astype(jnp.int32)
    return pstarts, pends, pcounts, blk_exp, n_used, n_blocks * EXPERT_ROWS


def _dest_kernel(route_ref, start_ref, o_ref, *, td):
    route = route_ref[...]
    tb = route.shape[0]
    lane = lax.broadcasted_iota(jnp.int32, (tb, ROUTE_W), 1)
    diagonal = lax.broadcasted_iota(jnp.int32, (td, td), 0) == lax.broadcasted_iota(jnp.int32, (td, td), 1)
    for k in range(TOP_K):
        expert = route[:, k:k + 1].astype(jnp.int32)
        start = jnp.sum(jnp.where(lane == expert, start_ref[...], 0.0), axis=-1, keepdims=True)
        dest = start + route[:, 2 * TOP_K + k:2 * TOP_K + k + 1]
        for j in range(tb // td):
            as_row = jnp.sum(jnp.where(diagonal, dest[j * td:(j + 1) * td, :], 0.0), axis=0, keepdims=True)
            o_ref[j:j + 1, k * td:(k + 1) * td] = as_row.astype(jnp.int32)


def _dest_table(route, pstarts, td):
    n = route.shape[0]
    tb = math.gcd(n, 8 * td)
    starts = jnp.zeros((1, ROUTE_W), F32).at[0, 0:N_EXPERTS].set(pstarts.astype(F32))
    return pl.pallas_call(
        functools.partial(_dest_kernel, td=td),
        grid=(n // tb,),
        in_specs=[pl.BlockSpec((tb, 8), lambda i: (i, 0)), pl.BlockSpec((1, ROUTE_W), lambda i: (0, 0))],
        out_specs=pl.BlockSpec((tb // td, TOP_K * td), lambda i: (i, 0)),
        out_shape=jax.ShapeDtypeStruct((n // td, TOP_K * td), jnp.int32),
        compiler_params=pltpu.CompilerParams(dimension_semantics=("arbitrary",)),
        name="dest_table",
    )(route, starts)


def _final_kernel(dest_hbm, ys_hbm, y_ref, route_ref, w_ref, o_ref, idx_sm, gbuf, sem_idx, sem_g,
                  *, tm, n_steps):
    i = pl.program_id(0)

    def idx_copy(step, slot):
        return pltpu.make_async_copy(dest_hbm.at[step], idx_sm.at[slot], sem_idx.at[slot])

    def gather(slot):
        for r in range(2 * tm):
            pltpu.make_async_copy(ys_hbm.at[pl.ds(idx_sm[slot, r], 1)], gbuf.at[slot, pl.ds(r, 1)],
                                  sem_g.at[slot]).start(priority=r % 2)

    def step(cur):
        nxt = 1 - cur
        if cur == 0:
            @pl.when(i == 0)
            def _():
                idx_copy(0, 0).start()
                idx_copy(0, 0).wait()
                gather(0)
                if n_steps > 1:
                    idx_copy(1, 1).start()

        if n_steps > 1:
            @pl.when(i + 1 < n_steps)
            def _():
                idx_copy(i + 1, nxt).wait()
                gather(nxt)

            if n_steps > 2:
                @pl.when(i + 2 < n_steps)
                def _():
                    idx_copy(i + 2, cur).start()

        pltpu.make_async_copy(ys_hbm.at[pl.ds(0, 2 * tm)], gbuf.at[cur], sem_g.at[cur]).wait()
        route = route_ref[...]
        rows = gbuf[cur]
        y = y_ref[...] + route[:, 2:3] * rows[0:tm] + route[:, 3:4] * rows[tm:]
        o_ref[...] = _rms(y, w_ref[...])

    pl.when(i % 2 == 0)(lambda: step(0))
    pl.when(i % 2 == 1)(lambda: step(1))


def _final(ymid, route, dest_tbl, ys, w, tm):
    n = ymid.shape[0]
    n_steps = n // tm
    return pl.pallas_call(
        functools.partial(_final_kernel, tm=tm, n_steps=n_steps),
        grid=(n_steps,),
        in_specs=[pl.BlockSpec(memory_space=pl.ANY), pl.BlockSpec(memory_space=pl.ANY),
                  pl.BlockSpec((tm, D_MODEL), lambda i: (i, 0)), pl.BlockSpec((tm, 8), lambda i: (i, 0)),
                  pl.BlockSpec((1, D_MODEL), lambda i: (0, 0))],
        out_specs=pl.BlockSpec((tm, D_MODEL), lambda i: (i, 0)),
        out_shape=jax.ShapeDtypeStruct((n, D_MODEL), F32),
        scratch_shapes=[pltpu.SMEM((2, 2 * tm), jnp.int32), pltpu.VMEM((2, 2 * tm, D_MODEL), F32),
                        pltpu.SemaphoreType.DMA((2,)), pltpu.SemaphoreType.DMA((2,))],
        compiler_params=pltpu.CompilerParams(dimension_semantics=("arbitrary",),
                                             vmem_limit_bytes=VMEM_LIMIT),
        name="final",
    )(dest_tbl, ys, ymid, route, w)


def kernel(x_prompt, x_sample, cache_k, cache_v, state_rec, w_in, lam_q1, lam_k1, lam_q2, lam_k2,
           subln_w, lb_param, gnorm_w, w_out, norm1_w, norm2_w, w_group, w_router,
           w_e_gate, w_e_up, w_e_down, final_w):
    assert w_in.shape[0] == 1 and lb_param.shape[0] == 2, "single-layer model"
    bp, sp, _ = x_prompt.shape
    bs, ts, _ = x_sample.shape
    past = cache_k.shape[2]
    assert sp % CHUNK == 0 and past % CHUNK == 0 and ts <= CHUNK and ts & (ts - 1) == 0
    n_p, n_s = bp * sp, bs * ts
    n_all = n_p + n_s
    tm = 512 if (n_p % 512 == 0 and n_s % 512 == 0) else math.gcd(n_p, n_s)

    w_in_bf = w_in[0].astype(BF16)
    w_out_bf = w_out[0].astype(BF16)
    n1 = norm1_w[0].reshape(1, D_MODEL)
    n2 = norm2_w[0].reshape(1, D_MODEL)
    lam_vecs = jnp.stack([lam_q1[0], lam_k1[0], lam_q2[0], lam_k2[0]]).astype(F32)
    sub_w = subln_w[0].reshape(1, HEAD_W)
    gn_w = gnorm_w[0].reshape(1, HEAD_W)
    w_route = jnp.zeros((D_MODEL, ROUTE_W), F32)
    w_route = w_route.at[:, 0:N_GROUPS].set(w_group[0]).at[:, N_GROUPS:N_GROUPS + N_EXPERTS].set(w_router[0])
    wr_hi = w_route.astype(BF16)
    wr = jnp.concatenate([wr_hi, (w_route - wr_hi.astype(F32)).astype(BF16)], axis=1)

    xp = x_prompt.reshape(n_p, D_MODEL)
    xs = x_sample.reshape(n_s, D_MODEL)

    kf, vf, kb, qr, logf, kr, ir, gr, qt, vt = _inproj(xp, n1, w_in_bf, lb_param, tm, True)
    a_p = _attn_prompt(qt, kb, vt, lam_vecs, sub_w.reshape(HEAD_W, 1), bp, sp)
    chunks = 8 if sp % (8 * CHUNK) == 0 else 1
    r_p, state_p = _hgrn(qr, logf, kr, ir, gr, gn_w, None, bp, sp, CHUNK, chunks)
    ymid_p, h2_p, route_p, counts_p = _merge(xp, a_p, r_p, w_out_bf, n2, wr,
                                             jnp.zeros((1, ROUTE_W), F32), tm)
    k_prompt = kf.reshape(1, bp, sp, A_HEADS, HEAD_W)
    v_prompt = vf.reshape(1, bp, sp, A_HEADS, HEAD_W)

    kf, vf, kb, qr, logf, kr, ir, gr, q, vb = _inproj(xs, n1, w_in_bf, lb_param, tm, False)
    cache_rows = (bs, past * A_HEADS, HEAD_W)
    a_s = _attn_sample(q, cache_k.reshape(cache_rows), cache_v.reshape(cache_rows), kb, vb,
                       lam_vecs, sub_w, bs, ts)
    r_s, state_s = _hgrn(qr, logf, kr, ir, gr, gn_w, state_rec[0], bs, ts, ts, 1)
    ymid_s, h2_s, route_s, counts = _merge(xs, a_s, r_s, w_out_bf, n2, wr, counts_p, tm)
    k_sample = kf.reshape(1, bs, ts, A_HEADS, HEAD_W)
    v_sample = vf.reshape(1, bs, ts, A_HEADS, HEAD_W)

    pstarts, pends, pcounts, blk_exp, n_used, n_rows = _expert_layout(counts, n_all)
    td = min(512, math.gcd(n_p, n_s))
    dest_p = _dest_table(route_p, pstarts, td)
    dest_s = _dest_table(route_s, pstarts, td)
    xs_rows = _dispatch(h2_p, h2_s, jnp.concatenate([dest_p, dest_s], axis=0), pends, pcounts, n_used,
                        n_rows, td)
    ys_rows = _experts(xs_rows, blk_exp, n_used, w_e_gate[0], w_e_up[0], w_e_down[0])

    fw = final_w.reshape(1, D_MODEL)
    y_prompt = _final(ymid_p, route_p, dest_p, ys_rows, fw, td).reshape(bp, sp, D_MODEL)
    y_sample = _final(ymid_s, route_s, dest_s, ys_rows, fw, td).reshape(bs, ts, D_MODEL)
    return (y_prompt, y_sample, k_prompt, v_prompt, state_p[None], k_sample, v_sample, state_s[None])
```

```python
import functools
import math

import numpy as np
import jax
import jax.numpy as jnp
from jax import lax
from jax.experimental import pallas as pl
from jax.experimental.pallas import tpu as pltpu

F32 = jnp.float32
BF16 = jnp.bfloat16

D_MODEL = 1024
RMS_EPS = 1e-6
CHUNK = 64
A_HEADS = 4
A_HEAD_DIM = 64
HEAD_W = 2 * A_HEAD_DIM
KEY_BLOCK = 256
VT_ROWS = HEAD_W + 16
LOG2_E = math.log2(math.e)
R_HEADS = 4
SEG_W = 512
N_SEG = 7
N_GROUPS = 4
EXPERTS_PER_GROUP = 8
N_EXPERTS = N_GROUPS * EXPERTS_PER_GROUP
TOP_K = 2
D_EXPERT = 512
EXPERT_ROWS = 512
ROUTE_W = 128
LAM_INIT = 0.8 - 0.6 * math.exp(-0.3 * 0)
VMEM_LIMIT = 56 * 1024 * 1024


def _sigmoid(x):
    return 1.0 / (1.0 + jnp.exp(-x))


def _dot(a, b):
    return jnp.dot(a, b, preferred_element_type=F32)


def _dot_nt(a, b):
    return lax.dot_general(a, b, (((1,), (1,)), ((), ())), preferred_element_type=F32)


def _dot_tn(a, b):
    return lax.dot_general(a, b, (((0,), (0,)), ((), ())), preferred_element_type=F32)


def _rms(x, w):
    return x * lax.rsqrt(jnp.mean(x * x, axis=-1, keepdims=True) + RMS_EPS) * w


def _inproj_kernel(x_ref, n1_ref, w_ref, lbp_ref, kf_ref, vf_ref, kb_ref, qr_ref, logf_ref, kr_ref,
                   ir_ref, gr_ref, qa_ref, va_ref, h_sc, *, transposed_qv):
    p = lbp_ref[...]
    e = jnp.exp(p - jnp.max(p, axis=0, keepdims=True))
    lb = e[0:1] / jnp.sum(e, axis=0, keepdims=True)
    ones = jnp.ones((VT_ROWS - HEAD_W, KEY_BLOCK), BF16)

    sub = min(KEY_BLOCK, x_ref.shape[0])
    for s in range(x_ref.shape[0] // sub):
        rows = slice(s * sub, (s + 1) * sub)
        h_sc[rows, :] = _rms(x_ref[rows, :], n1_ref[...]).astype(BF16)

        def seg(i, rows=rows):
            return _dot(h_sc[rows, :], w_ref[:, i * SEG_W:(i + 1) * SEG_W])

        q = seg(0) * (A_HEAD_DIM ** -0.5)
        k = seg(1)
        for hd in range(A_HEADS):
            kf_ref[rows, hd, :] = k[:, hd * HEAD_W:(hd + 1) * HEAD_W]
        kb_ref[rows, :] = k.astype(BF16)
        v = seg(2)
        for hd in range(A_HEADS):
            vf_ref[rows, hd, :] = v[:, hd * HEAD_W:(hd + 1) * HEAD_W]
        if transposed_qv:
            qa_ref[s] = (q * LOG2_E).T.astype(BF16)
            vt = v.T.astype(BF16)
            for hd in range(A_HEADS):
                va_ref[s, hd * VT_ROWS:hd * VT_ROWS + HEAD_W, :] = vt[hd * HEAD_W:(hd + 1) * HEAD_W, :]
                va_ref[s, hd * VT_ROWS + HEAD_W:(hd + 1) * VT_ROWS, :] = ones
        else:
            qa_ref[rows, :] = q.astype(BF16)
            va_ref[rows, :] = v.astype(BF16)
        qr = seg(3)
        qr_ref[rows, :] = (qr * _sigmoid(qr)).astype(BF16)
        f = lb + (1.0 - lb) * _sigmoid(seg(4))
        logf_ref[rows, :] = jnp.log(f) * LOG2_E
        kr_ref[rows, :] = (1.0 - f).astype(BF16)
        ir_ref[rows, :] = seg(5).astype(BF16)
        gr_ref[rows, :] = seg(6)


def _inproj(x, n1, w_bf, lbp, tm, transposed_qv):
    n = x.shape[0]
    row = lambda i: (i, 0)
    fix = lambda i: (0, 0)
    out = lambda dt: jax.ShapeDtypeStruct((n, SEG_W), dt)
    ospec = pl.BlockSpec((tm, SEG_W), row)
    hspec = pl.BlockSpec((tm, A_HEADS, HEAD_W), lambda i: (i, 0, 0))
    hout = jax.ShapeDtypeStruct((n, A_HEADS, HEAD_W), F32)
    slabs = tm // KEY_BLOCK
    tspec = lambda rows: pl.BlockSpec((slabs, rows, KEY_BLOCK), lambda i: (i, 0, 0))
    tout = lambda rows: jax.ShapeDtypeStruct((n // KEY_BLOCK, rows, KEY_BLOCK), BF16)
    if transposed_qv:
        qv_specs = [tspec(SEG_W), tspec(A_HEADS * VT_ROWS)]
        qv_shapes = [tout(SEG_W), tout(A_HEADS * VT_ROWS)]
    else:
        qv_specs = [ospec, ospec]
        qv_shapes = [out(BF16), out(BF16)]
    return pl.pallas_call(
        functools.partial(_inproj_kernel, transposed_qv=transposed_qv),
        grid=(n // tm,),
        in_specs=[pl.BlockSpec((tm, D_MODEL), row), pl.BlockSpec((1, D_MODEL), fix),
                  pl.BlockSpec((D_MODEL, N_SEG * SEG_W), fix), pl.BlockSpec(lbp.shape, fix)],
        out_specs=[hspec, hspec] + [ospec] * 6 + qv_specs,
        out_shape=[hout, hout, out(BF16), out(BF16), out(F32), out(BF16), out(BF16), out(F32)] + qv_shapes,
        scratch_shapes=[pltpu.VMEM((tm, D_MODEL), BF16)],
        compiler_params=pltpu.CompilerParams(dimension_semantics=("arbitrary",),
                                             vmem_limit_bytes=VMEM_LIMIT),
        name="inproj",
    )(x, n1, w_bf, lbp)


def _lam_value(lam_ref):
    l = lam_ref[...]
    s1 = jnp.sum(l[0:1] * l[1:2], axis=-1, keepdims=True)
    s2 = jnp.sum(l[2:3] * l[3:4], axis=-1, keepdims=True)
    return jnp.exp(s1) - jnp.exp(s2) + LAM_INIT


def _split_maps(q):
    lane = lax.broadcasted_iota(jnp.int32, q.shape, 1)
    zero = jnp.zeros_like(q)
    return jnp.concatenate([jnp.where(lane < A_HEAD_DIM, q, zero),
                            jnp.where(lane >= A_HEAD_DIM, q, zero)], axis=0)


def _attn_finish(o_num, l, lam, sub_w, tq):
    o = o_num[0:tq] / l[0:tq] - lam * (o_num[tq:] / l[tq:])
    return (_rms(o, sub_w) * (1.0 - LAM_INIT)).astype(BF16)


def _attn_prompt_kernel(lam_ref, sub_ref, bias_ref, qt_ref, k_ref, vt_ref, o_ref,
                        qbd_sc, m_sc, acc_sc, sa_sc, sb_sc, *, qblocks):
    i = pl.program_id(2)
    kb = KEY_BLOCK

    def scores(r, key_block):
        return _dot(k_ref[pl.ds(pl.multiple_of(key_block * kb, kb), kb), :], qbd_sc[r])

    def update(r, key_block, st):
        m_old = m_sc[r:r + 1, :]
        m_new = jnp.maximum(m_old, jnp.max(st, axis=0, keepdims=True))
        p = jnp.exp2(st - m_new).astype(BF16)
        acc_sc[r] = jnp.exp2(m_old - m_new) * acc_sc[r] + _dot(vt_ref[key_block], p)
        m_sc[r:r + 1, :] = m_new

    n_q = 2 * qblocks
    first = i * qblocks
    sub = lax.broadcasted_iota(jnp.int32, (HEAD_W, kb), 0)
    for r in range(n_q):
        qt = qt_ref[r % qblocks]
        keep = (sub < A_HEAD_DIM) if r < qblocks else (sub >= A_HEAD_DIM)
        qbd_sc[r] = jnp.where(keep, qt, jnp.zeros_like(qt))
        sa_sc[r] = scores(r, 0)
    m_sc[...] = jnp.full(m_sc.shape, -jnp.inf, F32)
    acc_sc[...] = jnp.zeros(acc_sc.shape, F32)

    def visible_blocks(start, count):
        lead = 2
        chains = [(j, r) for j in range(count) for r in range(n_q)]
        for c in range(len(chains) + lead):
            if c < len(chains):
                j, r = chains[c]
                (sb_sc if j % 2 == 0 else sa_sc)[r] = scores(r, start + j + 1)
            if c >= lead:
                j, r = chains[c - lead]
                update(r, start + j, (sa_sc if j % 2 == 0 else sb_sc)[r])

    def tile_pair(t, carry):
        visible_blocks(2 * t * qblocks, 2 * qblocks)
        return carry

    lax.fori_loop(0, i // 2, tile_pair, 0)

    def diagonal_tile():
        work = [(kl, r) for kl in range(qblocks) for r in range(n_q) if r % qblocks >= kl]
        buffer_of = lambda kl: sa_sc if kl % 2 == 0 else sb_sc
        ahead = 4
        issued = set()
        for n, (kl, r) in enumerate(work):
            for kl2, r2 in work[n:n + 1 + ahead]:
                if kl2 > 0 and (kl2, r2) not in issued:
                    assert (kl2 - 2, r2) not in work[n:], "score buffer reused before it was read"
                    buffer_of(kl2)[r2] = scores(r2, first + kl2)
                    issued.add((kl2, r2))
            st = buffer_of(kl)[r]
            update(r, first + kl, st + bias_ref[...] if r % qblocks == kl else st)

    @pl.when(i % 2 == 1)
    def _():
        visible_blocks((i - 1) * qblocks, qblocks)
        diagonal_tile()

    pl.when(i % 2 == 0)(diagonal_tile)

    lam = _lam_value(lam_ref)
    for ql in range(qblocks):
        a1 = acc_sc[ql]
        a2 = acc_sc[qblocks + ql]
        ot = a1[0:HEAD_W] / a1[HEAD_W:HEAD_W + 1] - lam * (a2[0:HEAD_W] / a2[HEAD_W:HEAD_W + 1])
        ot = ot * lax.rsqrt(jnp.mean(ot * ot, axis=0, keepdims=True) + RMS_EPS) * sub_ref[...]
        o_ref[ql * kb:(ql + 1) * kb, :] = (ot * (1.0 - LAM_INIT)).T.astype(BF16)


def _attn_prompt(qt, k, vt, lam_vecs, sub_col, batch, seq):
    kb = KEY_BLOCK
    tq = min(1024, seq)
    qblocks = tq // kb
    assert qblocks % 2 == 0, "the key-block loop is unrolled by two"
    nq = seq // tq
    pos = np.arange(kb) // CHUNK
    bias = jnp.asarray(np.where(pos[:, None] <= pos[None, :], 0.0, -np.inf), F32)
    fix = lambda b, h, i: (0, 0)
    return pl.pallas_call(
        functools.partial(_attn_prompt_kernel, qblocks=qblocks),
        grid=(batch, A_HEADS, nq),
        in_specs=[pl.BlockSpec((4, A_HEAD_DIM), fix), pl.BlockSpec((HEAD_W, 1), fix),
                  pl.BlockSpec((kb, kb), fix),
                  pl.BlockSpec((qblocks, HEAD_W, kb), lambda b, h, i: (b * nq + i, h, 0)),
                  pl.BlockSpec((seq, HEAD_W), lambda b, h, i: (b, h)),
                  pl.BlockSpec((seq // kb, VT_ROWS, kb), lambda b, h, i: (b, h, 0))],
        out_specs=pl.BlockSpec((tq, HEAD_W), lambda b, h, i: (b * nq + i, h)),
        out_shape=jax.ShapeDtypeStruct(k.shape, BF16),
        scratch_shapes=[pltpu.VMEM((2 * qblocks, HEAD_W, kb), BF16), pltpu.VMEM((2 * qblocks, kb), F32),
                        pltpu.VMEM((2 * qblocks, VT_ROWS, kb), F32),
                        pltpu.VMEM((2 * qblocks, kb, kb), F32), pltpu.VMEM((2 * qblocks, kb, kb), F32)],
        compiler_params=pltpu.CompilerParams(
            dimension_semantics=("arbitrary", "arbitrary", "arbitrary"), vmem_limit_bytes=VMEM_LIMIT),
        name="attn_prompt",
    )(lam_vecs, sub_col, bias, qt, k, vt)


def _attn_sample_kernel(lam_ref, sub_ref, bias_c_ref, bias_n_ref, q_ref, kc_ref, vc_ref, kn_ref, vn_ref,
                        o_ref, *, t):
    lam = _lam_value(lam_ref)
    heads = [slice(h * HEAD_W, (h + 1) * HEAD_W) for h in range(A_HEADS)]
    qbd = jnp.concatenate([_split_maps(q_ref[:, cs]) for cs in heads], axis=0)
    keys = [kc_ref[0].astype(BF16), jnp.concatenate([kn_ref[:, cs] for cs in heads], axis=0)]
    vals = [vc_ref[0].astype(BF16), jnp.concatenate([vn_ref[:, cs] for cs in heads], axis=0)]
    scores = [_dot_nt(qbd, k) + b[...] for k, b in zip(keys, (bias_c_ref, bias_n_ref))]
    m = functools.reduce(jnp.maximum, [jnp.max(sc, axis=-1, keepdims=True) for sc in scores])
    probs = [jnp.exp(sc - m) for sc in scores]
    l = sum(jnp.sum(p, axis=-1, keepdims=True) for p in probs)
    o_num = sum(_dot(p.astype(BF16), v) for p, v in zip(probs, vals))
    for h, cs in enumerate(heads):
        rows = slice(h * 2 * t, (h + 1) * 2 * t)
        o_ref[:, cs] = _attn_finish(o_num[rows], l[rows], lam, sub_ref[...], t)


def _attn_sample(q, cache_k, cache_v, k_new, v_new, lam_vecs, sub_w, batch, t):
    fix = lambda b: (0, 0)
    row = lambda b: (b, 0)
    cache = pl.BlockSpec((1,) + cache_k.shape[1:], lambda b: (b, 0, 0))
    q_head = np.arange(A_HEADS * 2 * t) // (2 * t)
    off = lambda key_head: jnp.asarray(np.where(q_head[:, None] == key_head[None, :], 0.0, -np.inf), F32)
    bias_c = off(np.arange(cache_k.shape[1]) % A_HEADS)
    bias_n = off(np.arange(A_HEADS * t) // t)
    return pl.pallas_call(
        functools.partial(_attn_sample_kernel, t=t),
        grid=(batch,),
        in_specs=[pl.BlockSpec((4, A_HEAD_DIM), fix), pl.BlockSpec((1, HEAD_W), fix),
                  pl.BlockSpec(bias_c.shape, fix), pl.BlockSpec(bias_n.shape, fix),
                  pl.BlockSpec((t, SEG_W), row),
                  cache, cache, pl.BlockSpec((t, SEG_W), row), pl.BlockSpec((t, SEG_W), row)],
        out_specs=pl.BlockSpec((t, SEG_W), row),
        out_shape=jax.ShapeDtypeStruct(q.shape, BF16),
        compiler_params=pltpu.CompilerParams(dimension_semantics=("arbitrary",),
                                             vmem_limit_bytes=VMEM_LIMIT),
        name="attn_sample",
    )(lam_vecs, sub_w, bias_c, bias_n, q, cache_k, cache_v, k_new, v_new)


def _hgrn_consts(length):
    t = np.arange(length)[:, None]
    j = np.arange(length)[None, :]
    sums = [j <= t]
    masks = [j == t]
    blk = length
    while blk >= 2:
        half = blk // 2
        mid_t = (t // blk) * blk + half
        mid_j = (j // blk) * blk + half
        sums.append(np.where(t >= mid_t, (j >= mid_t) & (j <= t), (j > t) & (j < mid_t)))
        masks.append((t // blk == j // blk) & (t >= mid_t) & (j < mid_j))
        blk = half
    sums = np.tile(np.concatenate(sums, axis=0).astype(np.float32), (1, 2))
    masks = np.concatenate(masks, axis=0).astype(np.float32)
    return jnp.asarray(sums, BF16), jnp.asarray(masks, F32)


def _hgrn_kernel(*refs, length, chunks, group, has_state_in):
    if has_state_in:
        (sums_ref, masks_ref, gn_ref, q_ref, logf_ref, k_ref, v_ref, g_ref, s0_ref,
         r_ref, sout_ref, st_sc) = refs
    else:
        (sums_ref, masks_ref, gn_ref, q_ref, logf_ref, k_ref, v_ref, g_ref,
         r_ref, sout_ref, st_sc) = refs
    step = pl.program_id(1)
    levels = int(math.log2(length))
    L = length

    @pl.when(step == 0)
    def _():
        for h in range(R_HEADS):
            if has_state_in:
                st_sc[h] = s0_ref[0, h].T
            else:
                st_sc[h] = jnp.zeros(st_sc.shape[1:], F32)

    heads = [slice(h * HEAD_W, (h + 1) * HEAD_W) for h in range(R_HEADS)]

    def chunk_group(c, carry):
        rows = [pl.ds(pl.multiple_of((c * group + g) * L, L), L) for g in range(group)]
        expo = []
        for g in range(group):
            logf = logf_ref[rows[g], :]
            hi = logf.astype(BF16)
            lo = (logf - hi.astype(F32)).astype(BF16)
            expo.append(_dot(sums_ref[...], jnp.concatenate([hi, lo], axis=0)))
        streams = [(g, h) for g in range(group) for h in range(R_HEADS)]
        intra, q_dec, upd = {}, {}, {}

        def decay_matrix(g, h):
            cs = heads[h]
            q = q_ref[rows[g], cs]
            k = k_ref[rows[g], cs]
            a = masks_ref[0:L, :] * _dot_nt(q, k)
            for lv in range(levels):
                x = jnp.exp2(expo[g][(1 + lv) * L:(2 + lv) * L, cs]).astype(BF16)
                a = a + masks_ref[(1 + lv) * L:(2 + lv) * L, :] * _dot_nt(q * x, k * x)
            return a

        def state_free_parts(g, h, a):
            cs = heads[h]
            q = q_ref[rows[g], cs]
            k = k_ref[rows[g], cs]
            v = v_ref[rows[g], cs]
            intra[g, h] = _dot(a.astype(BF16), v)
            b = expo[g][0:L, cs]
            q_dec[g, h] = q * jnp.exp2(b).astype(BF16)
            k_dec = k * jnp.exp2(b[L - 1:L, :] - b).astype(BF16)
            upd[g, h] = _dot_tn(v, k_dec)

        for g, h in streams:
            state_free_parts(g, h, decay_matrix(g, h))
        for g, h in streams:
            cs = heads[h]
            st = st_sc[h]
            o = intra[g, h] + _dot_nt(q_dec[g, h], st.astype(BF16))
            st_sc[h] = st * jnp.exp2(expo[g][L - 1:L, cs]) + upd[g, h]
            gate = g_ref[rows[g], cs]
            r_ref[rows[g], cs] = (_rms(o, gn_ref[...]) * (gate * _sigmoid(gate))).astype(BF16)
        return carry

    lax.fori_loop(0, chunks // group, chunk_group, 0)

    @pl.when(step == pl.num_programs(1) - 1)
    def _():
        for h in range(R_HEADS):
            sout_ref[0, h] = st_sc[h].T


def _hgrn(q, logf, k, v, g, gn_w, state_in, batch, seq, length, chunks):
    sums, masks = _hgrn_consts(length)
    group = math.gcd(chunks, 8)
    tm = length * chunks
    steps = seq // tm
    fix = lambda b, s: (0, 0)
    row = lambda b, s: (b * steps + s, 0)
    tok = pl.BlockSpec((tm, SEG_W), row)
    in_specs = [pl.BlockSpec(sums.shape, fix), pl.BlockSpec(masks.shape, fix), pl.BlockSpec((1, HEAD_W), fix),
                tok, tok, tok, tok, tok]
    args = [sums, masks, gn_w, q, logf, k, v, g]
    state_spec = pl.BlockSpec((1, R_HEADS, HEAD_W, HEAD_W), lambda b, s: (b, 0, 0, 0))
    if state_in is not None:
        in_specs.append(state_spec)
        args.append(state_in)
    return pl.pallas_call(
        functools.partial(_hgrn_kernel, length=length, chunks=chunks, group=group,
                          has_state_in=state_in is not None),
        grid=(batch, steps),
        in_specs=in_specs,
        out_specs=[tok, state_spec],
        out_shape=[jax.ShapeDtypeStruct(q.shape, BF16),
                   jax.ShapeDtypeStruct((batch, R_HEADS, HEAD_W, HEAD_W), F32)],
        scratch_shapes=[pltpu.VMEM((R_HEADS, HEAD_W, HEAD_W), F32)],
        compiler_params=pltpu.CompilerParams(dimension_semantics=("arbitrary", "arbitrary"),
                                             vmem_limit_bytes=VMEM_LIMIT),
        name="hgrn",
    )(*args)


def _merge_kernel(x_ref, a_ref, r_ref, wo_ref, n2_ref, wr_ref, tri_ref, cnt0_ref,
                  ymid_ref, h2_ref, route_ref, cnt_ref, cnt_sc):
    step = pl.program_id(0)

    @pl.when(step == 0)
    def _():
        cnt_sc[...] = cnt0_ref[...]

    sub = tri_ref.shape[0]
    blocks = [slice(s * sub, (s + 1) * sub) for s in range(x_ref.shape[0] // sub)]
    half = a_ref.shape[1]
    for rows in blocks:
        ymid_ref[rows, :] = (x_ref[rows, :] + _dot(a_ref[rows, :], wo_ref[0:half, :])
                             + _dot(r_ref[rows, :], wo_ref[half:, :]))
    for rows in blocks:
        _route_rows(rows, n2_ref, wr_ref, tri_ref, ymid_ref, h2_ref, route_ref, cnt_sc)
    cnt_ref[...] = cnt_sc[...]


def _route_rows(rows, n2_ref, wr_ref, tri_ref, ymid_ref, h2_ref, route_ref, cnt_sc):
    hn = _rms(ymid_ref[rows, :], n2_ref[...])
    h2_ref[rows, :] = hn
    hi = hn.astype(BF16)
    lo = (hn - hi.astype(F32)).astype(BF16)
    logits = (_dot(hi, wr_ref[:, :ROUTE_W]) + _dot(lo, wr_ref[:, :ROUTE_W])) + _dot(hi, wr_ref[:, ROUTE_W:])
    lane = lax.broadcasted_iota(jnp.int32, logits.shape, 1)
    neg = -jnp.inf
    big = jnp.int32(ROUTE_W)

    def top1(mask):
        val = jnp.max(jnp.where(mask, logits, neg), axis=-1, keepdims=True)
        idx = jnp.min(jnp.where(mask & (logits == val), lane, big), axis=-1, keepdims=True)
        return val, idx

    is_group = lane < N_GROUPS
    g_max, g_sel = top1(is_group)
    p_group = 1.0 / jnp.sum(jnp.where(is_group, jnp.exp(logits - g_max), 0.0), axis=-1, keepdims=True)
    first = N_GROUPS + g_sel * EXPERTS_PER_GROUP
    in_group = (lane >= first) & (lane < first + EXPERTS_PER_GROUP)
    v1, i1 = top1(in_group)
    v2, i2 = top1(in_group & (lane != i1))
    e2 = jnp.exp(v2 - v1)
    gate1 = p_group * (1.0 / (1.0 + e2))
    gate2 = p_group * (e2 / (1.0 + e2))
    hot1 = lane == i1
    hot2 = lane == i2
    hot = jnp.where(hot1 | hot2, 1.0, 0.0)
    before = cnt_sc[...] + _dot(tri_ref[...], hot.astype(BF16))
    rank1 = jnp.sum(jnp.where(hot1, before, 0.0), axis=-1, keepdims=True)
    rank2 = jnp.sum(jnp.where(hot2, before, 0.0), axis=-1, keepdims=True)
    cnt_sc[...] = cnt_sc[...] + jnp.sum(hot, axis=0, keepdims=True)
    col = lax.broadcasted_iota(jnp.int32, (hot.shape[0], route_ref.shape[1]), 1)
    cols = [(i1 - N_GROUPS).astype(F32), (i2 - N_GROUPS).astype(F32), gate1, gate2, rank1, rank2]
    route = jnp.zeros(col.shape, F32)
    for c, val in enumerate(cols):
        route = jnp.where(col == c, val, route)
    route_ref[rows, :] = route


def _merge(x, a, r, wo_bf, n2, wr, counts_in, tm):
    n = x.shape[0]
    row = lambda i: (i, 0)
    fix = lambda i: (0, 0)
    sub = min(256, tm)
    tri = jnp.asarray(np.tril(np.ones((sub, sub), np.float32), -1), BF16)
    return pl.pallas_call(
        _merge_kernel,
        grid=(n // tm,),
        in_specs=[pl.BlockSpec((tm, D_MODEL), row), pl.BlockSpec((tm, SEG_W), row),
                  pl.BlockSpec((tm, SEG_W), row), pl.BlockSpec((D_MODEL, D_MODEL), fix),
                  pl.BlockSpec((1, D_MODEL), fix), pl.BlockSpec((D_MODEL, 2 * ROUTE_W), fix),
                  pl.BlockSpec((sub, sub), fix),
                  pl.BlockSpec((1, ROUTE_W), fix)],
        out_specs=[pl.BlockSpec((tm, D_MODEL), row), pl.BlockSpec((tm, D_MODEL), row),
                   pl.BlockSpec((tm, 8), row), pl.BlockSpec((1, ROUTE_W), fix)],
        out_shape=[jax.ShapeDtypeStruct((n, D_MODEL), F32), jax.ShapeDtypeStruct((n, D_MODEL), F32),
                   jax.ShapeDtypeStruct((n, 8), F32), jax.ShapeDtypeStruct((1, ROUTE_W), F32)],
        scratch_shapes=[pltpu.VMEM((1, ROUTE_W), F32)],
        compiler_params=pltpu.CompilerParams(dimension_semantics=("arbitrary",),
                                             vmem_limit_bytes=VMEM_LIMIT),
        name="merge",
    )(x, a, r, wo_bf, n2, wr, tri, counts_in)


def _dispatch_kernel(pend_ref, pcnt_ref, n_used_ref, dest_hbm, hp_ref, hs_ref, xs_hbm,
                     idx_sm, zero_sc, sem_idx, sem_out, *, tm, steps_p, steps_s, first_spare, n_blocks):
    i = pl.program_id(0)
    slot = i % 2
    n_steps = steps_p + steps_s

    def idx_copy(step, s):
        return pltpu.make_async_copy(dest_hbm.at[step], idx_sm.at[s], sem_idx.at[s])

    @pl.when(i == 0)
    def _():
        idx_copy(0, 0).start()
        zero_sc[...] = jnp.zeros(zero_sc.shape, F32)

        def fill(start):
            start = pl.multiple_of(start, EXPERT_ROWS)
            return pltpu.make_async_copy(zero_sc, xs_hbm.at[pl.ds(start, EXPERT_ROWS)], sem_out)

        for action in ("start", "wait"):
            for e in range(N_EXPERTS):
                pl.when(pcnt_ref[e] > 0)(
                    lambda e=e: getattr(fill(pend_ref[e] - EXPERT_ROWS), action)())
            for blk in range(first_spare, n_blocks):
                pl.when(blk >= n_used_ref[0])(
                    lambda blk=blk: getattr(fill(blk * EXPERT_ROWS), action)())

    idx_copy(i, slot).wait()

    @pl.when(i + 1 < n_steps)
    def _():
        idx_copy(i + 1, 1 - slot).start()

    def scatter(h_ref, idx_slot):
        for r in range(2 * tm):
            pltpu.make_async_copy(h_ref.at[pl.ds(r % tm, 1)], xs_hbm.at[pl.ds(idx_sm[idx_slot, r], 1)],
                                  sem_out).start(priority=r % 2)
        for _ in range(2):
            pltpu.make_async_copy(h_ref, xs_hbm.at[pl.ds(0, tm)], sem_out).wait()

    for parity in range(2):
        pl.when((i < steps_p) & (slot == parity))(lambda p=parity: scatter(hp_ref, p))
        pl.when((i >= steps_p) & (slot == parity))(lambda p=parity: scatter(hs_ref, p))


def _dispatch(h2_p, h2_s, dest_tbl, pends, pcounts, n_used, n_rows, tm):
    steps_p = h2_p.shape[0] // tm
    steps_s = h2_s.shape[0] // tm
    n_blocks = n_rows // EXPERT_ROWS
    first_spare = (h2_p.shape[0] + h2_s.shape[0]) * TOP_K // EXPERT_ROWS
    grid_spec = pltpu.PrefetchScalarGridSpec(
        num_scalar_prefetch=3, grid=(steps_p + steps_s,),
        in_specs=[pl.BlockSpec(memory_space=pl.ANY),
                  pl.BlockSpec((tm, D_MODEL), lambda i, pe, pc, nu: (jnp.minimum(i, steps_p - 1), 0)),
                  pl.BlockSpec((tm, D_MODEL), lambda i, pe, pc, nu: (jnp.maximum(i - steps_p, 0), 0))],
        out_specs=pl.BlockSpec(memory_space=pl.ANY),
        scratch_shapes=[pltpu.SMEM((2, 2 * tm), jnp.int32), pltpu.VMEM((EXPERT_ROWS, D_MODEL), F32),
                        pltpu.SemaphoreType.DMA((2,)), pltpu.SemaphoreType.DMA])
    return pl.pallas_call(
        functools.partial(_dispatch_kernel, tm=tm, steps_p=steps_p, steps_s=steps_s,
                          first_spare=first_spare, n_blocks=n_blocks),
        grid_spec=grid_spec,
        out_shape=jax.ShapeDtypeStruct((n_rows, D_MODEL), F32),
        compiler_params=pltpu.CompilerParams(dimension_semantics=("arbitrary",),
                                             vmem_limit_bytes=VMEM_LIMIT),
        name="dispatch",
    )(pends, pcounts, n_used, dest_tbl, h2_p, h2_s)


def _experts_kernel(blk_exp_ref, n_used_ref, x_ref, wg_ref, wu_ref, wd_ref, y_ref, wg_sc, wu_sc, wd_sc):
    i = pl.program_id(0)

    @pl.when(i < n_used_ref[0])
    def _():
        prev = blk_exp_ref[jnp.maximum(i - 1, 0)]

        @pl.when((i == 0) | (prev != blk_exp_ref[i]))
        def _():
            wg_sc[...] = wg_ref[0].astype(BF16)
            wu_sc[...] = wu_ref[0].astype(BF16)
            wd_sc[...] = wd_ref[0].astype(BF16)

        xb = x_ref[...].astype(BF16)
        gate = _dot(xb, wg_sc[...])
        up = _dot(xb, wu_sc[...])
        act = (gate * _sigmoid(gate) * up).astype(BF16)
        y_ref[...] = _dot(act, wd_sc[...])

    @pl.when(i >= n_used_ref[0])
    def _():
        y_ref[...] = jnp.zeros(y_ref.shape, F32)


def _experts(xs, blk_exp, n_used, wg, wu, wd):
    n_blocks = int(blk_exp.shape[0])
    wmap = lambda i, be, nu: (be[i], 0, 0)
    xmap = lambda i, be, nu: (jnp.minimum(i, nu[0] - 1), 0)
    grid_spec = pltpu.PrefetchScalarGridSpec(
        num_scalar_prefetch=2,
        grid=(n_blocks,),
        in_specs=[pl.BlockSpec((EXPERT_ROWS, D_MODEL), xmap),
                  pl.BlockSpec((1, D_MODEL, D_EXPERT), wmap),
                  pl.BlockSpec((1, D_MODEL, D_EXPERT), wmap),
                  pl.BlockSpec((1, D_EXPERT, D_MODEL), wmap)],
        out_specs=pl.BlockSpec((EXPERT_ROWS, D_MODEL), lambda i, be, nu: (i, 0)),
        scratch_shapes=[pltpu.VMEM((D_MODEL, D_EXPERT), BF16), pltpu.VMEM((D_MODEL, D_EXPERT), BF16),
                        pltpu.VMEM((D_EXPERT, D_MODEL), BF16)],
    )
    return pl.pallas_call(
        _experts_kernel,
        grid_spec=grid_spec,
        out_shape=jax.ShapeDtypeStruct(xs.shape, F32),
        compiler_params=pltpu.CompilerParams(dimension_semantics=("arbitrary",),
                                             vmem_limit_bytes=VMEM_LIMIT),
        name="experts",
    )(blk_exp, n_used, xs, wg, wu, wd)


def _expert_layout(counts_row, n_tok):
    counts = counts_row[0, N_GROUPS:N_GROUPS + N_EXPERTS].astype(jnp.int32)
    pcounts = (counts + EXPERT_ROWS - 1) // EXPERT_ROWS * EXPERT_ROWS
    pends = jnp.cumsum(pcounts).astype(jnp.int32)
    pstarts = pends - pcounts
    n_blocks = -(-(n_tok * TOP_K) // EXPERT_ROWS) + N_EXPERTS
    blk_start = jnp.arange(n_blocks, dtype=jnp.int32) * EXPERT_ROWS
    blk_exp = jnp.minimum(jnp.sum(blk_start[:, None] >= pends[None, :], axis=1), N_EXPERTS - 1).astype(jnp.int32)
    n_used = (pends[-1:] // EXPERT_ROWS).astype(jnp.int32)
    return pstarts, pends, pcounts, blk_exp, n_used, n_blocks * EXPERT_ROWS


def _dest_kernel(route_ref, start_ref, o_ref, *, td):
    route = route_ref[...]
    tb = route.shape[0]
    lane = lax.broadcasted_iota(jnp.int32, (tb, ROUTE_W), 1)
    diagonal = lax.broadcasted_iota(jnp.int32, (td, td), 0) == lax.broadcasted_iota(jnp.int32, (td, td), 1)
    for k in range(TOP_K):
        expert = route[:, k:k + 1].astype(jnp.int32)
        start = jnp.sum(jnp.where(lane == expert, start_ref[...], 0.0), axis=-1, keepdims=True)
        dest = start + route[:, 2 * TOP_K + k:2 * TOP_K + k + 1]
        for j in range(tb // td):
            as_row = jnp.sum(jnp.where(diagonal, dest[j * td:(j + 1) * td, :], 0.0), axis=0, keepdims=True)
            o_ref[j:j + 1, k * td:(k + 1) * td] = as_row.astype(jnp.int32)


def _dest_table(route, pstarts, td):
    n = route.shape[0]
    tb = math.gcd(n, 8 * td)
    starts = jnp.zeros((1, ROUTE_W), F32).at[0, 0:N_EXPERTS].set(pstarts.astype(F32))
    return pl.pallas_call(
        functools.partial(_dest_kernel, td=td),
        grid=(n // tb,),
        in_specs=[pl.BlockSpec((tb, 8), lambda i: (i, 0)), pl.BlockSpec((1, ROUTE_W), lambda i: (0, 0))],
        out_specs=pl.BlockSpec((tb // td, TOP_K * td), lambda i: (i, 0)),
        out_shape=jax.ShapeDtypeStruct((n // td, TOP_K * td), jnp.int32),
        compiler_params=pltpu.CompilerParams(dimension_semantics=("arbitrary",)),
        name="dest_table",
    )(route, starts)


def _final_kernel(dest_hbm, ys_hbm, y_ref, route_ref, w_ref, o_ref, idx_sm, gbuf, sem_idx, sem_g,
                  *, tm, n_steps):
    i = pl.program_id(0)

    def idx_copy(step, slot):
        return pltpu.make_async_copy(dest_hbm.at[step], idx_sm.at[slot], sem_idx.at[slot])

    def gather(slot):
        for r in range(2 * tm):
            pltpu.make_async_copy(ys_hbm.at[pl.ds(idx_sm[slot, r], 1)], gbuf.at[slot, pl.ds(r, 1)],
                                  sem_g.at[slot]).start(priority=r % 2)

    def step(cur):
        nxt = 1 - cur
        if cur == 0:
            @pl.when(i == 0)
            def _():
                idx_copy(0, 0).start()
                idx_copy(0, 0).wait()
                gather(0)
                if n_steps > 1:
                    idx_copy(1, 1).start()

        if n_steps > 1:
            @pl.when(i + 1 < n_steps)
            def _():
                idx_copy(i + 1, nxt).wait()
                gather(nxt)

            if n_steps > 2:
                @pl.when(i + 2 < n_steps)
                def _():
                    idx_copy(i + 2, cur).start()

        pltpu.make_async_copy(ys_hbm.at[pl.ds(0, 2 * tm)], gbuf.at[cur], sem_g.at[cur]).wait()
        route = route_ref[...]
        rows = gbuf[cur]
        y = y_ref[...] + route[:, 2:3] * rows[0:tm] + route[:, 3:4] * rows[tm:]
        o_ref[...] = _rms(y, w_ref[...])

    pl.when(i % 2 == 0)(lambda: step(0))
    pl.when(i % 2 == 1)(lambda: step(1))


def _final(ymid, route, dest_tbl, ys, w, tm):
    n = ymid.shape[0]
    n_steps = n // tm
    return pl.pallas_call(
        functools.partial(_final_kernel, tm=tm, n_steps=n_steps),
        grid=(n_steps,),
        in_specs=[pl.BlockSpec(memory_space=pl.ANY), pl.BlockSpec(memory_space=pl.ANY),
                  pl.BlockSpec((tm, D_MODEL), lambda i: (i, 0)), pl.BlockSpec((tm, 8), lambda i: (i, 0)),
                  pl.BlockSpec((1, D_MODEL), lambda i: (0, 0))],
        out_specs=pl.BlockSpec((tm, D_MODEL), lambda i: (i, 0)),
        out_shape=jax.ShapeDtypeStruct((n, D_MODEL), F32),
        scratch_shapes=[pltpu.SMEM((2, 2 * tm), jnp.int32), pltpu.VMEM((2, 2 * tm, D_MODEL), F32),
                        pltpu.SemaphoreType.DMA((2,)), pltpu.SemaphoreType.DMA((2,))],
        compiler_params=pltpu.CompilerParams(dimension_semantics=("arbitrary",),
                                             vmem_limit_bytes=VMEM_LIMIT),
        name="final",
    )(dest_tbl, ys, ymid, route, w)


def kernel(x_prompt, x_sample, cache_k, cache_v, state_rec, w_in, lam_q1, lam_k1, lam_q2, lam_k2,
           subln_w, lb_param, gnorm_w, w_out, norm1_w, norm2_w, w_group, w_router,
           w_e_gate, w_e_up, w_e_down, final_w):
    assert w_in.shape[0] == 1 and lb_param.shape[0] == 2, "single-layer model"
    bp, sp, _ = x_prompt.shape
    bs, ts, _ = x_sample.shape
    past = cache_k.shape[2]
    assert sp % CHUNK == 0 and past % CHUNK == 0 and ts <= CHUNK and ts & (ts - 1) == 0
    n_p, n_s = bp * sp, bs * ts
    n_all = n_p + n_s
    tm = 512 if (n_p % 512 == 0 and n_s % 512 == 0) else math.gcd(n_p, n_s)

    w_in_bf = w_in[0].astype(BF16)
    w_out_bf = w_out[0].astype(BF16)
    n1 = norm1_w[0].reshape(1, D_MODEL)
    n2 = norm2_w[0].reshape(1, D_MODEL)
    lam_vecs = jnp.stack([lam_q1[0], lam_k1[0], lam_q2[0], lam_k2[0]]).astype(F32)
    sub_w = subln_w[0].reshape(1, HEAD_W)
    gn_w = gnorm_w[0].reshape(1, HEAD_W)
    w_route = jnp.zeros((D_MODEL, ROUTE_W), F32)
    w_route = w_route.at[:, 0:N_GROUPS].set(w_group[0]).at[:, N_GROUPS:N_GROUPS + N_EXPERTS].set(w_router[0])
    wr_hi = w_route.astype(BF16)
    wr = jnp.concatenate([wr_hi, (w_route - wr_hi.astype(F32)).astype(BF16)], axis=1)

    xp = x_prompt.reshape(n_p, D_MODEL)
    xs = x_sample.reshape(n_s, D_MODEL)

    kf, vf, kb, qr, logf, kr, ir, gr, qt, vt = _inproj(xp, n1, w_in_bf, lb_param, tm, True)
    a_p = _attn_prompt(qt, kb, vt, lam_vecs, sub_w.reshape(HEAD_W, 1), bp, sp)
    chunks = 8 if sp % (8 * CHUNK) == 0 else 1
    r_p, state_p = _hgrn(qr, logf, kr, ir, gr, gn_w, None, bp, sp, CHUNK, chunks)
    ymid_p, h2_p, route_p, counts_p = _merge(xp, a_p, r_p, w_out_bf, n2, wr,
                                             jnp.zeros((1, ROUTE_W), F32), tm)
    k_prompt = kf.reshape(1, bp, sp, A_HEADS, HEAD_W)
    v_prompt = vf.reshape(1, bp, sp, A_HEADS, HEAD_W)

    kf, vf, kb, qr, logf, kr, ir, gr, q, vb = _inproj(xs, n1, w_in_bf, lb_param, tm, False)
    cache_rows = (bs, past * A_HEADS, HEAD_W)
    a_s = _attn_sample(q, cache_k.reshape(cache_rows), cache_v.reshape(cache_rows), kb, vb,
                       lam_vecs, sub_w, bs, ts)
    r_s, state_s = _hgrn(qr, logf, kr, ir, gr, gn_w, state_rec[0], bs, ts, ts, 1)
    ymid_s, h2_s, route_s, counts = _merge(xs, a_s, r_s, w_out_bf, n2, wr, counts_p, tm)
    k_sample = kf.reshape(1, bs, ts, A_HEADS, HEAD_W)
    v_sample = vf.reshape(1, bs, ts, A_HEADS, HEAD_W)

    pstarts, pends, pcounts, blk_exp, n_used, n_rows = _expert_layout(counts, n_all)
    td = min(512, math.gcd(n_p, n_s))
    dest_p = _dest_table(route_p, pstarts, td)
    dest_s = _dest_table(route_s, pstarts, td)
    xs_rows = _dispatch(h2_p, h2_s, jnp.concatenate([dest_p, dest_s], axis=0), pends, pcounts, n_used,
                        n_rows, td)
    ys_rows = _experts(xs_rows, blk_exp, n_used, w_e_gate[0], w_e_up[0], w_e_down[0])

    fw = final_w.reshape(1, D_MODEL)
    y_prompt = _final(ymid_p, route_p, dest_p, ys_rows, fw, td).reshape(bp, sp, D_MODEL)
    y_sample = _final(ymid_s, route_s, dest_s, ys_rows, fw, td).reshape(bs, ts, D_MODEL)
    return (y_prompt, y_sample, k_prompt, v_prompt, state_p[None], k_sample, v_sample, state_s[None])
```

```python
import functools
import math

import numpy as np
import jax
import jax.numpy as jnp
from jax import lax
from jax.experimental import pallas as pl
from jax.experimental.pallas import tpu as pltpu

F32 = jnp.float32
BF16 = jnp.bfloat16

D_MODEL = 1024
RMS_EPS = 1e-6
CHUNK = 64
A_HEADS = 4
A_HEAD_DIM = 64
HEAD_W = 2 * A_HEAD_DIM
KEY_BLOCK = 256
VT_ROWS = HEAD_W + 16
LOG2_E = math.log2(math.e)
R_HEADS = 4
SEG_W = 512
N_SEG = 7
N_GROUPS = 4
EXPERTS_PER_GROUP = 8
N_EXPERTS = N_GROUPS * EXPERTS_PER_GROUP
TOP_K = 2
D_EXPERT = 512
EXPERT_ROWS = 512
ROUTE_W = 128
LAM_INIT = 0.8 - 0.6 * math.exp(-0.3 * 0)
VMEM_LIMIT = 56 * 1024 * 1024


def _sigmoid(x):
    return 1.0 / (1.0 + jnp.exp(-x))


def _dot(a, b):
    return jnp.dot(a, b, preferred_element_type=F32)


def _dot_nt(a, b):
    return lax.dot_general(a, b, (((1,), (1,)), ((), ())), preferred_element_type=F32)


def _dot_tn(a, b):
    return lax.dot_general(a, b, (((0,), (0,)), ((), ())), preferred_element_type=F32)


def _rms(x, w):
    return x * lax.rsqrt(jnp.mean(x * x, axis=-1, keepdims=True) + RMS_EPS) * w


def _inproj_kernel(x_ref, n1_ref, w_ref, lbp_ref, kf_ref, vf_ref, kb_ref, qr_ref, logf_ref, kr_ref,
                   ir_ref, gr_ref, qa_ref, va_ref, h_sc, *, transposed_qv):
    p = lbp_ref[...]
    e = jnp.exp(p - jnp.max(p, axis=0, keepdims=True))
    lb = e[0:1] / jnp.sum(e, axis=0, keepdims=True)
    ones = jnp.ones((VT_ROWS - HEAD_W, KEY_BLOCK), BF16)

    sub = min(KEY_BLOCK, x_ref.shape[0])
    for s in range(x_ref.shape[0] // sub):
        rows = slice(s * sub, (s + 1) * sub)
        h_sc[rows, :] = _rms(x_ref[rows, :], n1_ref[...]).astype(BF16)

        def seg(i, rows=rows):
            return _dot(h_sc[rows, :], w_ref[:, i * SEG_W:(i + 1) * SEG_W])

        q = seg(0) * (A_HEAD_DIM ** -0.5)
        k = seg(1)
        for hd in range(A_HEADS):
            kf_ref[rows, hd, :] = k[:, hd * HEAD_W:(hd + 1) * HEAD_W]
        kb_ref[rows, :] = k.astype(BF16)
        v = seg(2)
        for hd in range(A_HEADS):
            vf_ref[rows, hd, :] = v[:, hd * HEAD_W:(hd + 1) * HEAD_W]
        if transposed_qv:
            qa_ref[s] = (q * LOG2_E).T.astype(BF16)
            vt = v.T.astype(BF16)
            for hd in range(A_HEADS):
                va_ref[s, hd * VT_ROWS:hd * VT_ROWS + HEAD_W, :] = vt[hd * HEAD_W:(hd + 1) * HEAD_W, :]
                va_ref[s, hd * VT_ROWS + HEAD_W:(hd + 1) * VT_ROWS, :] = ones
        else:
            qa_ref[rows, :] = q.astype(BF16)
            va_ref[rows, :] = v.astype(BF16)
        qr = seg(3)
        qr_ref[rows, :] = (qr * _sigmoid(qr)).astype(BF16)
        f = lb + (1.0 - lb) * _sigmoid(seg(4))
        logf_ref[rows, :] = jnp.log(f) * LOG2_E
        kr_ref[rows, :] = (1.0 - f).astype(BF16)
        ir_ref[rows, :] = seg(5).astype(BF16)
        gr_ref[rows, :] = seg(6)


def _inproj(x, n1, w_bf, lbp, tm, transposed_qv):
    n = x.shape[0]
    row = lambda i: (i, 0)
    fix = lambda i: (0, 0)
    out = lambda dt: jax.ShapeDtypeStruct((n, SEG_W), dt)
    ospec = pl.BlockSpec((tm, SEG_W), row)
    hspec = pl.BlockSpec((tm, A_HEADS, HEAD_W), lambda i: (i, 0, 0))
    hout = jax.ShapeDtypeStruct((n, A_HEADS, HEAD_W), F32)
    slabs = tm // KEY_BLOCK
    tspec = lambda rows: pl.BlockSpec((slabs, rows, KEY_BLOCK), lambda i: (i, 0, 0))
    tout = lambda rows: jax.ShapeDtypeStruct((n // KEY_BLOCK, rows, KEY_BLOCK), BF16)
    if transposed_qv:
        qv_specs = [tspec(SEG_W), tspec(A_HEADS * VT_ROWS)]
        qv_shapes = [tout(SEG_W), tout(A_HEADS * VT_ROWS)]
    else:
        qv_specs = [ospec, ospec]
        qv_shapes = [out(BF16), out(BF16)]
    return pl.pallas_call(
        functools.partial(_inproj_kernel, transposed_qv=transposed_qv),
        grid=(n // tm,),
        in_specs=[pl.BlockSpec((tm, D_MODEL), row), pl.BlockSpec((1, D_MODEL), fix),
                  pl.BlockSpec((D_MODEL, N_SEG * SEG_W), fix), pl.BlockSpec(lbp.shape, fix)],
        out_specs=[hspec, hspec] + [ospec] * 6 + qv_specs,
        out_shape=[hout, hout, out(BF16), out(BF16), out(F32), out(BF16), out(BF16), out(F32)] + qv_shapes,
        scratch_shapes=[pltpu.VMEM((tm, D_MODEL), BF16)],
        compiler_params=pltpu.CompilerParams(dimension_semantics=("arbitrary",),
                                             vmem_limit_bytes=VMEM_LIMIT),
        name="inproj",
    )(x, n1, w_bf, lbp)


def _lam_value(lam_ref):
    l = lam_ref[...]
    s1 = jnp.sum(l[0:1] * l[1:2], axis=-1, keepdims=True)
    s2 = jnp.sum(l[2:3] * l[3:4], axis=-1, keepdims=True)
    return jnp.exp(s1) - jnp.exp(s2) + LAM_INIT


def _split_maps(q):
    lane = lax.broadcasted_iota(jnp.int32, q.shape, 1)
    zero = jnp.zeros_like(q)
    return jnp.concatenate([jnp.where(lane < A_HEAD_DIM, q, zero),
                            jnp.where(lane >= A_HEAD_DIM, q, zero)], axis=0)


def _attn_finish(o_num, l, lam, sub_w, tq):
    o = o_num[0:tq] / l[0:tq] - lam * (o_num[tq:] / l[tq:])
    return (_rms(o, sub_w) * (1.0 - LAM_INIT)).astype(BF16)


def _attn_prompt_kernel(lam_ref, sub_ref, bias_ref, qt_ref, k_ref, vt_ref, o_ref,
                        qbd_sc, m_sc, acc_sc, sa_sc, sb_sc, *, qblocks):
    i = pl.program_id(2)
    kb = KEY_BLOCK

    def scores(r, key_block):
        return _dot(k_ref[pl.ds(pl.multiple_of(key_block * kb, kb), kb), :], qbd_sc[r])

    def update(r, key_block, st):
        m_old = m_sc[r:r + 1, :]
        m_new = jnp.maximum(m_old, jnp.max(st, axis=0, keepdims=True))
        p = jnp.exp2(st - m_new).astype(BF16)
        acc_sc[r] = jnp.exp2(m_old - m_new) * acc_sc[r] + _dot(vt_ref[key_block], p)
        m_sc[r:r + 1, :] = m_new

    n_q = 2 * qblocks
    first = i * qblocks
    sub = lax.broadcasted_iota(jnp.int32, (HEAD_W, kb), 0)
    for r in range(n_q):
        qt = qt_ref[r % qblocks]
        keep = (sub < A_HEAD_DIM) if r < qblocks else (sub >= A_HEAD_DIM)
        qbd_sc[r] = jnp.where(keep, qt, jnp.zeros_like(qt))
        sa_sc[r] = scores(r, 0)
    m_sc[...] = jnp.full(m_sc.shape, -jnp.inf, F32)
    acc_sc[...] = jnp.zeros(acc_sc.shape, F32)

    def visible_blocks(start, count):
        lead = 2
        chains = [(j, r) for j in range(count) for r in range(n_q)]
        for c in range(len(chains) + lead):
            if c < len(chains):
                j, r = chains[c]
                (sb_sc if j % 2 == 0 else sa_sc)[r] = scores(r, start + j + 1)
            if c >= lead:
                j, r = chains[c - lead]
                update(r, start + j, (sa_sc if j % 2 == 0 else sb_sc)[r])

    def tile_pair(t, carry):
        visible_blocks(2 * t * qblocks, 2 * qblocks)
        return carry

    lax.fori_loop(0, i // 2, tile_pair, 0)

    def diagonal_tile():
        work = [(kl, r) for kl in range(qblocks) for r in range(n_q) if r % qblocks >= kl]
        buffer_of = lambda kl: sa_sc if kl % 2 == 0 else sb_sc
        ahead = 4
        issued = set()
        for n, (kl, r) in enumerate(work):
            for kl2, r2 in work[n:n + 1 + ahead]:
                if kl2 > 0 and (kl2, r2) not in issued:
                    assert (kl2 - 2, r2) not in work[n:], "score buffer reused before it was read"
                    buffer_of(kl2)[r2] = scores(r2, first + kl2)
                    issued.add((kl2, r2))
            st = buffer_of(kl)[r]
            update(r, first + kl, st + bias_ref[...] if r % qblocks == kl else st)

    @pl.when(i % 2 == 1)
    def _():
        visible_blocks((i - 1) * qblocks, qblocks)
        diagonal_tile()

    pl.when(i % 2 == 0)(diagonal_tile)

    lam = _lam_value(lam_ref)
    for ql in range(qblocks):
        a1 = acc_sc[ql]
        a2 = acc_sc[qblocks + ql]
        ot = a1[0:HEAD_W] / a1[HEAD_W:HEAD_W + 1] - lam * (a2[0:HEAD_W] / a2[HEAD_W:HEAD_W + 1])
        ot = ot * lax.rsqrt(jnp.mean(ot * ot, axis=0, keepdims=True) + RMS_EPS) * sub_ref[...]
        o_ref[ql * kb:(ql + 1) * kb, :] = (ot * (1.0 - LAM_INIT)).T.astype(BF16)


def _attn_prompt(qt, k, vt, lam_vecs, sub_col, batch, seq):
    kb = KEY_BLOCK
    tq = min(1024, seq)
    qblocks = tq // kb
    assert qblocks % 2 == 0, "the key-block loop is unrolled by two"
    nq = seq // tq
    pos = np.arange(kb) // CHUNK
    bias = jnp.asarray(np.where(pos[:, None] <= pos[None, :], 0.0, -np.inf), F32)
    fix = lambda b, h, i: (0, 0)
    return pl.pallas_call(
        functools.partial(_attn_prompt_kernel, qblocks=qblocks),
        grid=(batch, A_HEADS, nq),
        in_specs=[pl.BlockSpec((4, A_HEAD_DIM), fix), pl.BlockSpec((HEAD_W, 1), fix),
                  pl.BlockSpec((kb, kb), fix),
                  pl.BlockSpec((qblocks, HEAD_W, kb), lambda b, h, i: (b * nq + i, h, 0)),
                  pl.BlockSpec((seq, HEAD_W), lambda b, h, i: (b, h)),
                  pl.BlockSpec((seq // kb, VT_ROWS, kb), lambda b, h, i: (b, h, 0))],
        out_specs=pl.BlockSpec((tq, HEAD_W), lambda b, h, i: (b * nq + i, h)),
        out_shape=jax.ShapeDtypeStruct(k.shape, BF16),
        scratch_shapes=[pltpu.VMEM((2 * qblocks, HEAD_W, kb), BF16), pltpu.VMEM((2 * qblocks, kb), F32),
                        pltpu.VMEM((2 * qblocks, VT_ROWS, kb), F32),
                        pltpu.VMEM((2 * qblocks, kb, kb), F32), pltpu.VMEM((2 * qblocks, kb, kb), F32)],
        compiler_params=pltpu.CompilerParams(
            dimension_semantics=("arbitrary", "arbitrary", "arbitrary"), vmem_limit_bytes=VMEM_LIMIT),
        name="attn_prompt",
    )(lam_vecs, sub_col, bias, qt, k, vt)


def _attn_sample_kernel(lam_ref, sub_ref, bias_c_ref, bias_n_ref, q_ref, kc_ref, vc_ref, kn_ref, vn_ref,
                        o_ref, *, t):
    lam = _lam_value(lam_ref)
    heads = [slice(h * HEAD_W, (h + 1) * HEAD_W) for h in range(A_HEADS)]
    qbd = jnp.concatenate([_split_maps(q_ref[:, cs]) for cs in heads], axis=0)
    keys = [kc_ref[0].astype(BF16), jnp.concatenate([kn_ref[:, cs] for cs in heads], axis=0)]
    vals = [vc_ref[0].astype(BF16), jnp.concatenate([vn_ref[:, cs] for cs in heads], axis=0)]
    scores = [_dot_nt(qbd, k) + b[...] for k, b in zip(keys, (bias_c_ref, bias_n_ref))]
    m = functools.reduce(jnp.maximum, [jnp.max(sc, axis=-1, keepdims=True) for sc in scores])
    probs = [jnp.exp(sc - m) for sc in scores]
    l = sum(jnp.sum(p, axis=-1, keepdims=True) for p in probs)
    o_num = sum(_dot(p.astype(BF16), v) for p, v in zip(probs, vals))
    for h, cs in enumerate(heads):
        rows = slice(h * 2 * t, (h + 1) * 2 * t)
        o_ref[:, cs] = _attn_finish(o_num[rows], l[rows], lam, sub_ref[...], t)


def _attn_sample(q, cache_k, cache_v, k_new, v_new, lam_vecs, sub_w, batch, t):
    fix = lambda b: (0, 0)
    row = lambda b: (b, 0)
    cache = pl.BlockSpec((1,) + cache_k.shape[1:], lambda b: (b, 0, 0))
    q_head = np.arange(A_HEADS * 2 * t) // (2 * t)
    off = lambda key_head: jnp.asarray(np.where(q_head[:, None] == key_head[None, :], 0.0, -np.inf), F32)
    bias_c = off(np.arange(cache_k.shape[1]) % A_HEADS)
    bias_n = off(np.arange(A_HEADS * t) // t)
    return pl.pallas_call(
        functools.partial(_attn_sample_kernel, t=t),
        grid=(batch,),
        in_specs=[pl.BlockSpec((4, A_HEAD_DIM), fix), pl.BlockSpec((1, HEAD_W), fix),
                  pl.BlockSpec(bias_c.shape, fix), pl.BlockSpec(bias_n.shape, fix),
                  pl.BlockSpec((t, SEG_W), row),
                  cache, cache, pl.BlockSpec((t, SEG_W), row), pl.BlockSpec((t, SEG_W), row)],
        out_specs=pl.BlockSpec((t, SEG_W), row),
        out_shape=jax.ShapeDtypeStruct(q.shape, BF16),
        compiler_params=pltpu.CompilerParams(dimension_semantics=("arbitrary",),
                                             vmem_limit_bytes=VMEM_LIMIT),
        name="attn_sample",
    )(lam_vecs, sub_w, bias_c, bias_n, q, cache_k, cache_v, k_new, v_new)


def _hgrn_consts(length):
    t = np.arange(length)[:, None]
    j = np.arange(length)[None, :]
    sums = [j <= t]
    masks = [j == t]
    blk = length
    while blk >= 2:
        half = blk // 2
        mid_t = (t // blk) * blk + half
        mid_j = (j // blk) * blk + half
        sums.append(np.where(t >= mid_t, (j >= mid_t) & (j <= t), (j > t) & (j < mid_t)))
        masks.append((t // blk == j // blk) & (t >= mid_t) & (j < mid_j))
        blk = half
    sums = np.tile(np.concatenate(sums, axis=0).astype(np.float32), (1, 2))
    masks = np.concatenate(masks, axis=0).astype(np.float32)
    return jnp.asarray(sums, BF16), jnp.asarray(masks, F32)


def _hgrn_kernel(*refs, length, chunks, group, has_state_in):
    if has_state_in:
        (sums_ref, masks_ref, gn_ref, q_ref, logf_ref, k_ref, v_ref, g_ref, s0_ref,
         r_ref, sout_ref, st_sc) = refs
    else:
        (sums_ref, masks_ref, gn_ref, q_ref, logf_ref, k_ref, v_ref, g_ref,
         r_ref, sout_ref, st_sc) = refs
    step = pl.program_id(1)
    levels = int(math.log2(length))
    L = length

    @pl.when(step == 0)
    def _():
        for h in range(R_HEADS):
            if has_state_in:
                st_sc[h] = s0_ref[0, h].T
            else:
                st_sc[h] = jnp.zeros(st_sc.shape[1:], F32)

    heads = [slice(h * HEAD_W, (h + 1) * HEAD_W) for h in range(R_HEADS)]

    def chunk_group(c, carry):
        rows = [pl.ds(pl.multiple_of((c * group + g) * L, L), L) for g in range(group)]
        expo = []
        for g in range(group):
            logf = logf_ref[rows[g], :]
            hi = logf.astype(BF16)
            lo = (logf - hi.astype(F32)).astype(BF16)
            expo.append(_dot(sums_ref[...], jnp.concatenate([hi, lo], axis=0)))
        streams = [(g, h) for g in range(group) for h in range(R_HEADS)]
        intra, q_dec, upd = {}, {}, {}

        def decay_matrix(g, h):
            cs = heads[h]
            q = q_ref[rows[g], cs]
            k = k_ref[rows[g], cs]
            a = masks_ref[0:L, :] * _dot_nt(q, k)
            for lv in range(levels):
                x = jnp.exp2(expo[g][(1 + lv) * L:(2 + lv) * L, cs]).astype(BF16)
                a = a + masks_ref[(1 + lv) * L:(2 + lv) * L, :] * _dot_nt(q * x, k * x)
            return a

        def state_free_parts(g, h, a):
            cs = heads[h]
            q = q_ref[rows[g], cs]
            k = k_ref[rows[g], cs]
            v = v_ref[rows[g], cs]
            intra[g, h] = _dot(a.astype(BF16), v)
            b = expo[g][0:L, cs]
            q_dec[g, h] = q * jnp.exp2(b).astype(BF16)
            k_dec = k * jnp.exp2(b[L - 1:L, :] - b).astype(BF16)
            upd[g, h] = _dot_tn(v, k_dec)

        for g, h in streams:
            state_free_parts(g, h, decay_matrix(g, h))
        for g, h in streams:
            cs = heads[h]
            st = st_sc[h]
            o = intra[g, h] + _dot_nt(q_dec[g, h], st.astype(BF16))
            st_sc[h] = st * jnp.exp2(expo[g][L - 1:L, cs]) + upd[g, h]
            gate = g_ref[rows[g], cs]
            r_ref[rows[g], cs] = (_rms(o, gn_ref[...]) * (gate * _sigmoid(gate))).astype(BF16)
        return carry

    lax.fori_loop(0, chunks // group, chunk_group, 0)

    @pl.when(step == pl.num_programs(1) - 1)
    def _():
        for h in range(R_HEADS):
            sout_ref[0, h] = st_sc[h].T


def _hgrn(q, logf, k, v, g, gn_w, state_in, batch, seq, length, chunks):
    sums, masks = _hgrn_consts(length)
    group = math.gcd(chunks, 8)
    tm = length * chunks
    steps = seq // tm
    fix = lambda b, s: (0, 0)
    row = lambda b, s: (b * steps + s, 0)
    tok = pl.BlockSpec((tm, SEG_W), row)
    in_specs = [pl.BlockSpec(sums.shape, fix), pl.BlockSpec(masks.shape, fix), pl.BlockSpec((1, HEAD_W), fix),
                tok, tok, tok, tok, tok]
    args = [sums, masks, gn_w, q, logf, k, v, g]
    state_spec = pl.BlockSpec((1, R_HEADS, HEAD_W, HEAD_W), lambda b, s: (b, 0, 0, 0))
    if state_in is not None:
        in_specs.append(state_spec)
        args.append(state_in)
    return pl.pallas_call(
        functools.partial(_hgrn_kernel, length=length, chunks=chunks, group=group,
                          has_state_in=state_in is not None),
        grid=(batch, steps),
        in_specs=in_specs,
        out_specs=[tok, state_spec],
        out_shape=[jax.ShapeDtypeStruct(q.shape, BF16),
                   jax.ShapeDtypeStruct((batch, R_HEADS, HEAD_W, HEAD_W), F32)],
        scratch_shapes=[pltpu.VMEM((R_HEADS, HEAD_W, HEAD_W), F32)],
        compiler_params=pltpu.CompilerParams(dimension_semantics=("arbitrary", "arbitrary"),
                                             vmem_limit_bytes=VMEM_LIMIT),
        name="hgrn",
    )(*args)


def _merge_kernel(x_ref, a_ref, r_ref, wo_ref, n2_ref, wr_ref, tri_ref, cnt0_ref,
                  ymid_ref, h2_ref, route_ref, cnt_ref, cnt_sc):
    step = pl.program_id(0)

    @pl.when(step == 0)
    def _():
        cnt_sc[...] = cnt0_ref[...]

    sub = tri_ref.shape[0]
    blocks = [slice(s * sub, (s + 1) * sub) for s in range(x_ref.shape[0] // sub)]
    half = a_ref.shape[1]
    for rows in blocks:
        ymid_ref[rows, :] = (x_ref[rows, :] + _dot(a_ref[rows, :], wo_ref[0:half, :])
                             + _dot(r_ref[rows, :], wo_ref[half:, :]))
    for rows in blocks:
        _route_rows(rows, n2_ref, wr_ref, tri_ref, ymid_ref, h2_ref, route_ref, cnt_sc)
    cnt_ref[...] = cnt_sc[...]


def _route_rows(rows, n2_ref, wr_ref, tri_ref, ymid_ref, h2_ref, route_ref, cnt_sc):
    hn = _rms(ymid_ref[rows, :], n2_ref[...])
    h2_ref[rows, :] = hn
    hi = hn.astype(BF16)
    lo = (hn - hi.astype(F32)).astype(BF16)
    logits = (_dot(hi, wr_ref[:, :ROUTE_W]) + _dot(lo, wr_ref[:, :ROUTE_W])) + _dot(hi, wr_ref[:, ROUTE_W:])
    lane = lax.broadcasted_iota(jnp.int32, logits.shape, 1)
    neg = -jnp.inf
    big = jnp.int32(ROUTE_W)

    def top1(mask):
        val = jnp.max(jnp.where(mask, logits, neg), axis=-1, keepdims=True)
        idx = jnp.min(jnp.where(mask & (logits == val), lane, big), axis=-1, keepdims=True)
        return val, idx

    is_group = lane < N_GROUPS
    g_max, g_sel = top1(is_group)
    p_group = 1.0 / jnp.sum(jnp.where(is_group, jnp.exp(logits - g_max), 0.0), axis=-1, keepdims=True)
    first = N_GROUPS + g_sel * EXPERTS_PER_GROUP
    in_group = (lane >= first) & (lane < first + EXPERTS_PER_GROUP)
    v1, i1 = top1(in_group)
    v2, i2 = top1(in_group & (lane != i1))
    e2 = jnp.exp(v2 - v1)
    gate1 = p_group * (1.0 / (1.0 + e2))
    gate2 = p_group * (e2 / (1.0 + e2))
    hot1 = lane == i1
    hot2 = lane == i2
    hot = jnp.where(hot1 | hot2, 1.0, 0.0)
    before = cnt_sc[...] + _dot(tri_ref[...], hot.astype(BF16))
    rank1 = jnp.sum(jnp.where(hot1, before, 0.0), axis=-1, keepdims=True)
    rank2 = jnp.sum(jnp.where(hot2, before, 0.0), axis=-1, keepdims=True)
    cnt_sc[...] = cnt_sc[...] + jnp.sum(hot, axis=0, keepdims=True)
    col = lax.broadcasted_iota(jnp.int32, (hot.shape[0], route_ref.shape[1]), 1)
    cols = [(i1 - N_GROUPS).astype(F32), (i2 - N_GROUPS).astype(F32), gate1, gate2, rank1, rank2]
    route = jnp.zeros(col.shape, F32)
    for c, val in enumerate(cols):
        route = jnp.where(col == c, val, route)
    route_ref[rows, :] = route


def _merge(x, a, r, wo_bf, n2, wr, counts_in, tm):
    n = x.shape[0]
    row = lambda i: (i, 0)
    fix = lambda i: (0, 0)
    sub = min(256, tm)
    tri = jnp.asarray(np.tril(np.ones((sub, sub), np.float32), -1), BF16)
    return pl.pallas_call(
        _merge_kernel,
        grid=(n // tm,),
        in_specs=[pl.BlockSpec((tm, D_MODEL), row), pl.BlockSpec((tm, SEG_W), row),
                  pl.BlockSpec((tm, SEG_W), row), pl.BlockSpec((D_MODEL, D_MODEL), fix),
                  pl.BlockSpec((1, D_MODEL), fix), pl.BlockSpec((D_MODEL, 2 * ROUTE_W), fix),
                  pl.BlockSpec((sub, sub), fix),
                  pl.BlockSpec((1, ROUTE_W), fix)],
        out_specs=[pl.BlockSpec((tm, D_MODEL), row), pl.BlockSpec((tm, D_MODEL), row),
                   pl.BlockSpec((tm, 8), row), pl.BlockSpec((1, ROUTE_W), fix)],
        out_shape=[jax.ShapeDtypeStruct((n, D_MODEL), F32), jax.ShapeDtypeStruct((n, D_MODEL), F32),
                   jax.ShapeDtypeStruct((n, 8), F32), jax.ShapeDtypeStruct((1, ROUTE_W), F32)],
        scratch_shapes=[pltpu.VMEM((1, ROUTE_W), F32)],
        compiler_params=pltpu.CompilerParams(dimension_semantics=("arbitrary",),
                                             vmem_limit_bytes=VMEM_LIMIT),
        name="merge",
    )(x, a, r, wo_bf, n2, wr, tri, counts_in)


DISPATCH_RING = 3


def _dispatch_kernel(pend_ref, pcnt_ref, n_used_ref, dest_hbm, hp_hbm, hs_hbm, xs_hbm,
                     idx_0, idx_1, idx_2, hbuf, zero_sc, sem_idx, sem_load, sem_out, sem_fill,
                     *, tm, steps_p, steps_s, first_spare, n_blocks):
    i = pl.program_id(0)
    n_steps = steps_p + steps_s
    idx_sm = (idx_0, idx_1, idx_2)

    def idx_copy(step, slot):
        return pltpu.make_async_copy(dest_hbm.at[step], idx_sm[slot], sem_idx.at[slot])

    def start_load(step, slot):
        def from_prompt():
            row0 = pl.multiple_of(step * tm, tm)
            pltpu.make_async_copy(hp_hbm.at[pl.ds(row0, tm)], hbuf.at[slot], sem_load.at[slot]).start()

        def from_sample():
            row0 = pl.multiple_of((step - steps_p) * tm, tm)
            pltpu.make_async_copy(hs_hbm.at[pl.ds(row0, tm)], hbuf.at[slot], sem_load.at[slot]).start()

        pl.when(step < steps_p)(from_prompt)
        pl.when(step >= steps_p)(from_sample)

    def wait_load(slot):
        pltpu.make_async_copy(hp_hbm.at[pl.ds(0, tm)], hbuf.at[slot], sem_load.at[slot]).wait()

    def wait_scatter(slot):
        for _ in range(2):
            pltpu.make_async_copy(hbuf.at[slot], xs_hbm.at[pl.ds(0, tm)], sem_out.at[slot]).wait()

    @pl.when(i == 0)
    def _():
        idx_copy(0, 0).start()
        start_load(0, 0)
        zero_sc[...] = jnp.zeros(zero_sc.shape, F32)

        def fill(start):
            start = pl.multiple_of(start, EXPERT_ROWS)
            return pltpu.make_async_copy(zero_sc, xs_hbm.at[pl.ds(start, EXPERT_ROWS)], sem_fill)

        for action in ("start", "wait"):
            for e in range(N_EXPERTS):
                pl.when(pcnt_ref[e] > 0)(
                    lambda e=e: getattr(fill(pend_ref[e] - EXPERT_ROWS), action)())
            for blk in range(first_spare, n_blocks):
                pl.when(blk >= n_used_ref[0])(
                    lambda blk=blk: getattr(fill(blk * EXPERT_ROWS), action)())

    def step(cur):
        nxt = (cur + 1) % DISPATCH_RING
        prv = (cur - 1) % DISPATCH_RING

        pl.when(i >= 2)(lambda: wait_scatter(nxt))

        @pl.when(i + 1 < n_steps)
        def _():
            start_load(i + 1, nxt)
            idx_copy(i + 1, nxt).start()

        idx_copy(i, cur).wait()
        wait_load(cur)
        for r in range(2 * tm):
            pltpu.make_async_copy(hbuf.at[cur, pl.ds(r % tm, 1)], xs_hbm.at[pl.ds(idx_sm[cur][r], 1)],
                                  sem_out.at[cur]).start(priority=r % 2)

        @pl.when(i == n_steps - 1)
        def _():
            pl.when(i >= 1)(lambda: wait_scatter(prv))
            wait_scatter(cur)

    for cur in range(DISPATCH_RING):
        pl.when(i % DISPATCH_RING == cur)(lambda cur=cur: step(cur))


def _dispatch(h2_p, h2_s, dest_tbl, pends, pcounts, n_used, n_rows, tm):
    steps_p = h2_p.shape[0] // tm
    steps_s = h2_s.shape[0] // tm
    n_blocks = n_rows // EXPERT_ROWS
    first_spare = (h2_p.shape[0] + h2_s.shape[0]) * TOP_K // EXPERT_ROWS
    grid_spec = pltpu.PrefetchScalarGridSpec(
        num_scalar_prefetch=3, grid=(steps_p + steps_s,),
        in_specs=[pl.BlockSpec(memory_space=pl.ANY)] * 3,
        out_specs=pl.BlockSpec(memory_space=pl.ANY),
        scratch_shapes=[pltpu.SMEM((2 * tm,), jnp.int32)] * DISPATCH_RING + [
                        pltpu.VMEM((DISPATCH_RING, tm, D_MODEL), F32),
                        pltpu.VMEM((EXPERT_ROWS, D_MODEL), F32),
                        pltpu.SemaphoreType.DMA((DISPATCH_RING,)), pltpu.SemaphoreType.DMA((DISPATCH_RING,)),
                        pltpu.SemaphoreType.DMA((DISPATCH_RING,)), pltpu.SemaphoreType.DMA])
    return pl.pallas_call(
        functools.partial(_dispatch_kernel, tm=tm, steps_p=steps_p, steps_s=steps_s,
                          first_spare=first_spare, n_blocks=n_blocks),
        grid_spec=grid_spec,
        out_shape=jax.ShapeDtypeStruct((n_rows, D_MODEL), F32),
        compiler_params=pltpu.CompilerParams(dimension_semantics=("arbitrary",),
                                             vmem_limit_bytes=VMEM_LIMIT),
        name="dispatch",
    )(pends, pcounts, n_used, dest_tbl, h2_p, h2_s)


def _experts_kernel(blk_exp_ref, n_used_ref, x_ref, wg_ref, wu_ref, wd_ref, y_ref, wg_sc, wu_sc, wd_sc):
    i = pl.program_id(0)

    @pl.when(i < n_used_ref[0])
    def _():
        prev = blk_exp_ref[jnp.maximum(i - 1, 0)]

        @pl.when((i == 0) | (prev != blk_exp_ref[i]))
        def _():
            wg_sc[...] = wg_ref[0].astype(BF16)
            wu_sc[...] = wu_ref[0].astype(BF16)
            wd_sc[...] = wd_ref[0].astype(BF16)

        xb = x_ref[...].astype(BF16)
        gate = _dot(xb, wg_sc[...])
        up = _dot(xb, wu_sc[...])
        act = (gate * _sigmoid(gate) * up).astype(BF16)
        y_ref[...] = _dot(act, wd_sc[...])

    @pl.when(i >= n_used_ref[0])
    def _():
        y_ref[...] = jnp.zeros(y_ref.shape, F32)


def _experts(xs, blk_exp, n_used, wg, wu, wd):
    n_blocks = int(blk_exp.shape[0])
    wmap = lambda i, be, nu: (be[i], 0, 0)
    xmap = lambda i, be, nu: (jnp.minimum(i, nu[0] - 1), 0)
    grid_spec = pltpu.PrefetchScalarGridSpec(
        num_scalar_prefetch=2,
        grid=(n_blocks,),
        in_specs=[pl.BlockSpec((EXPERT_ROWS, D_MODEL), xmap),
                  pl.BlockSpec((1, D_MODEL, D_EXPERT), wmap),
                  pl.BlockSpec((1, D_MODEL, D_EXPERT), wmap),
                  pl.BlockSpec((1, D_EXPERT, D_MODEL), wmap)],
        out_specs=pl.BlockSpec((EXPERT_ROWS, D_MODEL), lambda i, be, nu: (i, 0)),
        scratch_shapes=[pltpu.VMEM((D_MODEL, D_EXPERT), BF16), pltpu.VMEM((D_MODEL, D_EXPERT), BF16),
                        pltpu.VMEM((D_EXPERT, D_MODEL), BF16)],
    )
    return pl.pallas_call(
        _experts_kernel,
        grid_spec=grid_spec,
        out_shape=jax.ShapeDtypeStruct(xs.shape, F32),
        compiler_params=pltpu.CompilerParams(dimension_semantics=("arbitrary",),
                                             vmem_limit_bytes=VMEM_LIMIT),
        name="experts",
    )(blk_exp, n_used, xs, wg, wu, wd)


def _expert_layout(counts_row, n_tok):
    counts = counts_row[0, N_GROUPS:N_GROUPS + N_EXPERTS].astype(jnp.int32)
    pcounts = (counts + EXPERT_ROWS - 1) // EXPERT_ROWS * EXPERT_ROWS
    pends = jnp.cumsum(pcounts).astype(jnp.int32)
    pstarts = pends - pcounts
    n_blocks = -(-(n_tok * TOP_K) // EXPERT_ROWS) + N_EXPERTS
    blk_start = jnp.arange(n_blocks, dtype=jnp.int32) * EXPERT_ROWS
    blk_exp = jnp.minimum(jnp.sum(blk_start[:, None] >= pends[None, :], axis=1), N_EXPERTS - 1).astype(jnp.int32)
    n_used = (pends[-1:] // EXPERT_ROWS).astype(jnp.int32)
    return pstarts, pends, pcounts, blk_exp, n_used, n_blocks * EXPERT_ROWS


def _dest_kernel(route_ref, start_ref, o_ref, *, td):
    route = route_ref[...]
    tb = route.shape[0]
    lane = lax.broadcasted_iota(jnp.int32, (tb, ROUTE_W), 1)
    diagonal = lax.broadcasted_iota(jnp.int32, (td, td), 0) == lax.broadcasted_iota(jnp.int32, (td, td), 1)
    for k in range(TOP_K):
        expert = route[:, k:k + 1].astype(jnp.int32)
        start = jnp.sum(jnp.where(lane == expert, start_ref[...], 0.0), axis=-1, keepdims=True)
        dest = start + route[:, 2 * TOP_K + k:2 * TOP_K + k + 1]
        for j in range(tb // td):
            as_row = jnp.sum(jnp.where(diagonal, dest[j * td:(j + 1) * td, :], 0.0), axis=0, keepdims=True)
            o_ref[j:j + 1, k * td:(k + 1) * td] = as_row.astype(jnp.int32)


def _dest_table(route, pstarts, td):
    n = route.shape[0]
    tb = math.gcd(n, 8 * td)
    starts = jnp.zeros((1, ROUTE_W), F32).at[0, 0:N_EXPERTS].set(pstarts.astype(F32))
    return pl.pallas_call(
        functools.partial(_dest_kernel, td=td),
        grid=(n // tb,),
        in_specs=[pl.BlockSpec((tb, 8), lambda i: (i, 0)), pl.BlockSpec((1, ROUTE_W), lambda i: (0, 0))],
        out_specs=pl.BlockSpec((tb // td, TOP_K * td), lambda i: (i, 0)),
        out_shape=jax.ShapeDtypeStruct((n // td, TOP_K * td), jnp.int32),
        compiler_params=pltpu.CompilerParams(dimension_semantics=("arbitrary",)),
        name="dest_table",
    )(route, starts)


def _final_kernel(dest_hbm, ys_hbm, y_ref, route_ref, w_ref, o_ref, idx_sm, gbuf, sem_idx, sem_g,
                  *, tm, n_steps):
    i = pl.program_id(0)

    def idx_copy(step, slot):
        return pltpu.make_async_copy(dest_hbm.at[step], idx_sm.at[slot], sem_idx.at[slot])

    def gather(slot):
        for r in range(2 * tm):
            pltpu.make_async_copy(ys_hbm.at[pl.ds(idx_sm[slot, r], 1)], gbuf.at[slot, pl.ds(r, 1)],
                                  sem_g.at[slot]).start(priority=r % 2)

    def step(cur):
        nxt = 1 - cur
        if cur == 0:
            @pl.when(i == 0)
            def _():
                idx_copy(0, 0).start()
                idx_copy(0, 0).wait()
                gather(0)
                if n_steps > 1:
                    idx_copy(1, 1).start()

        if n_steps > 1:
            @pl.when(i + 1 < n_steps)
            def _():
                idx_copy(i + 1, nxt).wait()
                gather(nxt)

            if n_steps > 2:
                @pl.when(i + 2 < n_steps)
                def _():
                    idx_copy(i + 2, cur).start()

        pltpu.make_async_copy(ys_hbm.at[pl.ds(0, 2 * tm)], gbuf.at[cur], sem_g.at[cur]).wait()
        route = route_ref[...]
        rows = gbuf[cur]
        y = y_ref[...] + route[:, 2:3] * rows[0:tm] + route[:, 3:4] * rows[tm:]
        o_ref[...] = _rms(y, w_ref[...])

    pl.when(i % 2 == 0)(lambda: step(0))
    pl.when(i % 2 == 1)(lambda: step(1))


def _final(ymid, route, dest_tbl, ys, w, tm):
    n = ymid.shape[0]
    n_steps = n // tm
    return pl.pallas_call(
        functools.partial(_final_kernel, tm=tm, n_steps=n_steps),
        grid=(n_steps,),
        in_specs=[pl.BlockSpec(memory_space=pl.ANY), pl.BlockSpec(memory_space=pl.ANY),
                  pl.BlockSpec((tm, D_MODEL), lambda i: (i, 0)), pl.BlockSpec((tm, 8), lambda i: (i, 0)),
                  pl.BlockSpec((1, D_MODEL), lambda i: (0, 0))],
        out_specs=pl.BlockSpec((tm, D_MODEL), lambda i: (i, 0)),
        out_shape=jax.ShapeDtypeStruct((n, D_MODEL), F32),
        scratch_shapes=[pltpu.SMEM((2, 2 * tm), jnp.int32), pltpu.VMEM((2, 2 * tm, D_MODEL), F32),
                        pltpu.SemaphoreType.DMA((2,)), pltpu.SemaphoreType.DMA((2,))],
        compiler_params=pltpu.CompilerParams(dimension_semantics=("arbitrary",),
                                             vmem_limit_bytes=VMEM_LIMIT),
        name="final",
    )(dest_tbl, ys, ymid, route, w)


def kernel(x_prompt, x_sample, cache_k, cache_v, state_rec, w_in, lam_q1, lam_k1, lam_q2, lam_k2,
           subln_w, lb_param, gnorm_w, w_out, norm1_w, norm2_w, w_group, w_router,
           w_e_gate, w_e_up, w_e_down, final_w):
    assert w_in.shape[0] == 1 and lb_param.shape[0] == 2, "single-layer model"
    bp, sp, _ = x_prompt.shape
    bs, ts, _ = x_sample.shape
    past = cache_k.shape[2]
    assert sp % CHUNK == 0 and past % CHUNK == 0 and ts <= CHUNK and ts & (ts - 1) == 0
    n_p, n_s = bp * sp, bs * ts
    n_all = n_p + n_s
    tm = 512 if (n_p % 512 == 0 and n_s % 512 == 0) else math.gcd(n_p, n_s)

    w_in_bf = w_in[0].astype(BF16)
    w_out_bf = w_out[0].astype(BF16)
    n1 = norm1_w[0].reshape(1, D_MODEL)
    n2 = norm2_w[0].reshape(1, D_MODEL)
    lam_vecs = jnp.stack([lam_q1[0], lam_k1[0], lam_q2[0], lam_k2[0]]).astype(F32)
    sub_w = subln_w[0].reshape(1, HEAD_W)
    gn_w = gnorm_w[0].reshape(1, HEAD_W)
    w_route = jnp.zeros((D_MODEL, ROUTE_W), F32)
    w_route = w_route.at[:, 0:N_GROUPS].set(w_group[0]).at[:, N_GROUPS:N_GROUPS + N_EXPERTS].set(w_router[0])
    wr_hi = w_route.astype(BF16)
    wr = jnp.concatenate([wr_hi, (w_route - wr_hi.astype(F32)).astype(BF16)], axis=1)

    xp = x_prompt.reshape(n_p, D_MODEL)
    xs = x_sample.reshape(n_s, D_MODEL)

    kf, vf, kb, qr, logf, kr, ir, gr, qt, vt = _inproj(xp, n1, w_in_bf, lb_param, tm, True)
    a_p = _attn_prompt(qt, kb, vt, lam_vecs, sub_w.reshape(HEAD_W, 1), bp, sp)
    chunks = 8 if sp % (8 * CHUNK) == 0 else 1
    r_p, state_p = _hgrn(qr, logf, kr, ir, gr, gn_w, None, bp, sp, CHUNK, chunks)
    ymid_p, h2_p, route_p, counts_p = _merge(xp, a_p, r_p, w_out_bf, n2, wr,
                                             jnp.zeros((1, ROUTE_W), F32), tm)
    k_prompt = kf.reshape(1, bp, sp, A_HEADS, HEAD_W)
    v_prompt = vf.reshape(1, bp, sp, A_HEADS, HEAD_W)

    kf, vf, kb, qr, logf, kr, ir, gr, q, vb = _inproj(xs, n1, w_in_bf, lb_param, tm, False)
    cache_rows = (bs, past * A_HEADS, HEAD_W)
    a_s = _attn_sample(q, cache_k.reshape(cache_rows), cache_v.reshape(cache_rows), kb, vb,
                       lam_vecs, sub_w, bs, ts)
    r_s, state_s = _hgrn(qr, logf, kr, ir, gr, gn_w, state_rec[0], bs, ts, ts, 1)
    ymid_s, h2_s, route_s, counts = _merge(xs, a_s, r_s, w_out_bf, n2, wr, counts_p, tm)
    k_sample = kf.reshape(1, bs, ts, A_HEADS, HEAD_W)
    v_sample = vf.reshape(1, bs, ts, A_HEADS, HEAD_W)

    pstarts, pends, pcounts, blk_exp, n_used, n_rows = _expert_layout(counts, n_all)
    td = min(512, math.gcd(n_p, n_s))
    dest_p = _dest_table(route_p, pstarts, td)
    dest_s = _dest_table(route_s, pstarts, td)
    xs_rows = _dispatch(h2_p, h2_s, jnp.concatenate([dest_p, dest_s], axis=0), pends, pcounts, n_used,
                        n_rows, td)
    ys_rows = _experts(xs_rows, blk_exp, n_used, w_e_gate[0], w_e_up[0], w_e_down[0])

    fw = final_w.reshape(1, D_MODEL)
    y_prompt = _final(ymid_p, route_p, dest_p, ys_rows, fw, td).reshape(bp, sp, D_MODEL)
    y_sample = _final(ymid_s, route_s, dest_s, ys_rows, fw, td).reshape(bs, ts, D_MODEL)
    return (y_prompt, y_sample, k_prompt, v_prompt, state_p[None], k_sample, v_sample, state_s[None])
```
